```python
import jax, jax.numpy as jnp
from jax import lax
import numpy as np

D_MODEL = 1024
BATCH = 2
SEQ = 8192
DEPTH = 2
DEC_BATCH = 128
DEC_SEQ = 8
PAST_LEN = 2048
PAGE_SIZE = 128

A_PATTERNS = ((128, 1), (512, 4), (2048, 16))
A_GROUPS = 3
A_HEADS = 4
A_HEAD_DIM = 128
A_GROUP_WIDTH = A_HEADS * A_HEAD_DIM
A_WIDTH = A_HEADS * A_HEAD_DIM
B_HEADS = 8
B_KEY_DIM = D_MODEL // B_HEADS
B_VAL_DIM = D_MODEL // B_HEADS
B_KEY_WIDTH = B_HEADS * B_KEY_DIM
B_WIDTH = B_HEADS * B_VAL_DIM
B_CHUNK = 64
C_WINDOWS = (2, 4, 8, 16)
C_GROUPS = 4
C_GROUP_DIM = 128
C_WIDTH = C_GROUPS * C_GROUP_DIM
C_HIST = 15
N_BRANCH = 3
EPS = 1e-6
NEG = -1e30
IN_SPLITS = (A_GROUPS * A_GROUP_WIDTH, A_GROUPS * A_GROUP_WIDTH, A_GROUPS * A_GROUP_WIDTH, A_WIDTH,
             B_KEY_WIDTH, B_KEY_WIDTH, B_WIDTH, B_WIDTH, C_WIDTH, C_WIDTH, N_BRANCH * D_MODEL)
IN_WIDTH = 13312
F32 = jnp.float32

kernel_name = 'hybrid_dilated_hgrn2_pool_decode_step'


def rmsnorm(x, g):
    xf = x.astype(F32)
    y = xf * lax.rsqrt(jnp.mean(xf * xf, axis=-1, keepdims=True) + EPS)
    return (y * g.astype(F32)).astype(x.dtype)


def split_in(z):
    offs = [int(o) for o in np.cumsum(IN_SPLITS)[:-1]]
    return jnp.split(z, offs, axis=-1)


def dilated_attn_prompt(q, k, v, dilation, n_steps):
    b, s, h, e = q.shape
    L = s // dilation
    nb = -(-L // n_steps)
    Lp = nb * n_steps

    def to_blocks(t):
        t = t.reshape(b, L, dilation, h, e).transpose(0, 2, 1, 3, 4)
        t = jnp.pad(t, ((0, 0), (0, 0), (0, Lp - L), (0, 0), (0, 0)))
        return t.reshape(b, dilation, nb, n_steps, h, e)

    def with_prev(t):
        prev = jnp.pad(t, ((0, 0), (0, 0), (1, 0), (0, 0), (0, 0), (0, 0)))[:, :, :-1]
        return jnp.concatenate([prev, t], axis=3)

    qb = to_blocks(q)
    kk = with_prev(to_blocks(k))
    vv = with_prev(to_blocks(v))
    scores = jnp.einsum('brnqhe,brnkhe->brnhqk', qb, kk, preferred_element_type=F32) * (e ** -0.5)
    qi = jnp.arange(n_steps)[:, None]
    kj = jnp.arange(2 * n_steps)[None, :]
    band = (kj >= qi) & (kj <= qi + n_steps)
    blk = jnp.arange(nb)[:, None, None]
    valid = band[None] & ((blk > 0) | (kj[None] >= n_steps))
    scores = jnp.where(valid[None, None, :, None], scores, NEG)
    lse = jax.nn.logsumexp(scores, axis=-1)
    p = jnp.exp(scores - lse[..., None])
    o = jnp.einsum('brnhqk,brnkhe->brnqhe', p, vv.astype(F32))
    o = o.reshape(b, dilation, Lp, h, e)[:, :, :L].transpose(0, 2, 1, 3, 4).reshape(b, s, h, e)
    lse = lse.transpose(0, 1, 2, 4, 3).reshape(b, dilation, Lp, h)[:, :, :L]
    lse = lse.transpose(0, 2, 1, 3).reshape(b, s, h)
    return o, lse


def dilated_attn_sample(q, k_all, v_all, dilation, n_steps):
    b, T, h, e = q.shape
    n_ctx = k_all.shape[1]
    idx = (n_ctx - T + jnp.arange(T))[:, None] - dilation * jnp.arange(n_steps + 1)[None, :]
    valid = idx >= 0
    idx_c = jnp.maximum(idx, 0)
    kg = k_all[:, idx_c]
    vg = v_all[:, idx_c]
    scores = jnp.einsum('bthe,btjhe->bthj', q, kg, preferred_element_type=F32) * (e ** -0.5)
    scores = jnp.where(valid[None, :, None, :], scores, NEG)
    lse = jax.nn.logsumexp(scores, axis=-1)
    p = jnp.exp(scores - lse[..., None])
    o = jnp.einsum('bthj,btjhe->bthe', p, vg.astype(F32))
    return o, lse


def hgrn2_chunked(q, k, v, logf, s0):
    b, L, h, dk = q.shape
    dv = v.shape[-1]
    c = L if L <= B_CHUNK else B_CHUNK
    n = L // c

    def chunks(t):
        return t.astype(F32).reshape(b, n, c, h, t.shape[-1]).transpose(1, 0, 3, 2, 4)

    causal = jnp.tril(jnp.ones((c, c), dtype=bool))[:, :, None]

    def step(S, inp):
        qi, ki, vi, gi = inp
        A = jnp.cumsum(gi, axis=2)
        diff = A[:, :, :, None, :] - A[:, :, None, :, :]
        decay = jnp.where(causal, jnp.exp(jnp.where(causal, diff, 0.0)), 0.0)
        att = jnp.einsum('bhtk,bhtsk,bhsk->bhts', qi, decay, ki)
        o = jnp.einsum('bhts,bhsv->bhtv', att, vi) + jnp.einsum('bhtk,bhkv->bhtv', qi * jnp.exp(A), S)
        A_last = A[:, :, -1:, :]
        S = jnp.exp(A_last[:, :, 0, :, None]) * S + jnp.einsum('bhsk,bhsv->bhkv', ki * jnp.exp(A_last - A), vi)
        return S, o

    S, o = lax.scan(step, s0.astype(F32), (chunks(q), chunks(k), chunks(v), chunks(logf)))
    o = o.transpose(1, 0, 3, 2, 4).reshape(b, L, h, dv)
    return o, S


def multiscale_pool(u, hist, first_pos, pool_w, pool_scale):
    b, L, _ = u.shape
    ext = jnp.concatenate([hist.astype(u.dtype), u], axis=1)
    cs = jnp.pad(jnp.cumsum(ext.astype(F32), axis=1), ((0, 0), (1, 0), (0, 0)))
    pos = first_pos + jnp.arange(L)
    means = []
    for g, win in enumerate(C_WINDOWS):
        lo, hi = g * C_GROUP_DIM, (g + 1) * C_GROUP_DIM
        s = cs[:, C_HIST + 1:C_HIST + 1 + L, lo:hi] - cs[:, C_HIST + 1 - win:C_HIST + 1 - win + L, lo:hi]
        cnt = jnp.minimum(pos + 1, win).astype(F32)
        means.append(s / cnt[None, :, None])
    pooled = jnp.concatenate(means, axis=-1) - u.astype(F32)
    mixed = jnp.einsum('blgc,gcd->blgd', pooled.reshape(b, L, C_GROUPS, C_GROUP_DIM), pool_w.astype(F32))
    out = mixed.reshape(b, L, C_WIDTH) * pool_scale.astype(F32)
    return out, ext[:, -C_HIST:]


def mixer_layer(x, kv_hist, s0, pool_hist, first_pos, lb, norm_pre, norm_post, w_in, hgrn_norm,
                pool_w, pool_scale, w_br_a, w_br_b, w_br_c, w_out):
    b, L, _ = x.shape
    dt = x.dtype
    xn = rmsnorm(x, norm_pre)
    z = jnp.einsum('bld,dn->bln', xn, w_in)
    aq, ak, av, ag, bq, bfg, bi, bg, cu, cg, mg = split_in(z)

    shp = (b, L, A_GROUPS, A_HEADS, A_HEAD_DIM)
    aq, ak, av = aq.reshape(shp), ak.reshape(shp), av.reshape(shp)
    outs, lses, new_kv = [], [], []
    for g, (win, dil) in enumerate(A_PATTERNS):
        n_steps = win // dil
        q, k, v = aq[:, :, g], ak[:, :, g], av[:, :, g]
        if kv_hist is None:
            o, lse = dilated_attn_prompt(q, k, v, dil, n_steps)
            rows = min(win, L)
            new_kv.append(jnp.stack([k[:, L - rows:], v[:, L - rows:]], axis=2))
        else:
            kv_new = jnp.stack([k, v], axis=2).astype(kv_hist[g].dtype)
            full = jnp.concatenate([kv_hist[g], kv_new], axis=1)
            o, lse = dilated_attn_sample(q, full[:, :, 0], full[:, :, 1], dil, n_steps)
            new_kv.append(kv_new)
        outs.append(o)
        lses.append(lse)
    w_grp = jax.nn.softmax(jnp.stack(lses, axis=0), axis=0)
    o_a = jnp.einsum('gblh,gblhe->blhe', w_grp, jnp.stack(outs, axis=0)).reshape(b, L, A_WIDTH)
    y_a = (o_a * jax.nn.silu(ag.astype(F32))).astype(dt)

    f = lb + (1.0 - lb) * jax.nn.sigmoid(bfg.astype(F32))
    logf = jnp.log(f)
    kin = 1.0 - f
    o_b, s_new = hgrn2_chunked(bq.reshape(b, L, B_HEADS, B_KEY_DIM), kin.reshape(b, L, B_HEADS, B_KEY_DIM),
                               bi.reshape(b, L, B_HEADS, B_VAL_DIM), logf.reshape(b, L, B_HEADS, B_KEY_DIM), s0)
    o_b = rmsnorm(o_b, hgrn_norm).reshape(b, L, B_WIDTH)
    y_b = (o_b * jax.nn.silu(bg.astype(F32))).astype(dt)

    o_c, pool_new = multiscale_pool(cu, pool_hist, first_pos, pool_w, pool_scale)
    y_c = (o_c * jax.nn.silu(cg.astype(F32))).astype(dt)

    gates = jax.nn.sigmoid(mg.reshape(b, L, N_BRANCH, D_MODEL))
    merged = (gates[:, :, 0] * (y_a @ w_br_a) + gates[:, :, 1] * (y_b @ w_br_b)
              + gates[:, :, 2] * (y_c @ w_br_c))
    out = merged @ w_out
    return x + rmsnorm(out, norm_post), new_kv, s_new.astype(dt), pool_new


def setup_inputs(seed: int = 0) -> dict:
    key = jax.random.key(seed)
    ks = jax.random.split(key, 24)

    def nrm(k, shape, scale):
        return jax.random.normal(k, shape, F32) * scale

    inp = {}
    inp['x_prompt'] = nrm(ks[0], (BATCH, SEQ, D_MODEL), 1.0)
    inp['x_sample'] = nrm(ks[1], (DEC_BATCH, DEC_SEQ, D_MODEL), 1.0)
    for i, (win, _) in enumerate(A_PATTERNS):
        inp['cache_kv_w%d' % win] = nrm(ks[2 + i], (DEPTH, DEC_BATCH, min(win, PAST_LEN), 2, A_HEADS, A_HEAD_DIM), 1.0)
    inp['state_hgrn'] = nrm(ks[5], (DEPTH, DEC_BATCH, B_HEADS, B_KEY_DIM, B_VAL_DIM), 0.5)
    inp['state_pool'] = nrm(ks[6], (DEPTH, DEC_BATCH, C_HIST, C_WIDTH), 1.0)
    inp['norm_pre'] = 1.0 + nrm(ks[7], (DEPTH, D_MODEL), 0.02)
    inp['norm_post'] = 1.0 + nrm(ks[8], (DEPTH, D_MODEL), 0.02)
    inp['w_in'] = nrm(ks[9], (DEPTH, D_MODEL, IN_WIDTH), D_MODEL ** -0.5)
    inp['hgrn_lb_logits'] = nrm(ks[10], (DEPTH, B_KEY_WIDTH), 0.5)
    inp['hgrn_norm'] = 1.0 + nrm(ks[11], (DEPTH, B_VAL_DIM), 0.02)
    inp['pool_w'] = nrm(ks[12], (DEPTH, C_GROUPS, C_GROUP_DIM, C_GROUP_DIM), C_GROUP_DIM ** -0.5)
    inp['pool_scale'] = 1.0 + nrm(ks[13], (DEPTH, C_WIDTH), 0.1)
    inp['w_br_a'] = nrm(ks[14], (DEPTH, A_WIDTH, D_MODEL), A_WIDTH ** -0.5)
    inp['w_br_b'] = nrm(ks[15], (DEPTH, B_WIDTH, D_MODEL), B_WIDTH ** -0.5)
    inp['w_br_c'] = nrm(ks[16], (DEPTH, C_WIDTH, D_MODEL), C_WIDTH ** -0.5)
    inp['w_out'] = nrm(ks[17], (DEPTH, D_MODEL, D_MODEL), D_MODEL ** -0.5)
    return inp


def reference(x_prompt, x_sample, cache_kv_w128, cache_kv_w512, cache_kv_w2048, state_hgrn, state_pool,
              norm_pre, norm_post, w_in, hgrn_lb_logits, hgrn_norm, pool_w, pool_scale,
              w_br_a, w_br_b, w_br_c, w_out):
    p = jax.nn.softmax(hgrn_lb_logits.astype(F32), axis=0)
    lower_bounds = jnp.cumsum(p, axis=0) - p[0]
    hp, hs = x_prompt, x_sample
    b_p = x_prompt.shape[0]
    kv_p = ([], [], [])
    kv_s = ([], [], [])
    hg_p, hg_s, pl_p, pl_s = [], [], [], []
    for l in range(DEPTH):
        lw = (norm_pre[l], norm_post[l], w_in[l], hgrn_norm[l], pool_w[l], pool_scale[l],
              w_br_a[l], w_br_b[l], w_br_c[l], w_out[l])
        s0 = jnp.zeros((b_p, B_HEADS, B_KEY_DIM, B_VAL_DIM), F32)
        ph0 = jnp.zeros((b_p, C_HIST, C_WIDTH), x_prompt.dtype)
        hp, kvs, s_new, ph = mixer_layer(hp, None, s0, ph0, 0, lower_bounds[l], *lw)
        for g in range(A_GROUPS):
            kv_p[g].append(kvs[g])
        hg_p.append(s_new)
        pl_p.append(ph)
        hist = (cache_kv_w128[l], cache_kv_w512[l], cache_kv_w2048[l])
        hs, kvs, s_new, ph = mixer_layer(hs, hist, state_hgrn[l], state_pool[l], PAST_LEN, lower_bounds[l], *lw)
        for g in range(A_GROUPS):
            kv_s[g].append(kvs[g])
        hg_s.append(s_new)
        pl_s.append(ph)
    return (hp, hs,
            jnp.stack(kv_p[0]), jnp.stack(kv_p[1]), jnp.stack(kv_p[2]),
            jnp.stack(kv_s[0]), jnp.stack(kv_s[1]), jnp.stack(kv_s[2]),
            jnp.stack(hg_p), jnp.stack(hg_s), jnp.stack(pl_p), jnp.stack(pl_s))
```

```python
import functools

import numpy as np
import jax
import jax.numpy as jnp
from jax import lax
from jax.experimental import pallas as pl
from jax.experimental.pallas import tpu as pltpu

F32 = jnp.float32
BF16 = jnp.bfloat16

D_MODEL = 1024
A_PATTERNS = ((128, 1), (512, 4), (2048, 16))
A_GROUPS = 3
A_HEADS = 4
A_HEAD_DIM = 128
A_WIDTH = A_HEADS * A_HEAD_DIM
N_STEPS = 128
B_HEADS = 8
B_DIM = 128
B_WIDTH = B_HEADS * B_DIM
C_WINDOWS = (2, 4, 8, 16)
C_GROUPS = 4
C_GROUP_DIM = 128
C_WIDTH = C_GROUPS * C_GROUP_DIM
C_HIST = 15
N_BRANCH = 3
EPS = 1e-6
NEG = -1e30
IN_WIDTH = 13312
LANE = 128
HGRN_ROWS = 128

_COL = dict(ak=0, av=1536, cu=3072, cg=3584, aq=4096, ag=5632, bq=6144, bf=7168,
            bi=8192, bg=9216, mg=10240)
_F32_COLS = 4096
_REF_OFF = dict(aq=0, ak=1536, av=3072, ag=4608, bq=5120, bf=6144, bi=7168, bg=8192,
                cu=9216, cg=9728, mg=10240)
_WIDTHS = dict(aq=1536, ak=1536, av=1536, ag=512, bq=1024, bf=1024, bi=1024, bg=1024,
               cu=512, cg=512, mg=3072)
_NEW_ORDER = ("ak", "av", "cu", "cg", "aq", "ag", "bq", "bf", "bi", "bg", "mg")

_VMEM_LIMIT = 56 * 1024 * 1024


def _cparams(sem):
    return pltpu.CompilerParams(dimension_semantics=sem, vmem_limit_bytes=_VMEM_LIMIT)


def _nt(a, b):
    return lax.dot_general(a, b, (((1,), (1,)), ((), ())), preferred_element_type=F32)


def _tn(a, b):
    return lax.dot_general(a, b, (((0,), (0,)), ((), ())), preferred_element_type=F32)


def _silu(x):
    return x / (1.0 + jnp.exp(-x))


def _sigmoid(x):
    return 1.0 / (1.0 + jnp.exp(-x))


def _in_proj_kernel(x_ref, g_ref, w_ref, *refs, n_f32_tiles, z_dtype):
    if n_f32_tiles:
        z_ref, zf_ref, xn_ref = refs
    else:
        z_ref, xn_ref = refs
    j = pl.program_id(1)

    @pl.when(j == 0)
    def _():
        x = x_ref[...]
        ms = jnp.mean(x * x, axis=-1, keepdims=True)
        xn_ref[...] = (x * lax.rsqrt(ms + EPS) * g_ref[...]).astype(BF16)

    acc = jnp.dot(xn_ref[...], w_ref[...], preferred_element_type=F32)
    z_ref[...] = acc.astype(z_dtype)
    if n_f32_tiles:
        @pl.when(j < n_f32_tiles)
        def _():
            zf_ref[...] = acc


def _in_proj(x, gain, w_bf16, *, tm, tn, z_dtype, with_f32):
    n = x.shape[0]
    n_f32_tiles = _F32_COLS // tn if with_f32 else 0
    out_shape = [jax.ShapeDtypeStruct((n, IN_WIDTH), z_dtype)]
    out_specs = [pl.BlockSpec((tm, tn), lambda i, j: (i, j))]
    if with_f32:
        out_shape.append(jax.ShapeDtypeStruct((n, _F32_COLS), F32))
        out_specs.append(pl.BlockSpec((tm, tn), lambda i, j: (i, jnp.minimum(j, n_f32_tiles - 1))))
    return pl.pallas_call(
        functools.partial(_in_proj_kernel, n_f32_tiles=n_f32_tiles, z_dtype=z_dtype),
        grid=(n // tm, IN_WIDTH // tn),
        in_specs=[pl.BlockSpec((tm, D_MODEL), lambda i, j: (i, 0)),
                  pl.BlockSpec((1, D_MODEL), lambda i, j: (0, 0)),
                  pl.BlockSpec((D_MODEL, tn), lambda i, j: (0, j))],
        out_specs=out_specs,
        out_shape=out_shape,
        scratch_shapes=[pltpu.VMEM((tm, D_MODEL), BF16)],
        compiler_params=_cparams(("parallel", "arbitrary")),
        name="in_proj",
    )(x, gain.reshape(1, D_MODEL), w_bf16)


def _attn_prompt_kernel(q_ref, kp_ref, kc_ref, vp_ref, vc_ref, o_ref, lse_ref, *, tq):
    i = pl.program_id(2)
    nk = N_STEPS + tq
    qi = lax.broadcasted_iota(jnp.int32, (tq, nk), 0)
    kj = lax.broadcasted_iota(jnp.int32, (tq, nk), 1)
    valid = (kj >= qi) & (kj <= qi + N_STEPS) & ((i > 0) | (kj >= N_STEPS))
    scale = A_HEAD_DIM ** -0.5
    for h in range(A_HEADS):
        hs = slice(h * A_HEAD_DIM, (h + 1) * A_HEAD_DIM)
        q = q_ref[0, :, hs]
        k = jnp.concatenate([kp_ref[0, :, hs], kc_ref[0, :, hs]], axis=0)
        v = jnp.concatenate([vp_ref[0, :, hs], vc_ref[0, :, hs]], axis=0)
        s = jnp.where(valid, _nt(q, k) * scale, NEG)
        m = jnp.max(s, axis=-1, keepdims=True)
        p = jnp.exp(s - m)
        l = jnp.sum(p, axis=-1, keepdims=True)
        o = jnp.dot(p.astype(BF16), v, preferred_element_type=F32)
        o_ref[0, :, hs] = o / l
        lse_ref[0, :, hs] = jnp.broadcast_to(m + jnp.log(l), (tq, A_HEAD_DIM))


def _attn_prompt(z3, g, *, tq):
    b, s, _ = z3.shape
    d = A_PATTERNS[g][1]
    rows = s // d
    nblk = rows // tq
    prev_per_blk = tq // N_STEPS
    zv = z3.reshape(b, rows, d * IN_WIDTH)
    cpr = IN_WIDTH // A_WIDTH
    qc = _COL["aq"] // A_WIDTH + g
    kc = _COL["ak"] // A_WIDTH + g
    vc = _COL["av"] // A_WIDTH + g

    def cur(c):
        return pl.BlockSpec((1, tq, A_WIDTH), lambda bb, r, i: (bb, i, r * cpr + c))

    def prev(c):
        return pl.BlockSpec((1, N_STEPS, A_WIDTH),
                            lambda bb, r, i: (bb, jnp.maximum(i * prev_per_blk - 1, 0), r * cpr + c))

    out_spec = pl.BlockSpec((1, tq, A_WIDTH), lambda bb, r, i: (bb, i, r))
    o, lse = pl.pallas_call(
        functools.partial(_attn_prompt_kernel, tq=tq),
        grid=(b, d, nblk),
        in_specs=[cur(qc), prev(kc), cur(kc), prev(vc), cur(vc)],
        out_specs=[out_spec, out_spec],
        out_shape=[jax.ShapeDtypeStruct((b, rows, d * A_WIDTH), F32)] * 2,
        compiler_params=_cparams(("parallel", "parallel", "arbitrary")),
        name="attn_prompt_d%d" % d,
    )(zv, zv, zv, zv, zv)
    return o.reshape(b * s, A_WIDTH), lse.reshape(b * s, A_WIDTH)


def _attn_sample_kernel(z_ref, c0_ref, c1_ref, c2_ref, *out_refs, t_new, layer_rows):
    del layer_rows
    scale = A_HEAD_DIM ** -0.5
    nrow = A_HEADS * t_new
    kvw = 2 * A_WIDTH
    for g, c_ref in enumerate((c0_ref, c1_ref, c2_ref)):
        d = A_PATTERNS[g][1]
        n_res = min(d, t_new)
        o_ref, lse_ref = out_refs[2 * g], out_refs[2 * g + 1]
        q = z_ref[:, _COL["aq"] + g * A_WIDTH:_COL["aq"] + (g + 1) * A_WIDTH]
        kn = z_ref[:, _COL["ak"] + g * A_WIDTH:_COL["ak"] + (g + 1) * A_WIDTH]
        vn = z_ref[:, _COL["av"] + g * A_WIDTH:_COL["av"] + (g + 1) * A_WIDTH]
        blk = c_ref[0, 0]
        kcat = jnp.concatenate([blk[:, r * kvw:r * kvw + A_WIDTH] for r in range(n_res)],
                               axis=1).astype(BF16)
        vcat = jnp.concatenate([blk[:, r * kvw + A_WIDTH:(r + 1) * kvw] for r in range(n_res)],
                               axis=1).astype(BF16)
        wide = n_res * A_WIDTH
        q4 = jnp.concatenate([q] * A_HEADS, axis=0)
        row = lax.broadcasted_iota(jnp.int32, (nrow, wide), 0)
        col = lax.broadcasted_iota(jnp.int32, (nrow, wide), 1)
        row_h, row_t = row // t_new, row % t_new
        sel = ((col // A_WIDTH) == (row_t % d)) & (((col % A_WIDTH) // A_HEAD_DIM) == row_h)
        qbd = jnp.where(sel, jnp.concatenate([q4] * n_res, axis=1), 0.0).astype(BF16)
        s_c = _nt(qbd, kcat) * scale
        r2 = lax.broadcasted_iota(jnp.int32, (nrow, N_STEPS), 0) % t_new
        c2 = lax.broadcasted_iota(jnp.int32, (nrow, N_STEPS), 1)
        s_c = jnp.where(c2 >= r2 // d, s_c, NEG)
        pad = jnp.zeros((N_STEPS - t_new, A_WIDTH), F32)
        knp = jnp.concatenate([kn, pad], axis=0).astype(BF16)
        vnp = jnp.concatenate([vn, pad], axis=0).astype(BF16)
        rowq = lax.broadcasted_iota(jnp.int32, (nrow, A_WIDTH), 0) // t_new
        colq = lax.broadcasted_iota(jnp.int32, (nrow, A_WIDTH), 1) // A_HEAD_DIM
        selq = rowq == colq
        qbd2 = jnp.where(selq, q4, 0.0).astype(BF16)
        s_n = _nt(qbd2, knp) * scale
        s_n = jnp.where((c2 <= r2) & ((r2 - c2) % d == 0), s_n, NEG)
        m = jnp.maximum(jnp.max(s_c, axis=-1, keepdims=True), jnp.max(s_n, axis=-1, keepdims=True))
        p_c = jnp.exp(s_c - m)
        p_n = jnp.exp(s_n - m)
        l = jnp.sum(p_c, axis=-1, keepdims=True) + jnp.sum(p_n, axis=-1, keepdims=True)
        o_wide = jnp.where(sel, jnp.dot(p_c.astype(BF16), vcat, preferred_element_type=F32), 0.0)
        o_new = jnp.where(selq, jnp.dot(p_n.astype(BF16), vnp, preferred_element_type=F32), 0.0)
        o = o_new[:, 0:A_HEAD_DIM]
        for c in range(1, A_HEADS):
            o = o + o_new[:, c * A_HEAD_DIM:(c + 1) * A_HEAD_DIM]
        for c in range(wide // A_HEAD_DIM):
            o = o + o_wide[:, c * A_HEAD_DIM:(c + 1) * A_HEAD_DIM]
        o = o / l
        lse = jnp.broadcast_to(m + jnp.log(l), (nrow, A_HEAD_DIM))
        for h in range(A_HEADS):
            hs = slice(h * A_HEAD_DIM, (h + 1) * A_HEAD_DIM)
            o_ref[:, hs] = o[h * t_new:(h + 1) * t_new]
            lse_ref[:, hs] = lse[h * t_new:(h + 1) * t_new]


def _attn_sample(z_s, caches, layer, *, n_seq, t_new):
    in_specs = [pl.BlockSpec((t_new, IN_WIDTH), lambda b: (b, 0))]
    views = []
    for g, c in enumerate(caches):
        win, d = A_PATTERNS[g]
        assert c.shape[2] == win == N_STEPS * d, "cache must hold one full window"
        n_res = min(d, t_new)
        views.append(c.reshape(c.shape[0], n_seq, N_STEPS, d * 2 * A_WIDTH))
        in_specs.append(pl.BlockSpec((1, 1, N_STEPS, n_res * 2 * A_WIDTH),
                                     lambda b, layer=layer: (layer, b, 0, 0)))
    out_spec = pl.BlockSpec((t_new, A_WIDTH), lambda b: (b, 0))
    outs = pl.pallas_call(
        functools.partial(_attn_sample_kernel, t_new=t_new, layer_rows=None),
        grid=(n_seq,),
        in_specs=in_specs,
        out_specs=[out_spec] * (2 * A_GROUPS),
        out_shape=[jax.ShapeDtypeStruct((n_seq * t_new, A_WIDTH), F32)] * (2 * A_GROUPS),
        compiler_params=_cparams(("parallel",)),
        name="attn_sample",
    )(z_s, *views)
    return outs[0::2], outs[1::2]


def _hgrn_consts(seg):
    n = HGRN_ROWS
    t = np.arange(n)[:, None]
    u = np.arange(n)[None, :]
    seg_lo = (t // seg) * seg
    seg_hi = seg_lo + seg - 1
    mats = [(u >= seg_lo) & (u <= t),
            (u > t) & (u <= seg_hi)]
    level = np.full((n, n), -1, np.int32)
    s = u
    li = 0
    m = seg
    while m >= 2:
        half = m // 2
        upper = (t % m) >= half
        ref = (t // m) * m + half - 1
        mats.append(np.where(upper, (u > ref) & (u <= t), (u > t) & (u <= ref)))
        pair = (t // m == s // m) & ((t % m) >= half) & ((s % m) < half)
        level[pair] = li
        li += 1
        m //= 2
    level[np.arange(n), np.arange(n)] = li
    w = np.concatenate(mats, axis=0).astype(np.float32)
    w3 = np.concatenate([w, w, w], axis=1)
    return jnp.asarray(w3, BF16), jnp.asarray(level), li


def _hgrn_block(q, xf, v, lb, w3, level, n_lev, seg, states):
    n = HGRN_ROWS
    f = lb + (1.0 - lb) * _sigmoid(xf)
    g = jnp.log(f)
    kin = 1.0 - f
    g_hi = g.astype(BF16)
    r1 = g - g_hi.astype(F32)
    g_mid = r1.astype(BF16)
    g_lo = (r1 - g_mid.astype(F32)).astype(BF16)
    fac = jnp.exp(jnp.dot(w3, jnp.concatenate([g_hi, g_mid, g_lo], axis=0),
                          preferred_element_type=F32))
    ea = fac[0:n]
    qt = (q * ea).astype(BF16)
    kh = kin * fac[n:2 * n]
    row = lax.broadcasted_iota(jnp.int32, (n, B_DIM), 0)
    att = jnp.zeros((n, n), F32)
    m = seg
    for li in range(n_lev):
        upper = (row & (m // 2)) != 0
        tm = (jnp.where(upper, q, kin) * fac[(2 + li) * n:(3 + li) * n]).astype(BF16)
        att = jnp.where(level == li, _nt(tm, tm), att)
        m //= 2
    att = jnp.where(level == n_lev, _nt(q.astype(BF16), kin.astype(BF16)), att)
    vb = v.astype(BF16)
    o = jnp.dot(att.astype(BF16), vb, preferred_element_type=F32)
    new_states = []
    n_seg = n // seg
    for si in range(n_seg):
        st = states[si]
        o_s = _nt(qt, st.astype(BF16))
        if n_seg == 1:
            o = o + o_s
            kh_s = kh
        else:
            in_seg = (row >= si * seg) & (row < (si + 1) * seg)
            o = o + jnp.where(in_seg, o_s, 0.0)
            kh_s = jnp.where(in_seg, kh, 0.0)
        decay = ea[(si + 1) * seg - 1:(si + 1) * seg, :]
        new_states.append(st * decay + _tn(vb, kh_s.astype(BF16)))
    return o, new_states


def _head_rmsnorm(o, gain):
    return o * lax.rsqrt(jnp.mean(o * o, axis=-1, keepdims=True) + EPS) * gain


def _lower_bound(lbl_ref, layer):
    lg = lbl_ref[...]
    e = jnp.exp(lg - jnp.max(lg, axis=0, keepdims=True))
    p = e / jnp.sum(e, axis=0, keepdims=True)
    lb = jnp.sum(p[0:layer + 1], axis=0, keepdims=True) - p[0:1]
    return lb


def _hgrn_prompt_kernel(q_ref, f_ref, v_ref, lbl_ref, gn_ref, w3_ref, lev_ref, o_ref, s_ref,
                        st_ref, *, layer, n_lev, tl):
    li = pl.program_id(2)

    @pl.when(li == 0)
    def _():
        st_ref[...] = jnp.zeros_like(st_ref)

    lb = _lower_bound(lbl_ref, layer)
    w3 = w3_ref[...]
    level = lev_ref[...]
    gain = gn_ref[...]

    def body(c, carry):
        r0 = pl.multiple_of(c * HGRN_ROWS, HGRN_ROWS)
        q = q_ref[0, pl.ds(r0, HGRN_ROWS), :].astype(F32)
        xf = f_ref[0, pl.ds(r0, HGRN_ROWS), :].astype(F32)
        v = v_ref[0, pl.ds(r0, HGRN_ROWS), :].astype(F32)
        o, (st,) = _hgrn_block(q, xf, v, lb, w3, level, n_lev, HGRN_ROWS, [st_ref[...]])
        st_ref[...] = st
        o_ref[0, pl.ds(r0, HGRN_ROWS), :] = _head_rmsnorm(o, gain)
        return carry

    lax.fori_loop(0, tl // HGRN_ROWS, body, 0)

    @pl.when(li == pl.num_programs(2) - 1)
    def _():
        s_ref[0, 0] = st_ref[...].T


def _hgrn_prompt(z3, lb_logits, hgrn_gain, layer, *, tl):
    b, s, _ = z3.shape
    w3, level, n_lev = _hgrn_consts(HGRN_ROWS)

    def col(name):
        c0 = _COL[name] // B_DIM
        return pl.BlockSpec((1, tl, B_DIM), lambda bb, h, i: (bb, i, c0 + h))

    const = lambda shape: pl.BlockSpec(shape, lambda bb, h, i: (0,) * len(shape))
    return pl.pallas_call(
        functools.partial(_hgrn_prompt_kernel, layer=layer, n_lev=n_lev, tl=tl),
        grid=(b, B_HEADS, s // tl),
        in_specs=[col("bq"), col("bf"), col("bi"),
                  pl.BlockSpec((lb_logits.shape[0], B_DIM), lambda bb, h, i: (0, h)),
                  const((1, B_DIM)), const(w3.shape), const(level.shape)],
        out_specs=[pl.BlockSpec((1, tl, B_DIM), lambda bb, h, i: (bb, i, h)),
                   pl.BlockSpec((1, 1, B_DIM, B_DIM), lambda bb, h, i: (bb, h, 0, 0))],
        out_shape=[jax.ShapeDtypeStruct((b, s, B_WIDTH), F32),
                   jax.ShapeDtypeStruct((b, B_HEADS, B_DIM, B_DIM), F32)],
        scratch_shapes=[pltpu.VMEM((B_DIM, B_DIM), F32)],
        compiler_params=_cparams(("parallel", "parallel", "arbitrary")),
        name="hgrn_prompt",
    )(z3, z3, z3, lb_logits, hgrn_gain.reshape(1, B_DIM), w3, level)


def _hgrn_sample_kernel(q_ref, f_ref, v_ref, s0_ref, lbl_ref, gn_ref, w3_ref, lev_ref,
                        o_ref, s_ref, *, layer, n_lev, t_new):
    lb = _lower_bound(lbl_ref, layer)
    n_seg = HGRN_ROWS // t_new
    states = [s0_ref[0, si, 0].T for si in range(n_seg)]
    o, new_states = _hgrn_block(q_ref[...], f_ref[...], v_ref[...], lb, w3_ref[...],
                                lev_ref[...], n_lev, t_new, states)
    o_ref[...] = _head_rmsnorm(o, gn_ref[...])
    for si in range(n_seg):
        s_ref[si, 0] = new_states[si].T


def _hgrn_sample(z_s, state, lb_logits, hgrn_gain, layer, *, n_seq, t_new):
    w3, level, n_lev = _hgrn_consts(t_new)
    n_seg = HGRN_ROWS // t_new

    def col(name):
        c0 = _COL[name] // B_DIM
        return pl.BlockSpec((HGRN_ROWS, B_DIM), lambda i, h: (i, c0 + h))

    const = lambda shape: pl.BlockSpec(shape, lambda i, h: (0,) * len(shape))
    return pl.pallas_call(
        functools.partial(_hgrn_sample_kernel, layer=layer, n_lev=n_lev, t_new=t_new),
        grid=(n_seq // n_seg, B_HEADS),
        in_specs=[col("bq"), col("bf"), col("bi"),
                  pl.BlockSpec((1, n_seg, 1, B_DIM, B_DIM), lambda i, h, layer=layer: (layer, i, h, 0, 0)),
                  pl.BlockSpec((lb_logits.shape[0], B_DIM), lambda i, h: (0, h)),
                  const((1, B_DIM)), const(w3.shape), const(level.shape)],
        out_specs=[pl.BlockSpec((HGRN_ROWS, B_DIM), lambda i, h: (i, h)),
                   pl.BlockSpec((n_seg, 1, B_DIM, B_DIM), lambda i, h: (i, h, 0, 0))],
        out_shape=[jax.ShapeDtypeStruct((n_seq * t_new, B_WIDTH), F32),
                   jax.ShapeDtypeStruct((n_seq, B_HEADS, B_DIM, B_DIM), F32)],
        compiler_params=_cparams(("parallel", "parallel")),
        name="hgrn_sample",
    )(z_s, z_s, z_s, state, lb_logits, hgrn_gain.reshape(1, B_DIM), w3, level)


_POOL_PAD = 16


def _pool_mix(ext_at, u, pos, pw_ref, ps_ref):
    outs = []
    for g, win in enumerate(C_WINDOWS):
        gs = slice(g * C_GROUP_DIM, (g + 1) * C_GROUP_DIM)
        acc = u[:, gs]
        for k in range(1, win):
            acc = acc + ext_at(k, gs)
        cnt = jnp.minimum(pos + 1, win).astype(F32)
        pooled = acc * (1.0 / cnt) - u[:, gs]
        mixed = jnp.dot(pooled.astype(BF16), pw_ref[g].astype(BF16), preferred_element_type=F32)
        outs.append(mixed * ps_ref[:, gs])
    return outs


def _pool_prompt_kernel(u_ref, h_ref, pw_ref, ps_ref, o_ref, ext_ref, *, tl):
    i = pl.program_id(1)
    halo = h_ref[0].astype(F32)
    ext_ref[0:_POOL_PAD, :] = jnp.where(i > 0, halo, 0.0)
    u = u_ref[0].astype(F32)
    ext_ref[_POOL_PAD:, :] = u
    pos = i * tl + lax.broadcasted_iota(jnp.int32, (tl, 1), 0)
    outs = _pool_mix(lambda k, gs: ext_ref[_POOL_PAD - k:_POOL_PAD - k + tl, gs], u, pos, pw_ref, ps_ref)
    for g, og in enumerate(outs):
        o_ref[0, :, g * C_GROUP_DIM:(g + 1) * C_GROUP_DIM] = og


def _pool_prompt(z3, pool_w, pool_scale, *, tl):
    b, s, _ = z3.shape
    c0 = _COL["cu"] // C_WIDTH
    per = tl // _POOL_PAD
    return pl.pallas_call(
        functools.partial(_pool_prompt_kernel, tl=tl),
        grid=(b, s // tl),
        in_specs=[pl.BlockSpec((1, tl, C_WIDTH), lambda bb, i: (bb, i, c0)),
                  pl.BlockSpec((1, _POOL_PAD, C_WIDTH), lambda bb, i: (bb, jnp.maximum(i * per - 1, 0), c0)),
                  pl.BlockSpec((C_GROUPS, C_GROUP_DIM, C_GROUP_DIM), lambda bb, i: (0, 0, 0)),
                  pl.BlockSpec((1, C_WIDTH), lambda bb, i: (0, 0))],
        out_specs=pl.BlockSpec((1, tl, C_WIDTH), lambda bb, i: (bb, i, 0)),
        out_shape=jax.ShapeDtypeStruct((b, s, C_WIDTH), F32),
        scratch_shapes=[pltpu.VMEM((_POOL_PAD + tl, C_WIDTH), F32)],
        compiler_params=_cparams(("parallel", "arbitrary")),
        name="pool_prompt",
    )(z3, z3, pool_w, pool_scale.reshape(1, C_WIDTH))


def _pool_sample_kernel(u_ref, h_ref, pw_ref, ps_ref, o_ref, ext_ref, *, nb, t_new, first_pos):
    ext_ref[:, 0:1, :] = jnp.zeros((nb, 1, C_WIDTH), F32)
    ext_ref[:, 1:_POOL_PAD, :] = h_ref[0]
    u3 = u_ref[...].reshape(nb, t_new, C_WIDTH)
    ext_ref[:, _POOL_PAD:, :] = u3
    u = u_ref[...]
    pos = first_pos + lax.broadcasted_iota(jnp.int32, (nb * t_new, 1), 0) % t_new

    def ext_at(k, gs):
        return ext_ref[:, _POOL_PAD - k:_POOL_PAD - k + t_new, gs].reshape(nb * t_new, C_GROUP_DIM)

    outs = _pool_mix(ext_at, u, pos, pw_ref, ps_ref)
    for g, og in enumerate(outs):
        o_ref[:, g * C_GROUP_DIM:(g + 1) * C_GROUP_DIM] = og


def _pool_sample(z_s, state_pool, pool_w, pool_scale, layer, *, n_seq, t_new, first_pos, nb):
    c0 = _COL["cu"] // C_WIDTH
    return pl.pallas_call(
        functools.partial(_pool_sample_kernel, nb=nb, t_new=t_new, first_pos=first_pos),
        grid=(n_seq // nb,),
        in_specs=[pl.BlockSpec((nb * t_new, C_WIDTH), lambda i: (i, c0)),
                  pl.BlockSpec((1, nb, C_HIST, C_WIDTH), lambda i, layer=layer: (layer, i, 0, 0)),
                  pl.BlockSpec((C_GROUPS, C_GROUP_DIM, C_GROUP_DIM), lambda i: (0, 0, 0)),
                  pl.BlockSpec((1, C_WIDTH), lambda i: (0, 0))],
        out_specs=pl.BlockSpec((nb * t_new, C_WIDTH), lambda i: (i, 0)),
        out_shape=jax.ShapeDtypeStruct((n_seq * t_new, C_WIDTH), F32),
        scratch_shapes=[pltpu.VMEM((nb, _POOL_PAD + t_new, C_WIDTH), F32)],
        compiler_params=_cparams(("parallel",)),
        name="pool_sample",
    )(z_s, state_pool, pool_w, pool_scale.reshape(1, C_WIDTH))


def _out_proj_kernel(x_ref, ag_ref, bg_ref, cg_ref, m0_ref, m1_ref, m2_ref, o0_ref, o1_ref, o2_ref,
                     l0_ref, l1_ref, l2_ref, ob_ref, oc_ref, wa_ref, wb_ref, wc_ref, wo_ref,
                     gp_ref, y_ref):
    l0, l1, l2 = l0_ref[...], l1_ref[...], l2_ref[...]
    mx = jnp.maximum(jnp.maximum(l0, l1), l2)
    e0, e1, e2 = jnp.exp(l0 - mx), jnp.exp(l1 - mx), jnp.exp(l2 - mx)
    o_a = (e0 * o0_ref[...] + e1 * o1_ref[...] + e2 * o2_ref[...]) / (e0 + e1 + e2)
    y_a = (o_a * _silu(ag_ref[...].astype(F32))).astype(BF16)
    y_b = (ob_ref[...] * _silu(bg_ref[...].astype(F32))).astype(BF16)
    y_c = (oc_ref[...] * _silu(cg_ref[...].astype(F32))).astype(BF16)
    gate = lambda m_ref: _sigmoid(m_ref[...].astype(F32))
    merged = (gate(m0_ref) * jnp.dot(y_a, wa_ref[...], preferred_element_type=F32)
              + gate(m1_ref) * jnp.dot(y_b, wb_ref[...], preferred_element_type=F32)
              + gate(m2_ref) * jnp.dot(y_c, wc_ref[...], preferred_element_type=F32))
    out = jnp.dot(merged.astype(BF16), wo_ref[...], preferred_element_type=F32)
    normed = out * lax.rsqrt(jnp.mean(out * out, axis=-1, keepdims=True) + EPS) * gp_ref[...]
    y_ref[...] = x_ref[...] + normed


def _out_proj(x, z, o_g, lse_g, o_b, o_c, wa, wb, wc, wo, gain_post, *, tm):
    n = x.shape[0]

    def zcol(name, width, k=0):
        assert _COL[name] % width == 0
        c0 = _COL[name] // width + k
        return pl.BlockSpec((tm, width), lambda i: (i, c0))

    rows = lambda width: pl.BlockSpec((tm, width), lambda i: (i, 0))
    const = lambda a: pl.BlockSpec(a.shape, lambda i: (0,) * a.ndim)
    gp = gain_post.reshape(1, D_MODEL)
    return pl.pallas_call(
        _out_proj_kernel,
        grid=(n // tm,),
        in_specs=[rows(D_MODEL), zcol("ag", A_WIDTH), zcol("bg", B_WIDTH), zcol("cg", C_WIDTH),
                  zcol("mg", D_MODEL, 0), zcol("mg", D_MODEL, 1), zcol("mg", D_MODEL, 2)]
                 + [rows(A_WIDTH)] * 6 + [rows(B_WIDTH), rows(C_WIDTH)]
                 + [const(wa), const(wb), const(wc), const(wo), const(gp)],
        out_specs=rows(D_MODEL),
        out_shape=jax.ShapeDtypeStruct((n, D_MODEL), F32),
        compiler_params=_cparams(("parallel",)),
        name="out_proj",
    )(x, z, z, z, z, z, z, *o_g, *lse_g, o_b, o_c, wa, wb, wc, wo, gp)


def _reorder_w_in(w):
    parts = [w[:, _REF_OFF[k]:_REF_OFF[k] + _WIDTHS[k]] for k in _NEW_ORDER]
    return jnp.concatenate(parts, axis=1).astype(BF16)


def _kv_rows(zf, g):
    k = zf[..., _COL["ak"] + g * A_WIDTH:_COL["ak"] + (g + 1) * A_WIDTH]
    v = zf[..., _COL["av"] + g * A_WIDTH:_COL["av"] + (g + 1) * A_WIDTH]
    kv = jnp.stack([k, v], axis=-2)
    return kv.reshape(kv.shape[:-1] + (A_HEADS, A_HEAD_DIM))


def kernel(x_prompt, x_sample, cache_kv_w128, cache_kv_w512, cache_kv_w2048, state_hgrn, state_pool,
           norm_pre, norm_post, w_in, hgrn_lb_logits, hgrn_norm, pool_w, pool_scale,
           w_br_a, w_br_b, w_br_c, w_out):
    depth = w_in.shape[0]
    b_p, s_p, _ = x_prompt.shape
    n_seq, t_new, _ = x_sample.shape
    caches = (cache_kv_w128, cache_kv_w512, cache_kv_w2048)
    past_len = cache_kv_w2048.shape[2]
    assert HGRN_ROWS % t_new == 0 and n_seq % (HGRN_ROWS // t_new) == 0

    hp = x_prompt.reshape(b_p * s_p, D_MODEL)
    hs = x_sample.reshape(n_seq * t_new, D_MODEL)
    kv_p = [[] for _ in range(A_GROUPS)]
    kv_s = [[] for _ in range(A_GROUPS)]
    hg_p, hg_s, pl_p, pl_s = [], [], [], []
    tm_p = min(1024, b_p * s_p)
    tm_s = min(1024, n_seq * t_new)
    for l in range(depth):
        w_l = _reorder_w_in(w_in[l])
        wa, wb, wc, wo = (w[l].astype(BF16) for w in (w_br_a, w_br_b, w_br_c, w_out))

        z_p, zf_p = _in_proj(hp, norm_pre[l], w_l, tm=tm_p, tn=1024, z_dtype=BF16, with_f32=True)
        z3 = z_p.reshape(b_p, s_p, IN_WIDTH)
        zf3 = zf_p.reshape(b_p, s_p, _F32_COLS)
        o_g, lse_g = [], []
        for g, (win, d) in enumerate(A_PATTERNS):
            o, lse = _attn_prompt(z3, g, tq=min(256, s_p // d))
            o_g.append(o)
            lse_g.append(lse)
            rows = min(win, s_p)
            kv_p[g].append(_kv_rows(zf3[:, s_p - rows:], g))
        o_b, s_new = _hgrn_prompt(z3, hgrn_lb_logits, hgrn_norm[l], l, tl=min(1024, s_p))
        hg_p.append(s_new)
        o_c = _pool_prompt(z3, pool_w[l], pool_scale[l], tl=min(1024, s_p))
        pl_p.append(zf3[:, s_p - C_HIST:, _COL["cu"]:_COL["cu"] + C_WIDTH])
        hp = _out_proj(hp, z_p, o_g, lse_g, o_b.reshape(b_p * s_p, B_WIDTH),
                       o_c.reshape(b_p * s_p, C_WIDTH), wa, wb, wc, wo, norm_post[l],
                       tm=min(256, b_p * s_p))

        (z_s,) = _in_proj(hs, norm_pre[l], w_l, tm=tm_s, tn=1024, z_dtype=F32, with_f32=False)
        o_g, lse_g = _attn_sample(z_s, caches, l, n_seq=n_seq, t_new=t_new)
        zs3 = z_s.reshape(n_seq, t_new, IN_WIDTH)
        for g in range(A_GROUPS):
            kv_s[g].append(_kv_rows(zs3, g))
        o_b, s_new = _hgrn_sample(z_s, state_hgrn, hgrn_lb_logits, hgrn_norm[l], l,
                                  n_seq=n_seq, t_new=t_new)
        hg_s.append(s_new)
        o_c = _pool_sample(z_s, state_pool, pool_w[l], pool_scale[l], l, n_seq=n_seq,
                           t_new=t_new, first_pos=past_len, nb=HGRN_ROWS // t_new)
        u_new = zs3[:, :, _COL["cu"]:_COL["cu"] + C_WIDTH]
        pl_s.append(jnp.concatenate([state_pool[l], u_new], axis=1)[:, -C_HIST:])
        hs = _out_proj(hs, z_s, o_g, lse_g, o_b, o_c, wa, wb, wc, wo, norm_post[l],
                       tm=min(256, n_seq * t_new))

    return (hp.reshape(b_p, s_p, D_MODEL), hs.reshape(n_seq, t_new, D_MODEL),
            jnp.stack(kv_p[0]), jnp.stack(kv_p[1]), jnp.stack(kv_p[2]),
            jnp.stack(kv_s[0]), jnp.stack(kv_s[1]), jnp.stack(kv_s[2]),
            jnp.stack(hg_p), jnp.stack(hg_s), jnp.stack(pl_p), jnp.stack(pl_s))
```

```python
import functools

import numpy as np
import jax
import jax.numpy as jnp
from jax import lax
from jax.experimental import pallas as pl
from jax.experimental.pallas import tpu as pltpu

F32 = jnp.float32
BF16 = jnp.bfloat16

D_MODEL = 1024
A_PATTERNS = ((128, 1), (512, 4), (2048, 16))
A_GROUPS = 3
A_HEADS = 4
A_HEAD_DIM = 128
A_WIDTH = A_HEADS * A_HEAD_DIM
N_STEPS = 128
B_HEADS = 8
B_DIM = 128
B_WIDTH = B_HEADS * B_DIM
C_WINDOWS = (2, 4, 8, 16)
C_GROUPS = 4
C_GROUP_DIM = 128
C_WIDTH = C_GROUPS * C_GROUP_DIM
C_HIST = 15
N_BRANCH = 3
EPS = 1e-6
NEG = -1e30
IN_WIDTH = 13312
LANE = 128
HGRN_ROWS = 128
HGRN_HEADS_PER_STEP = 2

_REF_OFF = dict(aq=0, ak=1536, av=3072, ag=4608, bq=5120, bf=6144, bi=7168, bg=8192,
                cu=9216, cg=9728, mg=10240)
_WIDTHS = dict(aq=1536, ak=1536, av=1536, ag=512, bq=1024, bf=1024, bi=1024, bg=1024,
               cu=512, cg=512, mg=3072)

_ORDER_S = ("ak", "av", "cu", "cg", "aq", "ag", "bq", "bf", "bi", "bg", "mg")
_COL_S = dict(ak=0, av=1536, cu=3072, cg=3584, aq=4096, ag=5632, bq=6144, bf=7168,
              bi=8192, bg=9216, mg=10240)
_COL_P = dict(ak=0, av=512, cu=1024, cg=1536, aq=2048, ag=2560, bq=3072, bf=4096,
              bi=5120, bg=6144, mg=7168)
_P_WIDTH = 10240
_P_F32 = 2048
_PERM_WIDTH = 3 * A_WIDTH

_VMEM_LIMIT = 56 * 1024 * 1024


def _cparams(sem):
    return pltpu.CompilerParams(dimension_semantics=sem, vmem_limit_bytes=_VMEM_LIMIT)


def _nt(a, b):
    return lax.dot_general(a, b, (((1,), (1,)), ((), ())), preferred_element_type=F32)


def _tn(a, b):
    return lax.dot_general(a, b, (((0,), (0,)), ((), ())), preferred_element_type=F32)


def _silu(x):
    return x / (1.0 + jnp.exp(-x))


def _sigmoid(x):
    return 1.0 / (1.0 + jnp.exp(-x))


def _rmsnorm_rows(x, gain):
    return x * lax.rsqrt(jnp.mean(x * x, axis=-1, keepdims=True) + EPS) * gain


def _in_proj_kernel(x_ref, g_ref, w_ref, *refs, n_f32_tiles, z_dtype):
    if n_f32_tiles:
        z_ref, zf_ref, xn_ref = refs
    else:
        z_ref, xn_ref = refs
    j = pl.program_id(1)

    @pl.when(j == 0)
    def _():
        xn_ref[...] = _rmsnorm_rows(x_ref[...], g_ref[...]).astype(BF16)

    acc = jnp.dot(xn_ref[...], w_ref[...], preferred_element_type=F32)
    z_ref[...] = acc.astype(z_dtype)
    if n_f32_tiles:
        @pl.when(j < n_f32_tiles)
        def _():
            zf_ref[...] = acc


def _in_proj(x, gain, w_bf16, *, tm, tn, z_dtype, f32_cols):
    n = x.shape[0]
    width = w_bf16.shape[1]
    n_f32_tiles = f32_cols // tn
    out_shape = [jax.ShapeDtypeStruct((n, width), z_dtype)]
    out_specs = [pl.BlockSpec((tm, tn), lambda i, j: (i, j))]
    if n_f32_tiles:
        out_shape.append(jax.ShapeDtypeStruct((n, f32_cols), F32))
        out_specs.append(pl.BlockSpec((tm, tn), lambda i, j: (i, jnp.minimum(j, n_f32_tiles - 1))))
    return pl.pallas_call(
        functools.partial(_in_proj_kernel, n_f32_tiles=n_f32_tiles, z_dtype=z_dtype),
        grid=(n // tm, width // tn),
        in_specs=[pl.BlockSpec((tm, D_MODEL), lambda i, j: (i, 0)),
                  pl.BlockSpec((1, D_MODEL), lambda i, j: (0, 0)),
                  pl.BlockSpec((D_MODEL, tn), lambda i, j: (0, j))],
        out_specs=out_specs,
        out_shape=out_shape,
        scratch_shapes=[pltpu.VMEM((tm, D_MODEL), BF16)],
        compiler_params=_cparams(("parallel", "arbitrary")),
        name="in_proj",
    )(x, gain.reshape(1, D_MODEL), w_bf16)


def _in_proj_perm_kernel(x_ref, g_ref, w_ref, r_ref, f_ref, xs_ref, xn_ref, *, d, tm):
    xn = _rmsnorm_rows(x_ref[...], g_ref[...])
    n_chunk = D_MODEL // LANE
    rows = tm // d
    for c in range(n_chunk):
        xs_ref[c] = xn[:, c * LANE:(c + 1) * LANE]
    for r in range(d):
        for c in range(n_chunk):
            xn_ref[r * rows:(r + 1) * rows, c * LANE:(c + 1) * LANE] = (
                xs_ref[c, pl.ds(r, rows, stride=d), :].astype(BF16))
    acc = jnp.dot(xn_ref[...], w_ref[...], preferred_element_type=F32)
    r_ref[0] = acc.astype(BF16).reshape(d, rows, _PERM_WIDTH)
    f_ref[0] = acc[:, 0:2 * A_WIDTH].reshape(d, rows, 2 * A_WIDTH)


def _in_proj_perm(x, gain, w_g, *, b, s, d, tm):
    tpb = s // tm
    rows = tm // d
    idx = lambda i: (i // tpb, 0, i % tpb, 0)
    return pl.pallas_call(
        functools.partial(_in_proj_perm_kernel, d=d, tm=tm),
        grid=(b * tpb,),
        in_specs=[pl.BlockSpec((tm, D_MODEL), lambda i: (i, 0)),
                  pl.BlockSpec((1, D_MODEL), lambda i: (0, 0)),
                  pl.BlockSpec((D_MODEL, _PERM_WIDTH), lambda i: (0, 0))],
        out_specs=[pl.BlockSpec((1, d, rows, _PERM_WIDTH), idx),
                   pl.BlockSpec((1, d, rows, 2 * A_WIDTH), idx)],
        out_shape=[jax.ShapeDtypeStruct((b, d, s // d, _PERM_WIDTH), BF16),
                   jax.ShapeDtypeStruct((b, d, s // d, 2 * A_WIDTH), F32)],
        scratch_shapes=[pltpu.VMEM((D_MODEL // LANE, tm, LANE), F32),
                        pltpu.VMEM((tm, D_MODEL), BF16)],
        compiler_params=_cparams(("parallel",)),
        name="in_proj_d%d" % d,
    )(x, gain.reshape(1, D_MODEL), w_g)


def _attn_prompt_kernel(q_ref, kp_ref, kc_ref, vp_ref, vc_ref, o_ref, lse_ref, *, tq):
    i = pl.program_id(2)
    nk = N_STEPS + tq
    qi = lax.broadcasted_iota(jnp.int32, (tq, nk), 0)
    kj = lax.broadcasted_iota(jnp.int32, (tq, nk), 1)
    valid = (kj >= qi) & (kj <= qi + N_STEPS) & ((i > 0) | (kj >= N_STEPS))
    scale = A_HEAD_DIM ** -0.5
    for h in range(A_HEADS):
        hs = slice(h * A_HEAD_DIM, (h + 1) * A_HEAD_DIM)
        q = q_ref[0, 0, :, hs]
        k = jnp.concatenate([kp_ref[0, 0, :, hs], kc_ref[0, 0, :, hs]], axis=0)
        v = jnp.concatenate([vp_ref[0, 0, :, hs], vc_ref[0, 0, :, hs]], axis=0)
        s = jnp.where(valid, _nt(q, k) * scale, NEG)
        m = jnp.max(s, axis=-1, keepdims=True)
        p = jnp.exp(s - m)
        l = jnp.sum(p, axis=-1, keepdims=True)
        o = jnp.dot(p.astype(BF16), v, preferred_element_type=F32)
        o_ref[0, 0, :, hs] = o / l
        lse_ref[0, 0, :, hs] = jnp.broadcast_to(m + jnp.log(l), (tq, A_HEAD_DIM))


def _attn_prompt(arr, cols, *, tq):
    b, d, rows, _ = arr.shape
    qc, kc, vc = cols
    prev_per_blk = tq // N_STEPS

    def cur(c):
        return pl.BlockSpec((1, 1, tq, A_WIDTH), lambda bb, r, i: (bb, r, i, c))

    def prev(c):
        return pl.BlockSpec((1, 1, N_STEPS, A_WIDTH),
                            lambda bb, r, i: (bb, r, jnp.maximum(i * prev_per_blk - 1, 0), c))

    out_spec = pl.BlockSpec((1, 1, tq, A_WIDTH), lambda bb, r, i: (bb, r, i, 0))
    return pl.pallas_call(
        functools.partial(_attn_prompt_kernel, tq=tq),
        grid=(b, d, rows // tq),
        in_specs=[cur(qc), prev(kc), cur(kc), prev(vc), cur(vc)],
        out_specs=[out_spec, out_spec],
        out_shape=[jax.ShapeDtypeStruct((b, d, rows, A_WIDTH), F32)] * 2,
        compiler_params=_cparams(("parallel", "parallel", "arbitrary")),
        name="attn_prompt_d%d" % d,
    )(arr, arr, arr, arr, arr)


def _attn_sample_kernel(z_ref, c0_ref, c1_ref, c2_ref, *out_refs, t_new):
    scale = A_HEAD_DIM ** -0.5
    nrow = A_HEADS * t_new
    per_res = 2 * A_HEADS
    for g, c_ref in enumerate((c0_ref, c1_ref, c2_ref)):
        d = A_PATTERNS[g][1]
        n_res = min(d, t_new)
        o_ref, lse_ref = out_refs[2 * g], out_refs[2 * g + 1]
        q = z_ref[:, _COL_S["aq"] + g * A_WIDTH:_COL_S["aq"] + (g + 1) * A_WIDTH]
        kn = z_ref[:, _COL_S["ak"] + g * A_WIDTH:_COL_S["ak"] + (g + 1) * A_WIDTH]
        vn = z_ref[:, _COL_S["av"] + g * A_WIDTH:_COL_S["av"] + (g + 1) * A_WIDTH]
        kcat = jnp.concatenate([c_ref[:, r * per_res + h, :]
                                for r in range(n_res) for h in range(A_HEADS)], axis=1).astype(BF16)
        vcat = jnp.concatenate([c_ref[:, r * per_res + A_HEADS + h, :]
                                for r in range(n_res) for h in range(A_HEADS)], axis=1).astype(BF16)
        wide = n_res * A_WIDTH
        q4 = jnp.concatenate([q] * A_HEADS, axis=0)
        row = lax.broadcasted_iota(jnp.int32, (nrow, wide), 0)
        col = lax.broadcasted_iota(jnp.int32, (nrow, wide), 1)
        row_h, row_t = row // t_new, row % t_new
        sel = ((col // A_WIDTH) == (row_t % d)) & (((col % A_WIDTH) // A_HEAD_DIM) == row_h)
        qbd = jnp.where(sel, jnp.concatenate([q4] * n_res, axis=1), 0.0).astype(BF16)
        s_c = _nt(qbd, kcat) * scale
        r2 = lax.broadcasted_iota(jnp.int32, (nrow, N_STEPS), 0) % t_new
        c2 = lax.broadcasted_iota(jnp.int32, (nrow, N_STEPS), 1)
        s_c = jnp.where(c2 >= r2 // d, s_c, NEG)
        pad = jnp.zeros((N_STEPS - t_new, A_WIDTH), F32)
        knp = jnp.concatenate([kn, pad], axis=0).astype(BF16)
        vnp = jnp.concatenate([vn, pad], axis=0).astype(BF16)
        rowq = lax.broadcasted_iota(jnp.int32, (nrow, A_WIDTH), 0) // t_new
        colq = lax.broadcasted_iota(jnp.int32, (nrow, A_WIDTH), 1) // A_HEAD_DIM
        selq = rowq == colq
        qbd2 = jnp.where(selq, q4, 0.0).astype(BF16)
        s_n = _nt(qbd2, knp) * scale
        s_n = jnp.where((c2 <= r2) & ((r2 - c2) % d == 0), s_n, NEG)
        m = jnp.maximum(jnp.max(s_c, axis=-1, keepdims=True), jnp.max(s_n, axis=-1, keepdims=True))
        p_c = jnp.exp(s_c - m)
        p_n = jnp.exp(s_n - m)
        l = jnp.sum(p_c, axis=-1, keepdims=True) + jnp.sum(p_n, axis=-1, keepdims=True)
        o_wide = jnp.where(sel, jnp.dot(p_c.astype(BF16), vcat, preferred_element_type=F32), 0.0)
        o_new = jnp.where(selq, jnp.dot(p_n.astype(BF16), vnp, preferred_element_type=F32), 0.0)
        o = o_new[:, 0:A_HEAD_DIM]
        for c in range(1, A_HEADS):
            o = o + o_new[:, c * A_HEAD_DIM:(c + 1) * A_HEAD_DIM]
        for c in range(wide // A_HEAD_DIM):
            o = o + o_wide[:, c * A_HEAD_DIM:(c + 1) * A_HEAD_DIM]
        o = o / l
        lse = jnp.broadcast_to(m + jnp.log(l), (nrow, A_HEAD_DIM))
        for h in range(A_HEADS):
            hs = slice(h * A_HEAD_DIM, (h + 1) * A_HEAD_DIM)
            o_ref[:, hs] = o[h * t_new:(h + 1) * t_new]
            lse_ref[:, hs] = lse[h * t_new:(h + 1) * t_new]


def _attn_sample(z_s, caches, layer, *, n_seq, t_new):
    in_specs = [pl.BlockSpec((t_new, IN_WIDTH), lambda b: (b, 0))]
    views = []
    per_res = 2 * A_HEADS
    for g, c in enumerate(caches):
        win, d = A_PATTERNS[g]
        assert c.shape[2] == win == N_STEPS * d, "cache must hold one full window"
        n_res = min(d, t_new)
        views.append(c.reshape(c.shape[0], n_seq, N_STEPS, d * per_res, A_HEAD_DIM))
        in_specs.append(pl.BlockSpec((None, None, N_STEPS, n_res * per_res, A_HEAD_DIM),
                                     lambda b, layer=layer: (layer, b, 0, 0, 0)))
    out_spec = pl.BlockSpec((t_new, A_WIDTH), lambda b: (b, 0))
    outs = pl.pallas_call(
        functools.partial(_attn_sample_kernel, t_new=t_new),
        grid=(n_seq,),
        in_specs=in_specs,
        out_specs=[out_spec] * (2 * A_GROUPS),
        out_shape=[jax.ShapeDtypeStruct((n_seq * t_new, A_WIDTH), F32)] * (2 * A_GROUPS),
        compiler_params=_cparams(("parallel",)),
        name="attn_sample",
    )(z_s, *views)
    return outs[0::2], outs[1::2]


def _hgrn_consts(seg):
    n = HGRN_ROWS
    t = np.arange(n)[:, None]
    u = np.arange(n)[None, :]
    seg_lo = (t // seg) * seg
    seg_hi = seg_lo + seg - 1
    mats = [(u >= seg_lo) & (u <= t),
            (u > t) & (u <= seg_hi)]
    level = np.full((n, n), -1, np.int32)
    s = u
    li = 0
    m = seg
    while m >= 2:
        half = m // 2
        upper = (t % m) >= half
        ref = (t // m) * m + half - 1
        mats.append(np.where(upper, (u > ref) & (u <= t), (u > t) & (u <= ref)))
        pair = (t // m == s // m) & ((t % m) >= half) & ((s % m) < half)
        level[pair] = li
        li += 1
        m //= 2
    level[np.arange(n), np.arange(n)] = li
    w = np.concatenate(mats, axis=0).astype(np.float32)
    w3 = np.concatenate([w, w, w], axis=1)
    return jnp.asarray(w3, BF16), jnp.asarray(level), li


def _hgrn_block(q, xf, v, lb, w3, level, n_lev, seg, states):
    n = HGRN_ROWS
    f = lb + (1.0 - lb) * _sigmoid(xf)
    g = jnp.log(f)
    kin = 1.0 - f
    g_hi = g.astype(BF16)
    r1 = g - g_hi.astype(F32)
    g_mid = r1.astype(BF16)
    g_lo = (r1 - g_mid.astype(F32)).astype(BF16)
    fac = jnp.exp(jnp.dot(w3, jnp.concatenate([g_hi, g_mid, g_lo], axis=0),
                          preferred_element_type=F32))
    ea = fac[0:n]
    qt = (q * ea).astype(BF16)
    kh = kin * fac[n:2 * n]
    row = lax.broadcasted_iota(jnp.int32, (n, B_DIM), 0)
    att = jnp.zeros((n, n), F32)
    m = seg
    for li in range(n_lev):
        upper = (row & (m // 2)) != 0
        tm = (jnp.where(upper, q, kin) * fac[(2 + li) * n:(3 + li) * n]).astype(BF16)
        att = jnp.where(level == li, _nt(tm, tm), att)
        m //= 2
    att = jnp.where(level == n_lev, _nt(q.astype(BF16), kin.astype(BF16)), att)
    vb = v.astype(BF16)
    o = jnp.dot(att.astype(BF16), vb, preferred_element_type=F32)
    new_states = []
    n_seg = n // seg
    for si in range(n_seg):
        st = states[si]
        o_s = _nt(qt, st.astype(BF16))
        if n_seg == 1:
            o = o + o_s
            kh_s = kh
        else:
            in_seg = (row >= si * seg) & (row < (si + 1) * seg)
            o = o + jnp.where(in_seg, o_s, 0.0)
            kh_s = jnp.where(in_seg, kh, 0.0)
        decay = ea[(si + 1) * seg - 1:(si + 1) * seg, :]
        new_states.append(st * decay + _tn(vb, kh_s.astype(BF16)))
    return o, new_states


def _lower_bound(lbl_ref, layer):
    lg = lbl_ref[...]
    e = jnp.exp(lg - jnp.max(lg, axis=0, keepdims=True))
    p = e / jnp.sum(e, axis=0, keepdims=True)
    return jnp.sum(p[0:layer + 1], axis=0, keepdims=True) - p[0:1]


def _hgrn_prompt_kernel(q_ref, f_ref, v_ref, lbl_ref, gn_ref, w3_ref, lev_ref, o_ref, s_ref,
                        st_ref, *, layer, n_lev, tl, n_heads):
    li = pl.program_id(2)

    @pl.when(li == 0)
    def _():
        st_ref[...] = jnp.zeros_like(st_ref)

    lb = _lower_bound(lbl_ref, layer)
    w3 = w3_ref[...]
    level = lev_ref[...]
    gain = gn_ref[...]

    def body(c, carry):
        r0 = pl.multiple_of(c * HGRN_ROWS, HGRN_ROWS)
        for hh in range(n_heads):
            hs = slice(hh * B_DIM, (hh + 1) * B_DIM)
            q = q_ref[0, pl.ds(r0, HGRN_ROWS), hs].astype(F32)
            xf = f_ref[0, pl.ds(r0, HGRN_ROWS), hs].astype(F32)
            v = v_ref[0, pl.ds(r0, HGRN_ROWS), hs].astype(F32)
            o, (st,) = _hgrn_block(q, xf, v, lb[:, hs], w3, level, n_lev, HGRN_ROWS, [st_ref[hh]])
            st_ref[hh] = st
            o_ref[0, pl.ds(r0, HGRN_ROWS), hs] = _rmsnorm_rows(o, gain)
        return carry

    lax.fori_loop(0, tl // HGRN_ROWS, body, 0)

    @pl.when(li == pl.num_programs(2) - 1)
    def _():
        for hh in range(n_heads):
            s_ref[0, hh] = st_ref[hh].T


def _hgrn_prompt(z3, col, lb_logits, hgrn_gain, layer, *, tl):
    b, s, _ = z3.shape
    w3, level, n_lev = _hgrn_consts(HGRN_ROWS)
    nh = HGRN_HEADS_PER_STEP
    wblk = nh * B_DIM

    def cspec(name):
        c0 = col[name] // wblk
        return pl.BlockSpec((1, tl, wblk), lambda bb, h, i: (bb, i, c0 + h))

    const = lambda shape: pl.BlockSpec(shape, lambda bb, h, i: (0,) * len(shape))
    return pl.pallas_call(
        functools.partial(_hgrn_prompt_kernel, layer=layer, n_lev=n_lev, tl=tl, n_heads=nh),
        grid=(b, B_HEADS // nh, s // tl),
        in_specs=[cspec("bq"), cspec("bf"), cspec("bi"),
                  pl.BlockSpec((lb_logits.shape[0], wblk), lambda bb, h, i: (0, h)),
                  const((1, B_DIM)), const(w3.shape), const(level.shape)],
        out_specs=[pl.BlockSpec((1, tl, wblk), lambda bb, h, i: (bb, i, h)),
                   pl.BlockSpec((1, nh, B_DIM, B_DIM), lambda bb, h, i: (bb, h, 0, 0))],
        out_shape=[jax.ShapeDtypeStruct((b, s, B_WIDTH), F32),
                   jax.ShapeDtypeStruct((b, B_HEADS, B_DIM, B_DIM), F32)],
        scratch_shapes=[pltpu.VMEM((nh, B_DIM, B_DIM), F32)],
        compiler_params=_cparams(("parallel", "parallel", "arbitrary")),
        name="hgrn_prompt",
    )(z3, z3, z3, lb_logits, hgrn_gain.reshape(1, B_DIM), w3, level)


def _hgrn_sample_kernel(q_ref, f_ref, v_ref, s0_ref, lbl_ref, gn_ref, w3_ref, lev_ref,
                        o_ref, s_ref, *, layer, n_lev, t_new):
    lb = _lower_bound(lbl_ref, layer)
    n_seg = HGRN_ROWS // t_new
    states = [s0_ref[0, si, 0].T for si in range(n_seg)]
    o, new_states = _hgrn_block(q_ref[...], f_ref[...], v_ref[...], lb, w3_ref[...],
                                lev_ref[...], n_lev, t_new, states)
    o_ref[...] = _rmsnorm_rows(o, gn_ref[...])
    for si in range(n_seg):
        s_ref[si, 0] = new_states[si].T


def _hgrn_sample(z_s, state, lb_logits, hgrn_gain, layer, *, n_seq, t_new):
    w3, level, n_lev = _hgrn_consts(t_new)
    n_seg = HGRN_ROWS // t_new

    def cspec(name):
        c0 = _COL_S[name] // B_DIM
        return pl.BlockSpec((HGRN_ROWS, B_DIM), lambda i, h: (i, c0 + h))

    const = lambda shape: pl.BlockSpec(shape, lambda i, h: (0,) * len(shape))
    return pl.pallas_call(
        functools.partial(_hgrn_sample_kernel, layer=layer, n_lev=n_lev, t_new=t_new),
        grid=(n_seq // n_seg, B_HEADS),
        in_specs=[cspec("bq"), cspec("bf"), cspec("bi"),
                  pl.BlockSpec((1, n_seg, 1, B_DIM, B_DIM), lambda i, h, layer=layer: (layer, i, h, 0, 0)),
                  pl.BlockSpec((lb_logits.shape[0], B_DIM), lambda i, h: (0, h)),
                  const((1, B_DIM)), const(w3.shape), const(level.shape)],
        out_specs=[pl.BlockSpec((HGRN_ROWS, B_DIM), lambda i, h: (i, h)),
                   pl.BlockSpec((n_seg, 1, B_DIM, B_DIM), lambda i, h: (i, h, 0, 0))],
        out_shape=[jax.ShapeDtypeStruct((n_seq * t_new, B_WIDTH), F32),
                   jax.ShapeDtypeStruct((n_seq, B_HEADS, B_DIM, B_DIM), F32)],
        compiler_params=_cparams(("parallel", "parallel")),
        name="hgrn_sample",
    )(z_s, z_s, z_s, state, lb_logits, hgrn_gain.reshape(1, B_DIM), w3, level)


_POOL_PAD = 16


def _pool_mix(ext_at, u, pos, pw_ref, ps_ref):
    outs = []
    for g, win in enumerate(C_WINDOWS):
        gs = slice(g * C_GROUP_DIM, (g + 1) * C_GROUP_DIM)
        acc = u[:, gs]
        for k in range(1, win):
            acc = acc + ext_at(k, gs)
        cnt = jnp.minimum(pos + 1, win).astype(F32)
        pooled = acc * (1.0 / cnt) - u[:, gs]
        mixed = jnp.dot(pooled.astype(BF16), pw_ref[g].astype(BF16), preferred_element_type=F32)
        outs.append(mixed * ps_ref[:, gs])
    return outs


def _pool_prompt_kernel(u_ref, h_ref, pw_ref, ps_ref, o_ref, ext_ref, *, tl):
    i = pl.program_id(1)
    halo = h_ref[0].astype(F32)
    ext_ref[0:_POOL_PAD, :] = jnp.where(i > 0, halo, 0.0)
    u = u_ref[0].astype(F32)
    ext_ref[_POOL_PAD:, :] = u
    pos = i * tl + lax.broadcasted_iota(jnp.int32, (tl, 1), 0)
    outs = _pool_mix(lambda k, gs: ext_ref[_POOL_PAD - k:_POOL_PAD - k + tl, gs], u, pos, pw_ref, ps_ref)
    for g, og in enumerate(outs):
        o_ref[0, :, g * C_GROUP_DIM:(g + 1) * C_GROUP_DIM] = og


def _pool_prompt(z3, col, pool_w, pool_scale, *, tl):
    b, s, _ = z3.shape
    c0 = col["cu"] // C_WIDTH
    per = tl // _POOL_PAD
    return pl.pallas_call(
        functools.partial(_pool_prompt_kernel, tl=tl),
        grid=(b, s // tl),
        in_specs=[pl.BlockSpec((1, tl, C_WIDTH), lambda bb, i: (bb, i, c0)),
                  pl.BlockSpec((1, _POOL_PAD, C_WIDTH), lambda bb, i: (bb, jnp.maximum(i * per - 1, 0), c0)),
                  pl.BlockSpec((C_GROUPS, C_GROUP_DIM, C_GROUP_DIM), lambda bb, i: (0, 0, 0)),
                  pl.BlockSpec((1, C_WIDTH), lambda bb, i: (0, 0))],
        out_specs=pl.BlockSpec((1, tl, C_WIDTH), lambda bb, i: (bb, i, 0)),
        out_shape=jax.ShapeDtypeStruct((b, s, C_WIDTH), F32),
        scratch_shapes=[pltpu.VMEM((_POOL_PAD + tl, C_WIDTH), F32)],
        compiler_params=_cparams(("parallel", "arbitrary")),
        name="pool_prompt",
    )(z3, z3, pool_w, pool_scale.reshape(1, C_WIDTH))


def _pool_sample_kernel(u_ref, h_ref, pw_ref, ps_ref, o_ref, ext_ref, *, nb, t_new, first_pos):
    ext_ref[:, 0:1, :] = jnp.zeros((nb, 1, C_WIDTH), F32)
    ext_ref[:, 1:_POOL_PAD, :] = h_ref[0]
    u3 = u_ref[...].reshape(nb, t_new, C_WIDTH)
    ext_ref[:, _POOL_PAD:, :] = u3
    u = u_ref[...]
    pos = first_pos + lax.broadcasted_iota(jnp.int32, (nb * t_new, 1), 0) % t_new

    def ext_at(k, gs):
        return ext_ref[:, _POOL_PAD - k:_POOL_PAD - k + t_new, gs].reshape(nb * t_new, C_GROUP_DIM)

    outs = _pool_mix(ext_at, u, pos, pw_ref, ps_ref)
    for g, og in enumerate(outs):
        o_ref[:, g * C_GROUP_DIM:(g + 1) * C_GROUP_DIM] = og


def _pool_sample(z_s, state_pool, pool_w, pool_scale, layer, *, n_seq, t_new, first_pos, nb):
    c0 = _COL_S["cu"] // C_WIDTH
    return pl.pallas_call(
        functools.partial(_pool_sample_kernel, nb=nb, t_new=t_new, first_pos=first_pos),
        grid=(n_seq // nb,),
        in_specs=[pl.BlockSpec((nb * t_new, C_WIDTH), lambda i: (i, c0)),
                  pl.BlockSpec((1, nb, C_HIST, C_WIDTH), lambda i, layer=layer: (layer, i, 0, 0)),
                  pl.BlockSpec((C_GROUPS, C_GROUP_DIM, C_GROUP_DIM), lambda i: (0, 0, 0)),
                  pl.BlockSpec((1, C_WIDTH), lambda i: (0, 0))],
        out_specs=pl.BlockSpec((nb * t_new, C_WIDTH), lambda i: (i, 0)),
        out_shape=jax.ShapeDtypeStruct((n_seq * t_new, C_WIDTH), F32),
        scratch_shapes=[pltpu.VMEM((nb, _POOL_PAD + t_new, C_WIDTH), F32)],
        compiler_params=_cparams(("parallel",)),
        name="pool_sample",
    )(z_s, state_pool, pool_w, pool_scale.reshape(1, C_WIDTH))


def _token_order(ref, scr_ref, d, tm):
    if d == 1:
        return ref[0, 0]
    n_chunk = A_WIDTH // LANE
    for r in range(d):
        for c in range(n_chunk):
            scr_ref[c, pl.ds(r, tm // d, stride=d), :] = ref[0, r, :, c * LANE:(c + 1) * LANE]
    return jnp.concatenate([scr_ref[c] for c in range(n_chunk)], axis=1)


def _out_proj_kernel(x_ref, ag_ref, bg_ref, cg_ref, m0_ref, m1_ref, m2_ref, o0_ref, o1_ref, o2_ref,
                     l0_ref, l1_ref, l2_ref, ob_ref, oc_ref, wa_ref, wb_ref, wc_ref, wo_ref,
                     gp_ref, y_ref, *scratch, dil, tm):
    scr = iter(scratch)
    grp = []
    for o_ref, l_ref, d in zip((o0_ref, o1_ref, o2_ref), (l0_ref, l1_ref, l2_ref), dil):
        so = next(scr) if d > 1 else None
        sl = next(scr) if d > 1 else None
        grp.append((_token_order(o_ref, so, d, tm), _token_order(l_ref, sl, d, tm)))
    (o0, l0), (o1, l1), (o2, l2) = grp
    mx = jnp.maximum(jnp.maximum(l0, l1), l2)
    e0, e1, e2 = jnp.exp(l0 - mx), jnp.exp(l1 - mx), jnp.exp(l2 - mx)
    o_a = (e0 * o0 + e1 * o1 + e2 * o2) / (e0 + e1 + e2)
    y_a = (o_a * _silu(ag_ref[...].astype(F32))).astype(BF16)
    y_b = (ob_ref[...] * _silu(bg_ref[...].astype(F32))).astype(BF16)
    y_c = (oc_ref[...] * _silu(cg_ref[...].astype(F32))).astype(BF16)
    gate = lambda m_ref: _sigmoid(m_ref[...].astype(F32))
    merged = (gate(m0_ref) * jnp.dot(y_a, wa_ref[...], preferred_element_type=F32)
              + gate(m1_ref) * jnp.dot(y_b, wb_ref[...], preferred_element_type=F32)
              + gate(m2_ref) * jnp.dot(y_c, wc_ref[...], preferred_element_type=F32))
    out = jnp.dot(merged.astype(BF16), wo_ref[...], preferred_element_type=F32)
    y_ref[...] = x_ref[...] + _rmsnorm_rows(out, gp_ref[...])


def _out_proj(x, z, col, o_g, lse_g, o_b, o_c, wa, wb, wc, wo, gain_post, *, tm):
    n = x.shape[0]
    dil = tuple(int(a.shape[1]) for a in o_g)
    tpb = (n // o_g[0].shape[0]) // tm

    def zcol(name, width, k=0):
        assert col[name] % width == 0
        c0 = col[name] // width + k
        return pl.BlockSpec((tm, width), lambda i: (i, c0))

    def gspec(d):
        return pl.BlockSpec((1, d, tm // d, A_WIDTH), lambda i: (i // tpb, 0, i % tpb, 0))

    rows = lambda width: pl.BlockSpec((tm, width), lambda i: (i, 0))
    const = lambda a: pl.BlockSpec(a.shape, lambda i: (0,) * a.ndim)
    gp = gain_post.reshape(1, D_MODEL)
    scratch = []
    for d in dil:
        if d > 1:
            scratch += [pltpu.VMEM((A_WIDTH // LANE, tm, LANE), F32)] * 2
    return pl.pallas_call(
        functools.partial(_out_proj_kernel, dil=dil, tm=tm),
        grid=(n // tm,),
        in_specs=[rows(D_MODEL), zcol("ag", A_WIDTH), zcol("bg", B_WIDTH), zcol("cg", C_WIDTH),
                  zcol("mg", D_MODEL, 0), zcol("mg", D_MODEL, 1), zcol("mg", D_MODEL, 2)]
                 + [gspec(d) for d in dil] * 2 + [rows(B_WIDTH), rows(C_WIDTH)]
                 + [const(wa), const(wb), const(wc), const(wo), const(gp)],
        out_specs=rows(D_MODEL),
        out_shape=jax.ShapeDtypeStruct((n, D_MODEL), F32),
        scratch_shapes=scratch,
        compiler_params=_cparams(("parallel",)),
        name="out_proj",
    )(x, z, z, z, z, z, z, *o_g, *lse_g, o_b, o_c, wa, wb, wc, wo, gp)


def _ref_cols(w, name, g=None):
    off = _REF_OFF[name]
    if g is None:
        return w[:, off:off + _WIDTHS[name]]
    return w[:, off + g * A_WIDTH:off + (g + 1) * A_WIDTH]


def _prep_w_in(w):
    w_s = jnp.concatenate([_ref_cols(w, k) for k in _ORDER_S], axis=1)
    parts = [_ref_cols(w, "ak", 0), _ref_cols(w, "av", 0), _ref_cols(w, "cu"), _ref_cols(w, "cg"),
             _ref_cols(w, "aq", 0)] + [_ref_cols(w, k) for k in ("ag", "bq", "bf", "bi", "bg", "mg")]
    w_p = jnp.concatenate(parts, axis=1)
    w_g = [jnp.concatenate([_ref_cols(w, "ak", g), _ref_cols(w, "av", g), _ref_cols(w, "aq", g)], axis=1)
           for g in range(1, A_GROUPS)]
    return w_s.astype(BF16), w_p.astype(BF16), [a.astype(BF16) for a in w_g]


def _kv_split(kv):
    return kv.reshape(kv.shape[:-1] + (2, A_HEADS, A_HEAD_DIM))


def kernel(x_prompt, x_sample, cache_kv_w128, cache_kv_w512, cache_kv_w2048, state_hgrn, state_pool,
           norm_pre, norm_post, w_in, hgrn_lb_logits, hgrn_norm, pool_w, pool_scale,
           w_br_a, w_br_b, w_br_c, w_out):
    depth = w_in.shape[0]
    b_p, s_p, _ = x_prompt.shape
    n_seq, t_new, _ = x_sample.shape
    caches = (cache_kv_w128, cache_kv_w512, cache_kv_w2048)
    past_len = cache_kv_w2048.shape[2]
    assert HGRN_ROWS % t_new == 0 and n_seq % (HGRN_ROWS // t_new) == 0
    n_p, n_s = b_p * s_p, n_seq * t_new

    hp = x_prompt.reshape(n_p, D_MODEL)
    hs = x_sample.reshape(n_s, D_MODEL)
    kv_p = [[] for _ in range(A_GROUPS)]
    kv_s = [[] for _ in range(A_GROUPS)]
    hg_p, hg_s, pl_p, pl_s = [], [], [], []
    tm_p = min(1024, s_p)
    for l in range(depth):
        w_s, w_p, w_grp = _prep_w_in(w_in[l])
        wa, wb, wc, wo = (w[l].astype(BF16) for w in (w_br_a, w_br_b, w_br_c, w_out))

        z_p, zf_p = _in_proj(hp, norm_pre[l], w_p, tm=tm_p, tn=1024, z_dtype=BF16, f32_cols=_P_F32)
        z3 = z_p.reshape(b_p, s_p, _P_WIDTH)
        zf3 = zf_p.reshape(b_p, s_p, _P_F32)
        o_g, lse_g = [], []
        for g, (win, d) in enumerate(A_PATTERNS):
            rows = min(win, s_p)
            if g == 0:
                arr = z_p.reshape(b_p, 1, s_p, _P_WIDTH)
                cols = (_COL_P["aq"] // A_WIDTH, _COL_P["ak"] // A_WIDTH, _COL_P["av"] // A_WIDTH)
                kv_p[g].append(_kv_split(zf3[:, s_p - rows:, 0:2 * A_WIDTH]))
            else:
                arr, f_g = _in_proj_perm(hp, norm_pre[l], w_grp[g - 1], b=b_p, s=s_p, d=d, tm=tm_p)
                cols = (2, 0, 1)
                tail = f_g[:, :, (s_p - rows) // d:, :]
                kv_p[g].append(_kv_split(tail.transpose(0, 2, 1, 3).reshape(b_p, rows, 2 * A_WIDTH)))
            o, lse = _attn_prompt(arr, cols, tq=min(256, s_p // d))
            o_g.append(o)
            lse_g.append(lse)
        o_b, s_new = _hgrn_prompt(z3, _COL_P, hgrn_lb_logits, hgrn_norm[l], l, tl=min(1024, s_p))
        hg_p.append(s_new)
        o_c = _pool_prompt(z3, _COL_P, pool_w[l], pool_scale[l], tl=min(1024, s_p))
        pl_p.append(zf3[:, s_p - C_HIST:, _COL_P["cu"]:_COL_P["cu"] + C_WIDTH])
        hp = _out_proj(hp, z_p, _COL_P, o_g, lse_g, o_b.reshape(n_p, B_WIDTH),
                       o_c.reshape(n_p, C_WIDTH), wa, wb, wc, wo, norm_post[l], tm=min(256, s_p))

        (z_s,) = _in_proj(hs, norm_pre[l], w_s, tm=min(1024, n_s), tn=1024, z_dtype=F32, f32_cols=0)
        o_g, lse_g = _attn_sample(z_s, caches, l, n_seq=n_seq, t_new=t_new)
        zs3 = z_s.reshape(n_seq, t_new, IN_WIDTH)
        for g in range(A_GROUPS):
            k = zs3[:, :, _COL_S["ak"] + g * A_WIDTH:_COL_S["ak"] + (g + 1) * A_WIDTH]
            v = zs3[:, :, _COL_S["av"] + g * A_WIDTH:_COL_S["av"] + (g + 1) * A_WIDTH]
            kv_s[g].append(_kv_split(jnp.concatenate([k, v], axis=-1)))
        o_b, s_new = _hgrn_sample(z_s, state_hgrn, hgrn_lb_logits, hgrn_norm[l], l,
                                  n_seq=n_seq, t_new=t_new)
        hg_s.append(s_new)
        o_c = _pool_sample(z_s, state_pool, pool_w[l], pool_scale[l], l, n_seq=n_seq,
                           t_new=t_new, first_pos=past_len, nb=HGRN_ROWS // t_new)
        u_new = zs3[:, :, _COL_S["cu"]:_COL_S["cu"] + C_WIDTH]
        pl_s.append(jnp.concatenate([state_pool[l], u_new], axis=1)[:, -C_HIST:])
        as4 = lambda a: a.reshape(1, 1, n_s, A_WIDTH)
        hs = _out_proj(hs, z_s, _COL_S, [as4(a) for a in o_g], [as4(a) for a in lse_g], o_b, o_c,
                       wa, wb, wc, wo, norm_post[l], tm=min(256, n_s))

    return (hp.reshape(b_p, s_p, D_MODEL), hs.reshape(n_seq, t_new, D_MODEL),
            jnp.stack(kv_p[0]), jnp.stack(kv_p[1]), jnp.stack(kv_p[2]),
            jnp.stack(kv_s[0]), jnp.stack(kv_s[1]), jnp.stack(kv_s[2]),
            jnp.stack(hg_p), jnp.stack(hg_s), jnp.stack(pl_p), jnp.stack(pl_s))
```

```python
import functools

import numpy as np
import jax
import jax.numpy as jnp
from jax import lax
from jax.experimental import pallas as pl
from jax.experimental.pallas import tpu as pltpu

F32 = jnp.float32
BF16 = jnp.bfloat16

D_MODEL = 1024
A_PATTERNS = ((128, 1), (512, 4), (2048, 16))
A_GROUPS = 3
A_HEADS = 4
A_HEAD_DIM = 128
A_WIDTH = A_HEADS * A_HEAD_DIM
N_STEPS = 128
B_HEADS = 8
B_DIM = 128
B_WIDTH = B_HEADS * B_DIM
C_WINDOWS = (2, 4, 8, 16)
C_GROUPS = 4
C_GROUP_DIM = 128
C_WIDTH = C_GROUPS * C_GROUP_DIM
C_HIST = 15
N_BRANCH = 3
EPS = 1e-6
NEG = -1e30
IN_WIDTH = 13312
LANE = 128
HGRN_ROWS = 128
HGRN_HEADS_PER_STEP = 8

_REF_OFF = dict(aq=0, ak=1536, av=3072, ag=4608, bq=5120, bf=6144, bi=7168, bg=8192,
                cu=9216, cg=9728, mg=10240)
_WIDTHS = dict(aq=1536, ak=1536, av=1536, ag=512, bq=1024, bf=1024, bi=1024, bg=1024,
               cu=512, cg=512, mg=3072)

_ORDER_S = ("ak", "av", "cu", "cg", "aq", "ag", "bq", "bf", "bi", "bg", "mg")
_COL_S = dict(ak=0, av=1536, cu=3072, cg=3584, aq=4096, ag=5632, bq=6144, bf=7168,
              bi=8192, bg=9216, mg=10240)
_COL_P = dict(ak=0, av=512, cu=1024, cg=1536, aq=2048, ag=2560, bq=3072, bf=4096,
              bi=5120, bg=6144, mg=7168)
_P_WIDTH = 10240
_P_F32 = 2048
_PERM_WIDTH = 3 * A_WIDTH

_VMEM_LIMIT = 56 * 1024 * 1024


def _cparams(sem):
    return pltpu.CompilerParams(dimension_semantics=sem, vmem_limit_bytes=_VMEM_LIMIT)


def _nt(a, b):
    return lax.dot_general(a, b, (((1,), (1,)), ((), ())), preferred_element_type=F32)


def _tn(a, b):
    return lax.dot_general(a, b, (((0,), (0,)), ((), ())), preferred_element_type=F32)


def _silu(x):
    return x / (1.0 + jnp.exp(-x))


def _sigmoid(x):
    return 1.0 / (1.0 + jnp.exp(-x))


def _rmsnorm_rows(x, gain):
    return x * lax.rsqrt(jnp.mean(x * x, axis=-1, keepdims=True) + EPS) * gain


def _in_proj_kernel(x_ref, g_ref, w_ref, *refs, n_f32_tiles, z_dtype):
    if n_f32_tiles:
        z_ref, zf_ref, xn_ref = refs
    else:
        z_ref, xn_ref = refs
    j = pl.program_id(1)

    @pl.when(j == 0)
    def _():
        xn_ref[...] = _rmsnorm_rows(x_ref[...], g_ref[...]).astype(BF16)

    acc = jnp.dot(xn_ref[...], w_ref[...], preferred_element_type=F32)
    z_ref[...] = acc.astype(z_dtype)
    if n_f32_tiles:
        @pl.when(j < n_f32_tiles)
        def _():
            zf_ref[...] = acc


def _in_proj(x, gain, w_bf16, *, tm, tn, z_dtype, f32_cols):
    n = x.shape[0]
    width = w_bf16.shape[1]
    n_f32_tiles = f32_cols // tn
    out_shape = [jax.ShapeDtypeStruct((n, width), z_dtype)]
    out_specs = [pl.BlockSpec((tm, tn), lambda i, j: (i, j))]
    if n_f32_tiles:
        out_shape.append(jax.ShapeDtypeStruct((n, f32_cols), F32))
        out_specs.append(pl.BlockSpec((tm, tn), lambda i, j: (i, jnp.minimum(j, n_f32_tiles - 1))))
    return pl.pallas_call(
        functools.partial(_in_proj_kernel, n_f32_tiles=n_f32_tiles, z_dtype=z_dtype),
        grid=(n // tm, width // tn),
        in_specs=[pl.BlockSpec((tm, D_MODEL), lambda i, j: (i, 0)),
                  pl.BlockSpec((1, D_MODEL), lambda i, j: (0, 0)),
                  pl.BlockSpec((D_MODEL, tn), lambda i, j: (0, j))],
        out_specs=out_specs,
        out_shape=out_shape,
        scratch_shapes=[pltpu.VMEM((tm, D_MODEL), BF16)],
        compiler_params=_cparams(("parallel", "arbitrary")),
        name="in_proj",
    )(x, gain.reshape(1, D_MODEL), w_bf16)


def _in_proj_perm_kernel(x_ref, g_ref, w_ref, r_ref, f_ref, xs_ref, xn_ref, *, d, tm):
    xn = _rmsnorm_rows(x_ref[...], g_ref[...])
    n_chunk = D_MODEL // LANE
    rows = tm // d
    for c in range(n_chunk):
        xs_ref[c] = xn[:, c * LANE:(c + 1) * LANE]
    for r in range(d):
        for c in range(n_chunk):
            xn_ref[r * rows:(r + 1) * rows, c * LANE:(c + 1) * LANE] = (
                xs_ref[c, pl.ds(r, rows, stride=d), :].astype(BF16))
    acc = jnp.dot(xn_ref[...], w_ref[...], preferred_element_type=F32)
    r_ref[0] = acc.astype(BF16).reshape(d, rows, _PERM_WIDTH)
    f_ref[0] = acc[:, 0:2 * A_WIDTH].reshape(d, rows, 2 * A_WIDTH)


def _in_proj_perm(x, gain, w_g, *, b, s, d, tm):
    tpb = s // tm
    rows = tm // d
    idx = lambda i: (i // tpb, 0, i % tpb, 0)
    return pl.pallas_call(
        functools.partial(_in_proj_perm_kernel, d=d, tm=tm),
        grid=(b * tpb,),
        in_specs=[pl.BlockSpec((tm, D_MODEL), lambda i: (i, 0)),
                  pl.BlockSpec((1, D_MODEL), lambda i: (0, 0)),
                  pl.BlockSpec((D_MODEL, _PERM_WIDTH), lambda i: (0, 0))],
        out_specs=[pl.BlockSpec((1, d, rows, _PERM_WIDTH), idx),
                   pl.BlockSpec((1, d, rows, 2 * A_WIDTH), idx)],
        out_shape=[jax.ShapeDtypeStruct((b, d, s // d, _PERM_WIDTH), BF16),
                   jax.ShapeDtypeStruct((b, d, s // d, 2 * A_WIDTH), F32)],
        scratch_shapes=[pltpu.VMEM((D_MODEL // LANE, tm, LANE), F32),
                        pltpu.VMEM((tm, D_MODEL), BF16)],
        compiler_params=_cparams(("parallel",)),
        name="in_proj_d%d" % d,
    )(x, gain.reshape(1, D_MODEL), w_g)


def _attn_prompt_kernel(q_ref, kp_ref, kc_ref, vp_ref, vc_ref, o_ref, lse_ref, *, tq):
    i = pl.program_id(2)
    n = N_STEPS
    qi = lax.broadcasted_iota(jnp.int32, (n, 2 * n), 0)
    kj = lax.broadcasted_iota(jnp.int32, (n, 2 * n), 1)
    band = (kj >= qi) & (kj <= qi + n)
    band_first = band & ((i > 0) | (kj >= n))
    scale = A_HEAD_DIM ** -0.5
    for h in range(A_HEADS):
        hs = slice(h * A_HEAD_DIM, (h + 1) * A_HEAD_DIM)
        for j in range(tq // n):
            rows = slice(j * n, (j + 1) * n)
            q = q_ref[0, 0, rows, hs]
            if j == 0:
                k = jnp.concatenate([kp_ref[0, 0, :, hs], kc_ref[0, 0, rows, hs]], axis=0)
                v = jnp.concatenate([vp_ref[0, 0, :, hs], vc_ref[0, 0, rows, hs]], axis=0)
                valid = band_first
            else:
                k = kc_ref[0, 0, (j - 1) * n:(j + 1) * n, hs]
                v = vc_ref[0, 0, (j - 1) * n:(j + 1) * n, hs]
                valid = band
            s = jnp.where(valid, _nt(q, k) * scale, NEG)
            m = jnp.max(s, axis=-1, keepdims=True)
            p = jnp.exp(s - m)
            l = jnp.sum(p, axis=-1, keepdims=True)
            o = jnp.dot(p.astype(BF16), v, preferred_element_type=F32)
            o_ref[0, 0, rows, hs] = o / l
            lse_ref[0, 0, rows, hs] = jnp.broadcast_to(m + jnp.log(l), (n, A_HEAD_DIM))


def _attn_prompt(arr, cols, *, tq):
    b, d, rows, _ = arr.shape
    qc, kc, vc = cols
    prev_per_blk = tq // N_STEPS

    def cur(c):
        return pl.BlockSpec((1, 1, tq, A_WIDTH), lambda bb, r, i: (bb, r, i, c))

    def prev(c):
        return pl.BlockSpec((1, 1, N_STEPS, A_WIDTH),
                            lambda bb, r, i: (bb, r, jnp.maximum(i * prev_per_blk - 1, 0), c))

    out_spec = pl.BlockSpec((1, 1, tq, A_WIDTH), lambda bb, r, i: (bb, r, i, 0))
    return pl.pallas_call(
        functools.partial(_attn_prompt_kernel, tq=tq),
        grid=(b, d, rows // tq),
        in_specs=[cur(qc), prev(kc), cur(kc), prev(vc), cur(vc)],
        out_specs=[out_spec, out_spec],
        out_shape=[jax.ShapeDtypeStruct((b, d, rows, A_WIDTH), F32)] * 2,
        compiler_params=_cparams(("parallel", "parallel", "arbitrary")),
        name="attn_prompt_d%d" % d,
    )(arr, arr, arr, arr, arr)


def _gather_pitch(n_j):
    p = -(-n_j // 8)
    return 8 * (p if p % 2 else p + 1)


def _attn_sample_kernel(z_ref, c0_ref, c1_ref, c2_ref, *refs, t_new):
    out_refs, flat_refs = refs[:2 * A_GROUPS], refs[2 * A_GROUPS:]
    scale = A_HEAD_DIM ** -0.5
    nrow = A_HEADS * t_new
    per_res = 2 * A_HEADS
    for g, c_ref in enumerate((c0_ref, c1_ref, c2_ref)):
        d = A_PATTERNS[g][1]
        n_res = min(d, t_new)
        o_ref, lse_ref = out_refs[2 * g], out_refs[2 * g + 1]
        q = z_ref[:, _COL_S["aq"] + g * A_WIDTH:_COL_S["aq"] + (g + 1) * A_WIDTH]
        kn = z_ref[:, _COL_S["ak"] + g * A_WIDTH:_COL_S["ak"] + (g + 1) * A_WIDTH]
        vn = z_ref[:, _COL_S["av"] + g * A_WIDTH:_COL_S["av"] + (g + 1) * A_WIDTH]
        n_j = n_res * per_res
        pitch = _gather_pitch(n_j)
        flat_ref = flat_refs[g]
        for mm in range(N_STEPS):
            flat_ref[mm * pitch:mm * pitch + n_j, :] = c_ref[mm]
        gather = lambda j: flat_ref[pl.ds(j, N_STEPS, stride=pitch), :]
        kcat = jnp.concatenate([gather(r * per_res + h)
                                for r in range(n_res) for h in range(A_HEADS)], axis=1).astype(BF16)
        vcat = jnp.concatenate([gather(r * per_res + A_HEADS + h)
                                for r in range(n_res) for h in range(A_HEADS)], axis=1).astype(BF16)
        wide = n_res * A_WIDTH
        q4 = jnp.concatenate([q] * A_HEADS, axis=0)
        row = lax.broadcasted_iota(jnp.int32, (nrow, wide), 0)
        col = lax.broadcasted_iota(jnp.int32, (nrow, wide), 1)
        row_h, row_t = row // t_new, row % t_new
        sel = ((col // A_WIDTH) == (row_t % d)) & (((col % A_WIDTH) // A_HEAD_DIM) == row_h)
        qbd = jnp.where(sel, jnp.concatenate([q4] * n_res, axis=1), 0.0).astype(BF16)
        s_c = _nt(qbd, kcat) * scale
        r2 = lax.broadcasted_iota(jnp.int32, (nrow, N_STEPS), 0) % t_new
        c2 = lax.broadcasted_iota(jnp.int32, (nrow, N_STEPS), 1)
        s_c = jnp.where(c2 >= r2 // d, s_c, NEG)
        pad = jnp.zeros((N_STEPS - t_new, A_WIDTH), F32)
        knp = jnp.concatenate([kn, pad], axis=0).astype(BF16)
        vnp = jnp.concatenate([vn, pad], axis=0).astype(BF16)
        rowq = lax.broadcasted_iota(jnp.int32, (nrow, A_WIDTH), 0) // t_new
        colq = lax.broadcasted_iota(jnp.int32, (nrow, A_WIDTH), 1) // A_HEAD_DIM
        selq = rowq == colq
        qbd2 = jnp.where(selq, q4, 0.0).astype(BF16)
        s_n = _nt(qbd2, knp) * scale
        s_n = jnp.where((c2 <= r2) & ((r2 - c2) % d == 0), s_n, NEG)
        m = jnp.maximum(jnp.max(s_c, axis=-1, keepdims=True), jnp.max(s_n, axis=-1, keepdims=True))
        p_c = jnp.exp(s_c - m)
        p_n = jnp.exp(s_n - m)
        l = jnp.sum(p_c, axis=-1, keepdims=True) + jnp.sum(p_n, axis=-1, keepdims=True)
        o_wide = jnp.where(sel, jnp.dot(p_c.astype(BF16), vcat, preferred_element_type=F32), 0.0)
        o_new = jnp.where(selq, jnp.dot(p_n.astype(BF16), vnp, preferred_element_type=F32), 0.0)
        o = o_new[:, 0:A_HEAD_DIM]
        for c in range(1, A_HEADS):
            o = o + o_new[:, c * A_HEAD_DIM:(c + 1) * A_HEAD_DIM]
        for c in range(wide // A_HEAD_DIM):
            o = o + o_wide[:, c * A_HEAD_DIM:(c + 1) * A_HEAD_DIM]
        o = o / l
        lse = jnp.broadcast_to(m + jnp.log(l), (nrow, A_HEAD_DIM))
        for h in range(A_HEADS):
            hs = slice(h * A_HEAD_DIM, (h + 1) * A_HEAD_DIM)
            o_ref[:, hs] = o[h * t_new:(h + 1) * t_new]
            lse_ref[:, hs] = lse[h * t_new:(h + 1) * t_new]


def _attn_sample(z_s, caches, layer, *, n_seq, t_new):
    in_specs = [pl.BlockSpec((t_new, IN_WIDTH), lambda b: (b, 0))]
    views, scratch = [], []
    per_res = 2 * A_HEADS
    for g, c in enumerate(caches):
        win, d = A_PATTERNS[g]
        assert c.shape[2] == win == N_STEPS * d, "cache must hold one full window"
        n_res = min(d, t_new)
        views.append(c.reshape(c.shape[0], n_seq, N_STEPS, d * per_res, A_HEAD_DIM))
        in_specs.append(pl.BlockSpec((None, None, N_STEPS, n_res * per_res, A_HEAD_DIM),
                                     lambda b, layer=layer: (layer, b, 0, 0, 0)))
        scratch.append(pltpu.VMEM((N_STEPS * _gather_pitch(n_res * per_res), A_HEAD_DIM), F32))
    out_spec = pl.BlockSpec((t_new, A_WIDTH), lambda b: (b, 0))
    outs = pl.pallas_call(
        functools.partial(_attn_sample_kernel, t_new=t_new),
        grid=(n_seq,),
        in_specs=in_specs,
        out_specs=[out_spec] * (2 * A_GROUPS),
        out_shape=[jax.ShapeDtypeStruct((n_seq * t_new, A_WIDTH), F32)] * (2 * A_GROUPS),
        scratch_shapes=scratch,
        compiler_params=_cparams(("parallel",)),
        name="attn_sample",
    )(z_s, *views)
    return outs[0::2], outs[1::2]


def _hgrn_consts(seg):
    n = HGRN_ROWS
    t = np.arange(n)[:, None]
    u = np.arange(n)[None, :]
    seg_lo = (t // seg) * seg
    seg_hi = seg_lo + seg - 1
    mats = [(u >= seg_lo) & (u <= t),
            (u > t) & (u <= seg_hi)]
    level = np.full((n, n), -1, np.int32)
    s = u
    li = 0
    m = seg
    while m >= 2:
        half = m // 2
        upper = (t % m) >= half
        ref = (t // m) * m + half - 1
        if m > 2:
            mats.append(np.where(upper, (u > ref) & (u <= t), (u > t) & (u <= ref)))
        pair = (t // m == s // m) & ((t % m) >= half) & ((s % m) < half)
        level[pair] = li
        li += 1
        m //= 2
    level[np.arange(n), np.arange(n)] = li
    w = np.concatenate(mats, axis=0).astype(np.float32)
    w2 = np.concatenate([w, w], axis=1)
    return jnp.asarray(w2, BF16), jnp.asarray(level), li


def _hgrn_blocks(items, w2, level, n_lev, seg):
    n = HGRN_ROWS
    n_seg = n // seg
    row = lax.broadcasted_iota(jnp.int32, (n, B_DIM), 0)

    stage1 = []
    for q, xf, v, lb, states in items:
        f = lb + (1.0 - lb) * _sigmoid(xf)
        g = jnp.log(f)
        g_hi = g.astype(BF16)
        g_lo = (g - g_hi.astype(F32)).astype(BF16)
        dsum = jnp.dot(w2, jnp.concatenate([g_hi, g_lo], axis=0), preferred_element_type=F32)
        stage1.append((f, 1.0 - f, dsum))

    stage2 = []
    for (q, xf, v, lb, states), (f, kin, dsum) in zip(items, stage1):
        fac = jnp.exp(dsum)
        ea = fac[0:n]
        ops = []
        m = seg
        for li in range(n_lev):
            upper = (row & (m // 2)) != 0
            if m > 2:
                tm = jnp.where(upper, q, kin) * fac[(2 + li) * n:(3 + li) * n]
            else:
                tm = jnp.where(upper, q * f, kin)
            ops.append(tm.astype(BF16))
            m //= 2
        stage2.append((ea, (q * ea).astype(BF16), kin * fac[n:2 * n], ops))

    stage3 = []
    for (q, xf, v, lb, states), (f, kin, dsum), (ea, qt, kh, ops) in zip(items, stage1, stage2):
        att = jnp.zeros((n, n), F32)
        for li, tm in enumerate(ops):
            att = jnp.where(level == li, _nt(tm, tm), att)
        att = jnp.where(level == n_lev, _nt(q.astype(BF16), kin.astype(BF16)), att)
        stage3.append(att.astype(BF16))

    results = []
    for (q, xf, v, lb, states), (ea, qt, kh, ops), att in zip(items, stage2, stage3):
        vb = v.astype(BF16)
        o = jnp.dot(att, vb, preferred_element_type=F32)
        new_states = []
        for si in range(n_seg):
            st = states[si]
            o_s = _nt(qt, st.astype(BF16))
            if n_seg == 1:
                o = o + o_s
                kh_s = kh
            else:
                in_seg = (row >= si * seg) & (row < (si + 1) * seg)
                o = o + jnp.where(in_seg, o_s, 0.0)
                kh_s = jnp.where(in_seg, kh, 0.0)
            decay = ea[(si + 1) * seg - 1:(si + 1) * seg, :]
            new_states.append(st * decay + _tn(vb, kh_s.astype(BF16)))
        results.append((o, new_states))
    return results


def _lower_bound(lbl_ref, layer):
    lg = lbl_ref[...]
    e = jnp.exp(lg - jnp.max(lg, axis=0, keepdims=True))
    p = e / jnp.sum(e, axis=0, keepdims=True)
    return jnp.sum(p[0:layer + 1], axis=0, keepdims=True) - p[0:1]


def _hgrn_prompt_kernel(q_ref, f_ref, v_ref, lbl_ref, gn_ref, w3_ref, lev_ref, o_ref, s_ref,
                        st_ref, *, layer, n_lev, tl, n_heads):
    li = pl.program_id(2)

    @pl.when(li == 0)
    def _():
        st_ref[...] = jnp.zeros_like(st_ref)

    lb = _lower_bound(lbl_ref, layer)
    w3 = w3_ref[...]
    level = lev_ref[...]
    gain = gn_ref[...]
    head = [slice(hh * B_DIM, (hh + 1) * B_DIM) for hh in range(n_heads)]

    def body(c, carry):
        r0 = pl.multiple_of(c * HGRN_ROWS, HGRN_ROWS)
        rows = pl.ds(r0, HGRN_ROWS)
        items = [(q_ref[0, rows, hs].astype(F32), f_ref[0, rows, hs].astype(F32),
                  v_ref[0, rows, hs].astype(F32), lb[:, hs], [st_ref[hh]])
                 for hh, hs in enumerate(head)]
        for hh, (o, (st,)) in enumerate(_hgrn_blocks(items, w3, level, n_lev, HGRN_ROWS)):
            st_ref[hh] = st
            o_ref[0, rows, head[hh]] = _rmsnorm_rows(o, gain)
        return carry

    lax.fori_loop(0, tl // HGRN_ROWS, body, 0)

    @pl.when(li == pl.num_programs(2) - 1)
    def _():
        for hh in range(n_heads):
            s_ref[0, hh] = st_ref[hh].T


def _hgrn_prompt(z3, col, lb_logits, hgrn_gain, layer, *, tl):
    b, s, _ = z3.shape
    w3, level, n_lev = _hgrn_consts(HGRN_ROWS)
    nh = HGRN_HEADS_PER_STEP
    wblk = nh * B_DIM

    def cspec(name):
        c0 = col[name] // wblk
        return pl.BlockSpec((1, tl, wblk), lambda bb, h, i: (bb, i, c0 + h))

    const = lambda shape: pl.BlockSpec(shape, lambda bb, h, i: (0,) * len(shape))
    return pl.pallas_call(
        functools.partial(_hgrn_prompt_kernel, layer=layer, n_lev=n_lev, tl=tl, n_heads=nh),
        grid=(b, B_HEADS // nh, s // tl),
        in_specs=[cspec("bq"), cspec("bf"), cspec("bi"),
                  pl.BlockSpec((lb_logits.shape[0], wblk), lambda bb, h, i: (0, h)),
                  const((1, B_DIM)), const(w3.shape), const(level.shape)],
        out_specs=[pl.BlockSpec((1, tl, wblk), lambda bb, h, i: (bb, i, h)),
                   pl.BlockSpec((1, nh, B_DIM, B_DIM), lambda bb, h, i: (bb, h, 0, 0))],
        out_shape=[jax.ShapeDtypeStruct((b, s, B_WIDTH), F32),
                   jax.ShapeDtypeStruct((b, B_HEADS, B_DIM, B_DIM), F32)],
        scratch_shapes=[pltpu.VMEM((nh, B_DIM, B_DIM), F32)],
        compiler_params=_cparams(("parallel", "parallel", "arbitrary")),
        name="hgrn_prompt",
    )(z3, z3, z3, lb_logits, hgrn_gain.reshape(1, B_DIM), w3, level)


def _hgrn_sample_kernel(q_ref, f_ref, v_ref, s0_ref, lbl_ref, gn_ref, w3_ref, lev_ref,
                        o_ref, s_ref, *, layer, n_lev, t_new):
    lb = _lower_bound(lbl_ref, layer)
    n_seg = HGRN_ROWS // t_new
    states = [s0_ref[0, si, 0].T for si in range(n_seg)]
    ((o, new_states),) = _hgrn_blocks([(q_ref[...], f_ref[...], v_ref[...], lb, states)],
                                      w3_ref[...], lev_ref[...], n_lev, t_new)
    o_ref[...] = _rmsnorm_rows(o, gn_ref[...])
    for si in range(n_seg):
        s_ref[si, 0] = new_states[si].T


def _hgrn_sample(z_s, state, lb_logits, hgrn_gain, layer, *, n_seq, t_new):
    w3, level, n_lev = _hgrn_consts(t_new)
    n_seg = HGRN_ROWS // t_new

    def cspec(name):
        c0 = _COL_S[name] // B_DIM
        return pl.BlockSpec((HGRN_ROWS, B_DIM), lambda i, h: (i, c0 + h))

    const = lambda shape: pl.BlockSpec(shape, lambda i, h: (0,) * len(shape))
    return pl.pallas_call(
        functools.partial(_hgrn_sample_kernel, layer=layer, n_lev=n_lev, t_new=t_new),
        grid=(n_seq // n_seg, B_HEADS),
        in_specs=[cspec("bq"), cspec("bf"), cspec("bi"),
                  pl.BlockSpec((1, n_seg, 1, B_DIM, B_DIM), lambda i, h, layer=layer: (layer, i, h, 0, 0)),
                  pl.BlockSpec((lb_logits.shape[0], B_DIM), lambda i, h: (0, h)),
                  const((1, B_DIM)), const(w3.shape), const(level.shape)],
        out_specs=[pl.BlockSpec((HGRN_ROWS, B_DIM), lambda i, h: (i, h)),
                   pl.BlockSpec((n_seg, 1, B_DIM, B_DIM), lambda i, h: (i, h, 0, 0))],
        out_shape=[jax.ShapeDtypeStruct((n_seq * t_new, B_WIDTH), F32),
                   jax.ShapeDtypeStruct((n_seq, B_HEADS, B_DIM, B_DIM), F32)],
        compiler_params=_cparams(("parallel", "parallel")),
        name="hgrn_sample",
    )(z_s, z_s, z_s, state, lb_logits, hgrn_gain.reshape(1, B_DIM), w3, level)


_POOL_PAD = 16


def _pool_mix(ext_at, u, pos, pw_ref, ps_ref):
    outs = []
    for g, win in enumerate(C_WINDOWS):
        gs = slice(g * C_GROUP_DIM, (g + 1) * C_GROUP_DIM)
        acc = u[:, gs]
        for k in range(1, win):
            acc = acc + ext_at(k, gs)
        cnt = jnp.minimum(pos + 1, win).astype(F32)
        pooled = acc * (1.0 / cnt) - u[:, gs]
        mixed = jnp.dot(pooled.astype(BF16), pw_ref[g].astype(BF16), preferred_element_type=F32)
        outs.append(mixed * ps_ref[:, gs])
    return outs


def _pool_prompt_kernel(u_ref, h_ref, pw_ref, ps_ref, o_ref, ext_ref, *, tl):
    i = pl.program_id(1)
    halo = h_ref[0].astype(F32)
    ext_ref[0:_POOL_PAD, :] = jnp.where(i > 0, halo, 0.0)
    u = u_ref[0].astype(F32)
    ext_ref[_POOL_PAD:, :] = u
    pos = i * tl + lax.broadcasted_iota(jnp.int32, (tl, 1), 0)
    outs = _pool_mix(lambda k, gs: ext_ref[_POOL_PAD - k:_POOL_PAD - k + tl, gs], u, pos, pw_ref, ps_ref)
    for g, og in enumerate(outs):
        o_ref[0, :, g * C_GROUP_DIM:(g + 1) * C_GROUP_DIM] = og


def _pool_prompt(z3, col, pool_w, pool_scale, *, tl):
    b, s, _ = z3.shape
    c0 = col["cu"] // C_WIDTH
    per = tl // _POOL_PAD
    return pl.pallas_call(
        functools.partial(_pool_prompt_kernel, tl=tl),
        grid=(b, s // tl),
        in_specs=[pl.BlockSpec((1, tl, C_WIDTH), lambda bb, i: (bb, i, c0)),
                  pl.BlockSpec((1, _POOL_PAD, C_WIDTH), lambda bb, i: (bb, jnp.maximum(i * per - 1, 0), c0)),
                  pl.BlockSpec((C_GROUPS, C_GROUP_DIM, C_GROUP_DIM), lambda bb, i: (0, 0, 0)),
                  pl.BlockSpec((1, C_WIDTH), lambda bb, i: (0, 0))],
        out_specs=pl.BlockSpec((1, tl, C_WIDTH), lambda bb, i: (bb, i, 0)),
        out_shape=jax.ShapeDtypeStruct((b, s, C_WIDTH), F32),
        scratch_shapes=[pltpu.VMEM((_POOL_PAD + tl, C_WIDTH), F32)],
        compiler_params=_cparams(("parallel", "arbitrary")),
        name="pool_prompt",
    )(z3, z3, pool_w, pool_scale.reshape(1, C_WIDTH))


def _pool_sample_kernel(u_ref, h_ref, pw_ref, ps_ref, o_ref, ext_ref, *, nb, t_new, first_pos):
    ext_ref[:, 0:1, :] = jnp.zeros((nb, 1, C_WIDTH), F32)
    ext_ref[:, 1:_POOL_PAD, :] = h_ref[0]
    u3 = u_ref[...].reshape(nb, t_new, C_WIDTH)
    ext_ref[:, _POOL_PAD:, :] = u3
    u = u_ref[...]
    pos = first_pos + lax.broadcasted_iota(jnp.int32, (nb * t_new, 1), 0) % t_new

    def ext_at(k, gs):
        return ext_ref[:, _POOL_PAD - k:_POOL_PAD - k + t_new, gs].reshape(nb * t_new, C_GROUP_DIM)

    outs = _pool_mix(ext_at, u, pos, pw_ref, ps_ref)
    for g, og in enumerate(outs):
        o_ref[:, g * C_GROUP_DIM:(g + 1) * C_GROUP_DIM] = og


def _pool_sample(z_s, state_pool, pool_w, pool_scale, layer, *, n_seq, t_new, first_pos, nb):
    c0 = _COL_S["cu"] // C_WIDTH
    return pl.pallas_call(
        functools.partial(_pool_sample_kernel, nb=nb, t_new=t_new, first_pos=first_pos),
        grid=(n_seq // nb,),
        in_specs=[pl.BlockSpec((nb * t_new, C_WIDTH), lambda i: (i, c0)),
                  pl.BlockSpec((1, nb, C_HIST, C_WIDTH), lambda i, layer=layer: (layer, i, 0, 0)),
                  pl.BlockSpec((C_GROUPS, C_GROUP_DIM, C_GROUP_DIM), lambda i: (0, 0, 0)),
                  pl.BlockSpec((1, C_WIDTH), lambda i: (0, 0))],
        out_specs=pl.BlockSpec((nb * t_new, C_WIDTH), lambda i: (i, 0)),
        out_shape=jax.ShapeDtypeStruct((n_seq * t_new, C_WIDTH), F32),
        scratch_shapes=[pltpu.VMEM((nb, _POOL_PAD + t_new, C_WIDTH), F32)],
        compiler_params=_cparams(("parallel",)),
        name="pool_sample",
    )(z_s, state_pool, pool_w, pool_scale.reshape(1, C_WIDTH))


def _token_order(ref, scr_ref, d, tm):
    if d == 1:
        return ref[0, 0]
    n_chunk = A_WIDTH // LANE
    for r in range(d):
        for c in range(n_chunk):
            scr_ref[c, pl.ds(r, tm // d, stride=d), :] = ref[0, r, :, c * LANE:(c + 1) * LANE]
    return jnp.concatenate([scr_ref[c] for c in range(n_chunk)], axis=1)


def _out_proj_kernel(x_ref, ag_ref, bg_ref, cg_ref, m0_ref, m1_ref, m2_ref, o0_ref, o1_ref, o2_ref,
                     l0_ref, l1_ref, l2_ref, ob_ref, oc_ref, wa_ref, wb_ref, wc_ref, wo_ref,
                     gp_ref, y_ref, *scratch, dil, tm):
    scr = iter(scratch)
    grp = []
    for o_ref, l_ref, d in zip((o0_ref, o1_ref, o2_ref), (l0_ref, l1_ref, l2_ref), dil):
        so = next(scr) if d > 1 else None
        sl = next(scr) if d > 1 else None
        grp.append((_token_order(o_ref, so, d, tm), _token_order(l_ref, sl, d, tm)))
    (o0, l0), (o1, l1), (o2, l2) = grp
    mx = jnp.maximum(jnp.maximum(l0, l1), l2)
    e0, e1, e2 = jnp.exp(l0 - mx), jnp.exp(l1 - mx), jnp.exp(l2 - mx)
    o_a = (e0 * o0 + e1 * o1 + e2 * o2) / (e0 + e1 + e2)
    y_a = (o_a * _silu(ag_ref[...].astype(F32))).astype(BF16)
    y_b = (ob_ref[...] * _silu(bg_ref[...].astype(F32))).astype(BF16)
    y_c = (oc_ref[...] * _silu(cg_ref[...].astype(F32))).astype(BF16)
    gate = lambda m_ref: _sigmoid(m_ref[...].astype(F32))
    merged = (gate(m0_ref) * jnp.dot(y_a, wa_ref[...], preferred_element_type=F32)
              + gate(m1_ref) * jnp.dot(y_b, wb_ref[...], preferred_element_type=F32)
              + gate(m2_ref) * jnp.dot(y_c, wc_ref[...], preferred_element_type=F32))
    out = jnp.dot(merged.astype(BF16), wo_ref[...], preferred_element_type=F32)
    y_ref[...] = x_ref[...] + _rmsnorm_rows(out, gp_ref[...])


def _out_proj(x, z, col, o_g, lse_g, o_b, o_c, wa, wb, wc, wo, gain_post, *, tm):
    n = x.shape[0]
    dil = tuple(int(a.shape[1]) for a in o_g)
    tpb = (n // o_g[0].shape[0]) // tm

    def zcol(name, width, k=0):
        assert col[name] % width == 0
        c0 = col[name] // width + k
        return pl.BlockSpec((tm, width), lambda i: (i, c0))

    def gspec(d):
        return pl.BlockSpec((1, d, tm // d, A_WIDTH), lambda i: (i // tpb, 0, i % tpb, 0))

    rows = lambda width: pl.BlockSpec((tm, width), lambda i: (i, 0))
    const = lambda a: pl.BlockSpec(a.shape, lambda i: (0,) * a.ndim)
    gp = gain_post.reshape(1, D_MODEL)
    scratch = []
    for d in dil:
        if d > 1:
            scratch += [pltpu.VMEM((A_WIDTH // LANE, tm, LANE), F32)] * 2
    return pl.pallas_call(
        functools.partial(_out_proj_kernel, dil=dil, tm=tm),
        grid=(n // tm,),
        in_specs=[rows(D_MODEL), zcol("ag", A_WIDTH), zcol("bg", B_WIDTH), zcol("cg", C_WIDTH),
                  zcol("mg", D_MODEL, 0), zcol("mg", D_MODEL, 1), zcol("mg", D_MODEL, 2)]
                 + [gspec(d) for d in dil] * 2 + [rows(B_WIDTH), rows(C_WIDTH)]
                 + [const(wa), const(wb), const(wc), const(wo), const(gp)],
        out_specs=rows(D_MODEL),
        out_shape=jax.ShapeDtypeStruct((n, D_MODEL), F32),
        scratch_shapes=scratch,
        compiler_params=_cparams(("parallel",)),
        name="out_proj",
    )(x, z, z, z, z, z, z, *o_g, *lse_g, o_b, o_c, wa, wb, wc, wo, gp)


def _ref_cols(w, name, g=None):
    off = _REF_OFF[name]
    if g is None:
        return w[:, off:off + _WIDTHS[name]]
    return w[:, off + g * A_WIDTH:off + (g + 1) * A_WIDTH]


def _prep_w_in(w):
    w_s = jnp.concatenate([_ref_cols(w, k) for k in _ORDER_S], axis=1)
    parts = [_ref_cols(w, "ak", 0), _ref_cols(w, "av", 0), _ref_cols(w, "cu"), _ref_cols(w, "cg"),
             _ref_cols(w, "aq", 0)] + [_ref_cols(w, k) for k in ("ag", "bq", "bf", "bi", "bg", "mg")]
    w_p = jnp.concatenate(parts, axis=1)
    w_g = [jnp.concatenate([_ref_cols(w, "ak", g), _ref_cols(w, "av", g), _ref_cols(w, "aq", g)], axis=1)
           for g in range(1, A_GROUPS)]
    return w_s.astype(BF16), w_p.astype(BF16), [a.astype(BF16) for a in w_g]


def _kv_split(kv):
    return kv.reshape(kv.shape[:-1] + (2, A_HEADS, A_HEAD_DIM))


def kernel(x_prompt, x_sample, cache_kv_w128, cache_kv_w512, cache_kv_w2048, state_hgrn, state_pool,
           norm_pre, norm_post, w_in, hgrn_lb_logits, hgrn_norm, pool_w, pool_scale,
           w_br_a, w_br_b, w_br_c, w_out):
    depth = w_in.shape[0]
    b_p, s_p, _ = x_prompt.shape
    n_seq, t_new, _ = x_sample.shape
    caches = (cache_kv_w128, cache_kv_w512, cache_kv_w2048)
    past_len = cache_kv_w2048.shape[2]
    assert HGRN_ROWS % t_new == 0 and n_seq % (HGRN_ROWS // t_new) == 0
    n_p, n_s = b_p * s_p, n_seq * t_new

    hp = x_prompt.reshape(n_p, D_MODEL)
    hs = x_sample.reshape(n_s, D_MODEL)
    kv_p = [[] for _ in range(A_GROUPS)]
    kv_s = [[] for _ in range(A_GROUPS)]
    hg_p, hg_s, pl_p, pl_s = [], [], [], []
    tm_p = min(1024, s_p)
    for l in range(depth):
        w_s, w_p, w_grp = _prep_w_in(w_in[l])
        wa, wb, wc, wo = (w[l].astype(BF16) for w in (w_br_a, w_br_b, w_br_c, w_out))

        z_p, zf_p = _in_proj(hp, norm_pre[l], w_p, tm=min(2048, s_p), tn=512, z_dtype=BF16,
                             f32_cols=_P_F32)
        z3 = z_p.reshape(b_p, s_p, _P_WIDTH)
        zf3 = zf_p.reshape(b_p, s_p, _P_F32)
        o_g, lse_g = [], []
        for g, (win, d) in enumerate(A_PATTERNS):
            rows = min(win, s_p)
            if g == 0:
                arr = z_p.reshape(b_p, 1, s_p, _P_WIDTH)
                cols = (_COL_P["aq"] // A_WIDTH, _COL_P["ak"] // A_WIDTH, _COL_P["av"] // A_WIDTH)
                kv_p[g].append(_kv_split(zf3[:, s_p - rows:, 0:2 * A_WIDTH]))
            else:
                arr, f_g = _in_proj_perm(hp, norm_pre[l], w_grp[g - 1], b=b_p, s=s_p, d=d, tm=tm_p)
                cols = (2, 0, 1)
                tail = f_g[:, :, (s_p - rows) // d:, :]
                kv_p[g].append(_kv_split(tail.transpose(0, 2, 1, 3).reshape(b_p, rows, 2 * A_WIDTH)))
            o, lse = _attn_prompt(arr, cols, tq=min(512, s_p // d))
            o_g.append(o)
            lse_g.append(lse)
        o_b, s_new = _hgrn_prompt(z3, _COL_P, hgrn_lb_logits, hgrn_norm[l], l, tl=min(1024, s_p))
        hg_p.append(s_new)
        o_c = _pool_prompt(z3, _COL_P, pool_w[l], pool_scale[l], tl=min(1024, s_p))
        pl_p.append(zf3[:, s_p - C_HIST:, _COL_P["cu"]:_COL_P["cu"] + C_WIDTH])
        hp = _out_proj(hp, z_p, _COL_P, o_g, lse_g, o_b.reshape(n_p, B_WIDTH),
                       o_c.reshape(n_p, C_WIDTH), wa, wb, wc, wo, norm_post[l], tm=min(256, s_p))

        (z_s,) = _in_proj(hs, norm_pre[l], w_s, tm=min(1024, n_s), tn=1024, z_dtype=F32, f32_cols=0)
        o_g, lse_g = _attn_sample(z_s, caches, l, n_seq=n_seq, t_new=t_new)
        zs3 = z_s.reshape(n_seq, t_new, IN_WIDTH)
        for g in range(A_GROUPS):
            k = zs3[:, :, _COL_S["ak"] + g * A_WIDTH:_COL_S["ak"] + (g + 1) * A_WIDTH]
            v = zs3[:, :, _COL_S["av"] + g * A_WIDTH:_COL_S["av"] + (g + 1) * A_WIDTH]
            kv_s[g].append(_kv_split(jnp.concatenate([k, v], axis=-1)))
        o_b, s_new = _hgrn_sample(z_s, state_hgrn, hgrn_lb_logits, hgrn_norm[l], l,
                                  n_seq=n_seq, t_new=t_new)
        hg_s.append(s_new)
        o_c = _pool_sample(z_s, state_pool, pool_w[l], pool_scale[l], l, n_seq=n_seq,
                           t_new=t_new, first_pos=past_len, nb=HGRN_ROWS // t_new)
        u_new = zs3[:, :, _COL_S["cu"]:_COL_S["cu"] + C_WIDTH]
        pl_s.append(jnp.concatenate([state_pool[l], u_new], axis=1)[:, -C_HIST:])
        as4 = lambda a: a.reshape(1, 1, n_s, A_WIDTH)
        hs = _out_proj(hs, z_s, _COL_S, [as4(a) for a in o_g], [as4(a) for a in lse_g], o_b, o_c,
                       wa, wb, wc, wo, norm_post[l], tm=min(256, n_s))

    return (hp.reshape(b_p, s_p, D_MODEL), hs.reshape(n_seq, t_new, D_MODEL),
            jnp.stack(kv_p[0]), jnp.stack(kv_p[1]), jnp.stack(kv_p[2]),
            jnp.stack(kv_s[0]), jnp.stack(kv_s[1]), jnp.stack(kv_s[2]),
            jnp.stack(hg_p), jnp.stack(hg_s), jnp.stack(pl_p), jnp.stack(pl_s))
```

```python
import functools

import numpy as np
import jax
import jax.numpy as jnp
from jax import lax
from jax.experimental import pallas as pl
from jax.experimental.pallas import tpu as pltpu

F32 = jnp.float32
BF16 = jnp.bfloat16

D_MODEL = 1024
A_PATTERNS = ((128, 1), (512, 4), (2048, 16))
A_GROUPS = 3
A_HEADS = 4
A_HEAD_DIM = 128
A_WIDTH = A_HEADS * A_HEAD_DIM
N_STEPS = 128
B_HEADS = 8
B_DIM = 128
B_WIDTH = B_HEADS * B_DIM
C_WINDOWS = (2, 4, 8, 16)
C_GROUPS = 4
C_GROUP_DIM = 128
C_WIDTH = C_GROUPS * C_GROUP_DIM
C_HIST = 15
N_BRANCH = 3
EPS = 1e-6
NEG = -1e30
IN_WIDTH = 13312
LANE = 128
HGRN_ROWS = 128
HGRN_HEADS_PER_STEP = 8
HGRN_SAMPLE_HEADS_PER_STEP = 4

_REF_OFF = dict(aq=0, ak=1536, av=3072, ag=4608, bq=5120, bf=6144, bi=7168, bg=8192,
                cu=9216, cg=9728, mg=10240)
_WIDTHS = dict(aq=1536, ak=1536, av=1536, ag=512, bq=1024, bf=1024, bi=1024, bg=1024,
               cu=512, cg=512, mg=3072)

_PERM_WIDTH = 3 * A_WIDTH
_P_COL0 = (A_GROUPS - 1) * _PERM_WIDTH
_COL_P = dict(ak=0, av=512, cu=1024, cg=1536, aq=2048, ag=2560, bq=3072, bf=4096,
              bi=5120, bg=6144, mg=7168)
_P_WIDTH = IN_WIDTH - _P_COL0
_P_F32 = 2048
_COL_S = {name: _P_COL0 + off for name, off in _COL_P.items()}


def _attn_cols(g):
    if g == 0:
        return _COL_S["ak"], _COL_S["av"], _COL_S["aq"]
    base = (g - 1) * _PERM_WIDTH
    return base, base + A_WIDTH, base + 2 * A_WIDTH

_VMEM_LIMIT = 56 * 1024 * 1024


def _cparams(sem):
    return pltpu.CompilerParams(dimension_semantics=sem, vmem_limit_bytes=_VMEM_LIMIT)


def _nt(a, b):
    return lax.dot_general(a, b, (((1,), (1,)), ((), ())), preferred_element_type=F32)


def _tn(a, b):
    return lax.dot_general(a, b, (((0,), (0,)), ((), ())), preferred_element_type=F32)


def _sigmoid(x):
    return 1.0 / (1.0 + jnp.exp(-x))


def _rmsnorm_rows(x, gain):
    return x * lax.rsqrt(jnp.mean(x * x, axis=-1, keepdims=True) + EPS) * gain


def _in_proj_kernel(x_ref, g_ref, w_ref, *refs, n_f32_tiles, z_dtype):
    if n_f32_tiles:
        z_ref, zf_ref, xn_ref = refs
    else:
        z_ref, xn_ref = refs
    j = pl.program_id(1)

    @pl.when(j == 0)
    def _():
        xn_ref[...] = _rmsnorm_rows(x_ref[...], g_ref[...]).astype(BF16)

    acc = jnp.dot(xn_ref[...], w_ref[...], preferred_element_type=F32)
    z_ref[...] = acc.astype(z_dtype)
    if n_f32_tiles:
        @pl.when(j < n_f32_tiles)
        def _():
            zf_ref[...] = acc


def _in_proj(x, gain, w_bf16, *, col0, tm, tn, z_dtype, f32_cols):
    n = x.shape[0]
    width = w_bf16.shape[1] - col0
    j0 = col0 // tn
    n_f32_tiles = f32_cols // tn
    out_shape = [jax.ShapeDtypeStruct((n, width), z_dtype)]
    out_specs = [pl.BlockSpec((tm, tn), lambda i, j: (i, j))]
    if n_f32_tiles:
        out_shape.append(jax.ShapeDtypeStruct((n, f32_cols), F32))
        out_specs.append(pl.BlockSpec((tm, tn), lambda i, j: (i, jnp.minimum(j, n_f32_tiles - 1))))
    return pl.pallas_call(
        functools.partial(_in_proj_kernel, n_f32_tiles=n_f32_tiles, z_dtype=z_dtype),
        grid=(n // tm, width // tn),
        in_specs=[pl.BlockSpec((tm, D_MODEL), lambda i, j: (i, 0)),
                  pl.BlockSpec((1, D_MODEL), lambda i, j: (0, 0)),
                  pl.BlockSpec((D_MODEL, tn), lambda i, j: (0, j + j0))],
        out_specs=out_specs,
        out_shape=out_shape,
        scratch_shapes=[pltpu.VMEM((tm, D_MODEL), BF16)],
        compiler_params=_cparams(("parallel", "arbitrary")),
        name="in_proj",
    )(x, gain.reshape(1, D_MODEL), w_bf16)


def _in_proj_perm_kernel(x_ref, g_ref, w_ref, r_ref, f_ref, xs_ref, xn_ref, *, d, tm):
    xn = _rmsnorm_rows(x_ref[...], g_ref[...])
    n_chunk = D_MODEL // LANE
    rows = tm // d
    for c in range(n_chunk):
        xs_ref[c] = xn[:, c * LANE:(c + 1) * LANE]
    for r in range(d):
        for c in range(n_chunk):
            xn_ref[r * rows:(r + 1) * rows, c * LANE:(c + 1) * LANE] = (
                xs_ref[c, pl.ds(r, rows, stride=d), :].astype(BF16))
    acc = jnp.dot(xn_ref[...], w_ref[...], preferred_element_type=F32)
    r_ref[0] = acc.astype(BF16).reshape(d, rows, _PERM_WIDTH)
    f_ref[0] = acc[:, 0:2 * A_WIDTH].reshape(d, rows, 2 * A_WIDTH)


def _in_proj_perm(x, gain, w_bf16, g, *, b, s, d, tm):
    tpb = s // tm
    rows = tm // d
    idx = lambda i: (i // tpb, 0, i % tpb, 0)
    return pl.pallas_call(
        functools.partial(_in_proj_perm_kernel, d=d, tm=tm),
        grid=(b * tpb,),
        in_specs=[pl.BlockSpec((tm, D_MODEL), lambda i: (i, 0)),
                  pl.BlockSpec((1, D_MODEL), lambda i: (0, 0)),
                  pl.BlockSpec((D_MODEL, _PERM_WIDTH), lambda i: (0, g - 1))],
        out_specs=[pl.BlockSpec((1, d, rows, _PERM_WIDTH), idx),
                   pl.BlockSpec((1, d, rows, 2 * A_WIDTH), idx)],
        out_shape=[jax.ShapeDtypeStruct((b, d, s // d, _PERM_WIDTH), BF16),
                   jax.ShapeDtypeStruct((b, d, s // d, 2 * A_WIDTH), F32)],
        scratch_shapes=[pltpu.VMEM((D_MODEL // LANE, tm, LANE), F32),
                        pltpu.VMEM((tm, D_MODEL), BF16)],
        compiler_params=_cparams(("parallel",)),
        name="in_proj_d%d" % d,
    )(x, gain.reshape(1, D_MODEL), w_bf16)


def _attn_prompt_kernel(q_ref, kp_ref, kc_ref, vp_ref, vc_ref, o_ref, lse_ref, *, tq):
    i = pl.program_id(2)
    n = N_STEPS
    qi = lax.broadcasted_iota(jnp.int32, (n, 2 * n), 0)
    kj = lax.broadcasted_iota(jnp.int32, (n, 2 * n), 1)
    band = (kj >= qi) & (kj <= qi + n)
    band_first = band & ((i > 0) | (kj >= n))
    scale = A_HEAD_DIM ** -0.5
    for h in range(A_HEADS):
        hs = slice(h * A_HEAD_DIM, (h + 1) * A_HEAD_DIM)
        for j in range(tq // n):
            rows = slice(j * n, (j + 1) * n)
            q = q_ref[0, 0, rows, hs]
            if j == 0:
                k = jnp.concatenate([kp_ref[0, 0, :, hs], kc_ref[0, 0, rows, hs]], axis=0)
                v = jnp.concatenate([vp_ref[0, 0, :, hs], vc_ref[0, 0, rows, hs]], axis=0)
                valid = band_first
            else:
                k = kc_ref[0, 0, (j - 1) * n:(j + 1) * n, hs]
                v = vc_ref[0, 0, (j - 1) * n:(j + 1) * n, hs]
                valid = band
            s = jnp.where(valid, _nt(q, k) * scale, NEG)
            m = jnp.max(s, axis=-1, keepdims=True)
            p = jnp.exp(s - m)
            l = jnp.sum(p, axis=-1, keepdims=True)
            o = jnp.dot(p.astype(BF16), v, preferred_element_type=F32)
            o_ref[0, 0, rows, hs] = o / l
            lse_ref[0, 0, rows, hs] = jnp.broadcast_to(m + jnp.log(l), (n, A_HEAD_DIM))


def _attn_prompt(arr, cols, *, tq):
    b, d, rows, _ = arr.shape
    qc, kc, vc = cols
    prev_per_blk = tq // N_STEPS

    def cur(c):
        return pl.BlockSpec((1, 1, tq, A_WIDTH), lambda bb, r, i: (bb, r, i, c))

    def prev(c):
        return pl.BlockSpec((1, 1, N_STEPS, A_WIDTH),
                            lambda bb, r, i: (bb, r, jnp.maximum(i * prev_per_blk - 1, 0), c))

    out_spec = pl.BlockSpec((1, 1, tq, A_WIDTH), lambda bb, r, i: (bb, r, i, 0))
    return pl.pallas_call(
        functools.partial(_attn_prompt_kernel, tq=tq),
        grid=(b, d, rows // tq),
        in_specs=[cur(qc), prev(kc), cur(kc), prev(vc), cur(vc)],
        out_specs=[out_spec, out_spec],
        out_shape=[jax.ShapeDtypeStruct((b, d, rows, A_WIDTH), F32)] * 2,
        compiler_params=_cparams(("parallel", "parallel", "arbitrary")),
        name="attn_prompt_d%d" % d,
    )(arr, arr, arr, arr, arr)


def _gather_pitch(n_j):
    p = -(-n_j // 8)
    return 8 * (p if p % 2 else p + 1)


def _attn_sample_kernel(z_ref, c0_ref, c1_ref, c2_ref, *refs, t_new):
    out_refs, flat_refs = refs[:2 * A_GROUPS], refs[2 * A_GROUPS:]
    scale = A_HEAD_DIM ** -0.5
    nrow = A_HEADS * t_new
    per_res = 2 * A_HEADS
    for g, c_ref in enumerate((c0_ref, c1_ref, c2_ref)):
        d = A_PATTERNS[g][1]
        n_res = min(d, t_new)
        o_ref, lse_ref = out_refs[2 * g], out_refs[2 * g + 1]
        k_off, v_off, q_off = _attn_cols(g)
        q = z_ref[:, q_off:q_off + A_WIDTH]
        kn = z_ref[:, k_off:k_off + A_WIDTH]
        vn = z_ref[:, v_off:v_off + A_WIDTH]
        n_j = n_res * per_res
        pitch = _gather_pitch(n_j)
        flat_ref = flat_refs[g]
        for mm in range(N_STEPS):
            flat_ref[mm * pitch:mm * pitch + n_j, :] = c_ref[mm]
        gather = lambda j: flat_ref[pl.ds(j, N_STEPS, stride=pitch), :]
        kcat = jnp.concatenate([gather(r * per_res + h)
                                for r in range(n_res) for h in range(A_HEADS)], axis=1).astype(BF16)
        vcat = jnp.concatenate([gather(r * per_res + A_HEADS + h)
                                for r in range(n_res) for h in range(A_HEADS)], axis=1).astype(BF16)
        wide = n_res * A_WIDTH
        q4 = jnp.concatenate([q] * A_HEADS, axis=0)
        row = lax.broadcasted_iota(jnp.int32, (nrow, wide), 0)
        col = lax.broadcasted_iota(jnp.int32, (nrow, wide), 1)
        row_h, row_t = row // t_new, row % t_new
        sel = ((col // A_WIDTH) == (row_t % d)) & (((col % A_WIDTH) // A_HEAD_DIM) == row_h)
        qbd = jnp.where(sel, jnp.concatenate([q4] * n_res, axis=1), 0.0).astype(BF16)
        s_c = _nt(qbd, kcat) * scale
        r2 = lax.broadcasted_iota(jnp.int32, (nrow, N_STEPS), 0) % t_new
        c2 = lax.broadcasted_iota(jnp.int32, (nrow, N_STEPS), 1)
        s_c = jnp.where(c2 >= r2 // d, s_c, NEG)
        pad = jnp.zeros((N_STEPS - t_new, A_WIDTH), F32)
        knp = jnp.concatenate([kn, pad], axis=0).astype(BF16)
        vnp = jnp.concatenate([vn, pad], axis=0).astype(BF16)
        rowq = lax.broadcasted_iota(jnp.int32, (nrow, A_WIDTH), 0) // t_new
        colq = lax.broadcasted_iota(jnp.int32, (nrow, A_WIDTH), 1) // A_HEAD_DIM
        selq = rowq == colq
        qbd2 = jnp.where(selq, q4, 0.0).astype(BF16)
        s_n = _nt(qbd2, knp) * scale
        s_n = jnp.where((c2 <= r2) & ((r2 - c2) % d == 0), s_n, NEG)
        m = jnp.maximum(jnp.max(s_c, axis=-1, keepdims=True), jnp.max(s_n, axis=-1, keepdims=True))
        p_c = jnp.exp(s_c - m)
        p_n = jnp.exp(s_n - m)
        l = jnp.sum(p_c, axis=-1, keepdims=True) + jnp.sum(p_n, axis=-1, keepdims=True)
        o_wide = jnp.where(sel, jnp.dot(p_c.astype(BF16), vcat, preferred_element_type=F32), 0.0)
        o_new = jnp.where(selq, jnp.dot(p_n.astype(BF16), vnp, preferred_element_type=F32), 0.0)
        o = o_new[:, 0:A_HEAD_DIM]
        for c in range(1, A_HEADS):
            o = o + o_new[:, c * A_HEAD_DIM:(c + 1) * A_HEAD_DIM]
        for c in range(wide // A_HEAD_DIM):
            o = o + o_wide[:, c * A_HEAD_DIM:(c + 1) * A_HEAD_DIM]
        o = o / l
        lse = jnp.broadcast_to(m + jnp.log(l), (nrow, A_HEAD_DIM))
        for h in range(A_HEADS):
            hs = slice(h * A_HEAD_DIM, (h + 1) * A_HEAD_DIM)
            o_ref[:, hs] = o[h * t_new:(h + 1) * t_new]
            lse_ref[:, hs] = lse[h * t_new:(h + 1) * t_new]


def _attn_sample(z_s, caches, layer, *, n_seq, t_new):
    in_specs = [pl.BlockSpec((t_new, IN_WIDTH), lambda b: (b, 0))]
    views, scratch = [], []
    per_res = 2 * A_HEADS
    for g, c in enumerate(caches):
        win, d = A_PATTERNS[g]
        assert c.shape[2] == win == N_STEPS * d, "cache must hold one full window"
        n_res = min(d, t_new)
        views.append(c.reshape(c.shape[0], n_seq, N_STEPS, d * per_res, A_HEAD_DIM))
        in_specs.append(pl.BlockSpec((None, None, N_STEPS, n_res * per_res, A_HEAD_DIM),
                                     lambda b, layer=layer: (layer, b, 0, 0, 0)))
        scratch.append(pltpu.VMEM((N_STEPS * _gather_pitch(n_res * per_res), A_HEAD_DIM), F32))
    out_spec = pl.BlockSpec((t_new, A_WIDTH), lambda b: (b, 0))
    outs = pl.pallas_call(
        functools.partial(_attn_sample_kernel, t_new=t_new),
        grid=(n_seq,),
        in_specs=in_specs,
        out_specs=[out_spec] * (2 * A_GROUPS),
        out_shape=[jax.ShapeDtypeStruct((n_seq * t_new, A_WIDTH), F32)] * (2 * A_GROUPS),
        scratch_shapes=scratch,
        compiler_params=_cparams(("parallel",)),
        name="attn_sample",
    )(z_s, *views)
    return outs[0::2], outs[1::2]


def _hgrn_consts(seg):
    n = HGRN_ROWS
    t = np.arange(n)[:, None]
    u = np.arange(n)[None, :]
    seg_lo = (t // seg) * seg
    seg_hi = seg_lo + seg - 1
    mats = [(u >= seg_lo) & (u <= t),
            (u > t) & (u <= seg_hi)]
    level = np.full((n, n), -1, np.int32)
    s = u
    li = 0
    m = seg
    while m >= 2:
        half = m // 2
        upper = (t % m) >= half
        ref = (t // m) * m + half - 1
        if m > 2:
            mats.append(np.where(upper, (u > ref) & (u <= t), (u > t) & (u <= ref)))
        pair = (t // m == s // m) & ((t % m) >= half) & ((s % m) < half)
        level[pair] = li
        li += 1
        m //= 2
    level[np.arange(n), np.arange(n)] = li
    w = np.concatenate(mats, axis=0).astype(np.float32)
    w2 = np.concatenate([w, w], axis=1)
    return jnp.asarray(w2, BF16), jnp.asarray(level), li


def _hgrn_blocks(items, w2, level, n_lev, seg):
    n = HGRN_ROWS
    n_seg = n // seg
    row = lax.broadcasted_iota(jnp.int32, (n, B_DIM), 0)

    stage1 = []
    for q, xf, v, lb, states in items:
        f = lb + (1.0 - lb) * _sigmoid(xf)
        g = jnp.log(f)
        g_hi = g.astype(BF16)
        g_lo = (g - g_hi.astype(F32)).astype(BF16)
        dsum = jnp.dot(w2, jnp.concatenate([g_hi, g_lo], axis=0), preferred_element_type=F32)
        stage1.append((f, 1.0 - f, dsum))

    stage2 = []
    for (q, xf, v, lb, states), (f, kin, dsum) in zip(items, stage1):
        fac = jnp.exp(dsum)
        ea = fac[0:n]
        ops = []
        m = seg
        for li in range(n_lev):
            upper = (row & (m // 2)) != 0
            if m > 2:
                tm = jnp.where(upper, q, kin) * fac[(2 + li) * n:(3 + li) * n]
            else:
                tm = jnp.where(upper, q * f, kin)
            ops.append(tm.astype(BF16))
            m //= 2
        stage2.append((ea, (q * ea).astype(BF16), kin * fac[n:2 * n], ops))

    stage3 = []
    for (q, xf, v, lb, states), (f, kin, dsum), (ea, qt, kh, ops) in zip(items, stage1, stage2):
        att = jnp.zeros((n, n), F32)
        for li, tm in enumerate(ops):
            att = jnp.where(level == li, _nt(tm, tm), att)
        att = jnp.where(level == n_lev, _nt(q.astype(BF16), kin.astype(BF16)), att)
        stage3.append(att.astype(BF16))

    results = []
    for (q, xf, v, lb, states), (ea, qt, kh, ops), att in zip(items, stage2, stage3):
        vb = v.astype(BF16)
        o = jnp.dot(att, vb, preferred_element_type=F32)
        new_states = []
        for si in range(n_seg):
            st = states[si]
            o_s = _nt(qt, st.astype(BF16))
            if n_seg == 1:
                o = o + o_s
                kh_s = kh
            else:
                in_seg = (row >= si * seg) & (row < (si + 1) * seg)
                o = o + jnp.where(in_seg, o_s, 0.0)
                kh_s = jnp.where(in_seg, kh, 0.0)
            decay = ea[(si + 1) * seg - 1:(si + 1) * seg, :]
            new_states.append(st * decay + _tn(vb, kh_s.astype(BF16)))
        results.append((o, new_states))
    return results


def _lower_bound(lbl_ref, layer):
    lg = lbl_ref[...]
    e = jnp.exp(lg - jnp.max(lg, axis=0, keepdims=True))
    p = e / jnp.sum(e, axis=0, keepdims=True)
    return jnp.sum(p[0:layer + 1], axis=0, keepdims=True) - p[0:1]


def _hgrn_prompt_kernel(q_ref, f_ref, v_ref, lbl_ref, gn_ref, w3_ref, lev_ref, o_ref, s_ref,
                        st_ref, *, layer, n_lev, tl, n_heads):
    li = pl.program_id(2)

    @pl.when(li == 0)
    def _():
        st_ref[...] = jnp.zeros_like(st_ref)

    lb = _lower_bound(lbl_ref, layer)
    w3 = w3_ref[...]
    level = lev_ref[...]
    gain = gn_ref[...]
    head = [slice(hh * B_DIM, (hh + 1) * B_DIM) for hh in range(n_heads)]

    def body(c, carry):
        r0 = pl.multiple_of(c * HGRN_ROWS, HGRN_ROWS)
        rows = pl.ds(r0, HGRN_ROWS)
        items = [(q_ref[0, rows, hs].astype(F32), f_ref[0, rows, hs].astype(F32),
                  v_ref[0, rows, hs].astype(F32), lb[:, hs], [st_ref[hh]])
                 for hh, hs in enumerate(head)]
        for hh, (o, (st,)) in enumerate(_hgrn_blocks(items, w3, level, n_lev, HGRN_ROWS)):
            st_ref[hh] = st
            o_ref[0, rows, head[hh]] = _rmsnorm_rows(o, gain)
        return carry

    lax.fori_loop(0, tl // HGRN_ROWS, body, 0)

    @pl.when(li == pl.num_programs(2) - 1)
    def _():
        for hh in range(n_heads):
            s_ref[0, hh] = st_ref[hh].T


def _hgrn_prompt(z3, col, lb_logits, hgrn_gain, layer, *, tl):
    b, s, _ = z3.shape
    w3, level, n_lev = _hgrn_consts(HGRN_ROWS)
    nh = HGRN_HEADS_PER_STEP
    wblk = nh * B_DIM

    def cspec(name):
        c0 = col[name] // wblk
        return pl.BlockSpec((1, tl, wblk), lambda bb, h, i: (bb, i, c0 + h))

    const = lambda shape: pl.BlockSpec(shape, lambda bb, h, i: (0,) * len(shape))
    return pl.pallas_call(
        functools.partial(_hgrn_prompt_kernel, layer=layer, n_lev=n_lev, tl=tl, n_heads=nh),
        grid=(b, B_HEADS // nh, s // tl),
        in_specs=[cspec("bq"), cspec("bf"), cspec("bi"),
                  pl.BlockSpec((lb_logits.shape[0], wblk), lambda bb, h, i: (0, h)),
                  const((1, B_DIM)), const(w3.shape), const(level.shape)],
        out_specs=[pl.BlockSpec((1, tl, wblk), lambda bb, h, i: (bb, i, h)),
                   pl.BlockSpec((1, nh, B_DIM, B_DIM), lambda bb, h, i: (bb, h, 0, 0))],
        out_shape=[jax.ShapeDtypeStruct((b, s, B_WIDTH), F32),
                   jax.ShapeDtypeStruct((b, B_HEADS, B_DIM, B_DIM), F32)],
        scratch_shapes=[pltpu.VMEM((nh, B_DIM, B_DIM), F32)],
        compiler_params=_cparams(("parallel", "parallel", "arbitrary")),
        name="hgrn_prompt",
    )(z3, z3, z3, lb_logits, hgrn_gain.reshape(1, B_DIM), w3, level)


def _hgrn_sample_kernel(q_ref, f_ref, v_ref, s0_ref, lbl_ref, gn_ref, w3_ref, lev_ref, acc_ref,
                        o_ref, s_ref, *, layer, n_lev, t_new, n_heads):
    del acc_ref
    lb = _lower_bound(lbl_ref, layer)
    n_seg = HGRN_ROWS // t_new
    head = [slice(hh * B_DIM, (hh + 1) * B_DIM) for hh in range(n_heads)]
    items = [(q_ref[:, hs], f_ref[:, hs], v_ref[:, hs], lb[:, hs],
              [s0_ref[0, si, hh].T for si in range(n_seg)])
             for hh, hs in enumerate(head)]
    results = _hgrn_blocks(items, w3_ref[...], lev_ref[...], n_lev, t_new)
    for hh, (o, new_states) in enumerate(results):
        o_ref[:, head[hh]] = _rmsnorm_rows(o, gn_ref[...])
        for si in range(n_seg):
            s_ref[0, si, hh] = new_states[si].T


def _hgrn_sample(z_s, state, new_state, lb_logits, hgrn_gain, layer, *, n_seq, t_new):
    w3, level, n_lev = _hgrn_consts(t_new)
    n_seg = HGRN_ROWS // t_new
    nh = HGRN_SAMPLE_HEADS_PER_STEP
    wblk = nh * B_DIM

    def cspec(name):
        c0 = _COL_S[name] // wblk
        return pl.BlockSpec((HGRN_ROWS, wblk), lambda i, h: (i, c0 + h))

    const = lambda shape: pl.BlockSpec(shape, lambda i, h: (0,) * len(shape))
    return pl.pallas_call(
        functools.partial(_hgrn_sample_kernel, layer=layer, n_lev=n_lev, t_new=t_new, n_heads=nh),
        grid=(n_seq // n_seg, B_HEADS // nh),
        in_specs=[cspec("bq"), cspec("bf"), cspec("bi"),
                  pl.BlockSpec((1, n_seg, nh, B_DIM, B_DIM), lambda i, h, layer=layer: (layer, i, h, 0, 0)),
                  pl.BlockSpec((lb_logits.shape[0], wblk), lambda i, h: (0, h)),
                  const((1, B_DIM)), const(w3.shape), const(level.shape),
                  pl.BlockSpec(memory_space=pl.ANY)],
        out_specs=[pl.BlockSpec((HGRN_ROWS, wblk), lambda i, h: (i, h)),
                   pl.BlockSpec((1, n_seg, nh, B_DIM, B_DIM), lambda i, h, layer=layer: (layer, i, h, 0, 0))],
        out_shape=[jax.ShapeDtypeStruct((n_seq * t_new, B_WIDTH), F32),
                   jax.ShapeDtypeStruct(new_state.shape, F32)],
        input_output_aliases={8: 1},
        compiler_params=_cparams(("parallel", "parallel")),
        name="hgrn_sample",
    )(z_s, z_s, z_s, state, lb_logits, hgrn_gain.reshape(1, B_DIM), w3, level, new_state)


_POOL_PAD = 16


def _pool_mix(ext_at, u, pos, pw_ref, ps_ref):
    outs = []
    for g, win in enumerate(C_WINDOWS):
        gs = slice(g * C_GROUP_DIM, (g + 1) * C_GROUP_DIM)
        acc = u[:, gs]
        for k in range(1, win):
            acc = acc + ext_at(k, gs)
        cnt = jnp.minimum(pos + 1, win).astype(F32)
        pooled = acc * (1.0 / cnt) - u[:, gs]
        mixed = jnp.dot(pooled.astype(BF16), pw_ref[g].astype(BF16), preferred_element_type=F32)
        outs.append(mixed * ps_ref[:, gs])
    return outs


def _pool_prompt_kernel(u_ref, h_ref, pw_ref, ps_ref, o_ref, ext_ref, *, tl):
    i = pl.program_id(1)
    halo = h_ref[0].astype(F32)
    ext_ref[0:_POOL_PAD, :] = jnp.where(i > 0, halo, 0.0)
    u = u_ref[0].astype(F32)
    ext_ref[_POOL_PAD:, :] = u
    pos = i * tl + lax.broadcasted_iota(jnp.int32, (tl, 1), 0)
    outs = _pool_mix(lambda k, gs: ext_ref[_POOL_PAD - k:_POOL_PAD - k + tl, gs], u, pos, pw_ref, ps_ref)
    for g, og in enumerate(outs):
        o_ref[0, :, g * C_GROUP_DIM:(g + 1) * C_GROUP_DIM] = og


def _pool_prompt(z3, col, pool_w, pool_scale, *, tl):
    b, s, _ = z3.shape
    c0 = col["cu"] // C_WIDTH
    per = tl // _POOL_PAD
    return pl.pallas_call(
        functools.partial(_pool_prompt_kernel, tl=tl),
        grid=(b, s // tl),
        in_specs=[pl.BlockSpec((1, tl, C_WIDTH), lambda bb, i: (bb, i, c0)),
                  pl.BlockSpec((1, _POOL_PAD, C_WIDTH), lambda bb, i: (bb, jnp.maximum(i * per - 1, 0), c0)),
                  pl.BlockSpec((C_GROUPS, C_GROUP_DIM, C_GROUP_DIM), lambda bb, i: (0, 0, 0)),
                  pl.BlockSpec((1, C_WIDTH), lambda bb, i: (0, 0))],
        out_specs=pl.BlockSpec((1, tl, C_WIDTH), lambda bb, i: (bb, i, 0)),
        out_shape=jax.ShapeDtypeStruct((b, s, C_WIDTH), F32),
        scratch_shapes=[pltpu.VMEM((_POOL_PAD + tl, C_WIDTH), F32)],
        compiler_params=_cparams(("parallel", "arbitrary")),
        name="pool_prompt",
    )(z3, z3, pool_w, pool_scale.reshape(1, C_WIDTH))


def _pool_sample_kernel(u_ref, h_ref, pw_ref, ps_ref, o_ref, ext_ref, *, nb, t_new, first_pos):
    ext_ref[:, 0:1, :] = jnp.zeros((nb, 1, C_WIDTH), F32)
    ext_ref[:, 1:_POOL_PAD, :] = h_ref[0]
    u3 = u_ref[...].reshape(nb, t_new, C_WIDTH)
    ext_ref[:, _POOL_PAD:, :] = u3
    u = u_ref[...]
    pos = first_pos + lax.broadcasted_iota(jnp.int32, (nb * t_new, 1), 0) % t_new

    def ext_at(k, gs):
        return ext_ref[:, _POOL_PAD - k:_POOL_PAD - k + t_new, gs].reshape(nb * t_new, C_GROUP_DIM)

    outs = _pool_mix(ext_at, u, pos, pw_ref, ps_ref)
    for g, og in enumerate(outs):
        o_ref[:, g * C_GROUP_DIM:(g + 1) * C_GROUP_DIM] = og


def _pool_sample(z_s, state_pool, pool_w, pool_scale, layer, *, n_seq, t_new, first_pos, nb):
    c0 = _COL_S["cu"] // C_WIDTH
    return pl.pallas_call(
        functools.partial(_pool_sample_kernel, nb=nb, t_new=t_new, first_pos=first_pos),
        grid=(n_seq // nb,),
        in_specs=[pl.BlockSpec((nb * t_new, C_WIDTH), lambda i: (i, c0)),
                  pl.BlockSpec((1, nb, C_HIST, C_WIDTH), lambda i, layer=layer: (layer, i, 0, 0)),
                  pl.BlockSpec((C_GROUPS, C_GROUP_DIM, C_GROUP_DIM), lambda i: (0, 0, 0)),
                  pl.BlockSpec((1, C_WIDTH), lambda i: (0, 0))],
        out_specs=pl.BlockSpec((nb * t_new, C_WIDTH), lambda i: (i, 0)),
        out_shape=jax.ShapeDtypeStruct((n_seq * t_new, C_WIDTH), F32),
        scratch_shapes=[pltpu.VMEM((nb, _POOL_PAD + t_new, C_WIDTH), F32)],
        compiler_params=_cparams(("parallel",)),
        name="pool_sample",
    )(z_s, state_pool, pool_w, pool_scale.reshape(1, C_WIDTH))


def _token_order(ref, scr_ref, d, tm):
    if d == 1:
        return ref[0, 0]
    n_chunk = A_WIDTH // LANE
    for r in range(d):
        for c in range(n_chunk):
            scr_ref[c, pl.ds(r, tm // d, stride=d), :] = ref[0, r, :, c * LANE:(c + 1) * LANE]
    return jnp.concatenate([scr_ref[c] for c in range(n_chunk)], axis=1)


def _out_proj_kernel(x_ref, ag_ref, bg_ref, cg_ref, m0_ref, m1_ref, m2_ref, o0_ref, o1_ref, o2_ref,
                     l0_ref, l1_ref, l2_ref, ob_ref, oc_ref, wa_ref, wb_ref, wc_ref, wo_ref,
                     gp_ref, y_ref, *scratch, dil, tm):
    scr = iter(scratch)
    grp = []
    for o_ref, l_ref, d in zip((o0_ref, o1_ref, o2_ref), (l0_ref, l1_ref, l2_ref), dil):
        so = next(scr) if d > 1 else None
        sl = next(scr) if d > 1 else None
        grp.append((_token_order(o_ref, so, d, tm), _token_order(l_ref, sl, d, tm)))
    (o0, l0), (o1, l1), (o2, l2) = grp
    mx = jnp.maximum(jnp.maximum(l0, l1), l2)
    e0, e1, e2 = jnp.exp(l0 - mx), jnp.exp(l1 - mx), jnp.exp(l2 - mx)
    o_a = (e0 * o0 + e1 * o1 + e2 * o2) / (e0 + e1 + e2)
    def silu_gated(o, g_ref):
        h = g_ref[...].astype(F32) * 0.5
        return (o * (h + h * jnp.tanh(h))).astype(BF16)

    def merge_gated(m_ref, y, w_ref):
        p = jnp.dot(y, w_ref[...], preferred_element_type=F32)
        return p + jnp.tanh(m_ref[...].astype(F32) * 0.5) * p

    merged = (merge_gated(m0_ref, silu_gated(o_a, ag_ref), wa_ref)
              + merge_gated(m1_ref, silu_gated(ob_ref[...], bg_ref), wb_ref)
              + merge_gated(m2_ref, silu_gated(oc_ref[...], cg_ref), wc_ref)) * 0.5
    out = jnp.dot(merged.astype(BF16), wo_ref[...], preferred_element_type=F32)
    y_ref[...] = x_ref[...] + _rmsnorm_rows(out, gp_ref[...])


def _out_proj(x, z, col, o_g, lse_g, o_b, o_c, wa, wb, wc, wo, gain_post, *, tm):
    n = x.shape[0]
    dil = tuple(int(a.shape[1]) for a in o_g)
    tpb = (n // o_g[0].shape[0]) // tm

    def zcol(name, width, k=0):
        assert col[name] % width == 0
        c0 = col[name] // width + k
        return pl.BlockSpec((tm, width), lambda i: (i, c0))

    def gspec(d):
        return pl.BlockSpec((1, d, tm // d, A_WIDTH), lambda i: (i // tpb, 0, i % tpb, 0))

    rows = lambda width: pl.BlockSpec((tm, width), lambda i: (i, 0))
    const = lambda a: pl.BlockSpec(a.shape, lambda i: (0,) * a.ndim)
    gp = gain_post.reshape(1, D_MODEL)
    scratch = []
    for d in dil:
        if d > 1:
            scratch += [pltpu.VMEM((A_WIDTH // LANE, tm, LANE), F32)] * 2
    return pl.pallas_call(
        functools.partial(_out_proj_kernel, dil=dil, tm=tm),
        grid=(n // tm,),
        in_specs=[rows(D_MODEL), zcol("ag", A_WIDTH), zcol("bg", B_WIDTH), zcol("cg", C_WIDTH),
                  zcol("mg", D_MODEL, 0), zcol("mg", D_MODEL, 1), zcol("mg", D_MODEL, 2)]
                 + [gspec(d) for d in dil] * 2 + [rows(B_WIDTH), rows(C_WIDTH)]
                 + [const(wa), const(wb), const(wc), const(wo), const(gp)],
        out_specs=rows(D_MODEL),
        out_shape=jax.ShapeDtypeStruct((n, D_MODEL), F32),
        scratch_shapes=scratch,
        compiler_params=_cparams(("parallel",)),
        name="out_proj",
    )(x, z, z, z, z, z, z, *o_g, *lse_g, o_b, o_c, wa, wb, wc, wo, gp)


def _ref_cols(w, name, g=None):
    off = _REF_OFF[name]
    if g is None:
        return w[:, off:off + _WIDTHS[name]]
    return w[:, off + g * A_WIDTH:off + (g + 1) * A_WIDTH]


def _prep_w_in(w):
    parts = []
    for g in range(1, A_GROUPS):
        parts += [_ref_cols(w, "ak", g), _ref_cols(w, "av", g), _ref_cols(w, "aq", g)]
    parts += [_ref_cols(w, "ak", 0), _ref_cols(w, "av", 0), _ref_cols(w, "cu"), _ref_cols(w, "cg"),
              _ref_cols(w, "aq", 0)] + [_ref_cols(w, k) for k in ("ag", "bq", "bf", "bi", "bg", "mg")]
    return jnp.concatenate(parts, axis=1).astype(BF16)


def _kv_split(kv):
    return kv.reshape(kv.shape[:-1] + (2, A_HEADS, A_HEAD_DIM))


def kernel(x_prompt, x_sample, cache_kv_w128, cache_kv_w512, cache_kv_w2048, state_hgrn, state_pool,
           norm_pre, norm_post, w_in, hgrn_lb_logits, hgrn_norm, pool_w, pool_scale,
           w_br_a, w_br_b, w_br_c, w_out):
    depth = w_in.shape[0]
    b_p, s_p, _ = x_prompt.shape
    n_seq, t_new, _ = x_sample.shape
    caches = (cache_kv_w128, cache_kv_w512, cache_kv_w2048)
    past_len = cache_kv_w2048.shape[2]
    assert HGRN_ROWS % t_new == 0 and n_seq % (HGRN_ROWS // t_new) == 0
    n_p, n_s = b_p * s_p, n_seq * t_new

    hp = x_prompt.reshape(n_p, D_MODEL)
    hs = x_sample.reshape(n_s, D_MODEL)
    kv_p = [[] for _ in range(A_GROUPS)]
    kv_s = [[] for _ in range(A_GROUPS)]
    hg_p, pl_p, pl_s = [], [], []
    hg_s = jnp.zeros(state_hgrn.shape, F32)
    tm_p = min(1024, s_p)
    for l in range(depth):
        w_l = _prep_w_in(w_in[l])
        wa, wb, wc, wo = (w[l].astype(BF16) for w in (w_br_a, w_br_b, w_br_c, w_out))

        z_p, zf_p = _in_proj(hp, norm_pre[l], w_l, col0=_P_COL0, tm=min(2048, s_p), tn=512,
                             z_dtype=BF16, f32_cols=_P_F32)
        z3 = z_p.reshape(b_p, s_p, _P_WIDTH)
        zf3 = zf_p.reshape(b_p, s_p, _P_F32)
        o_g, lse_g = [], []
        for g, (win, d) in enumerate(A_PATTERNS):
            rows = min(win, s_p)
            if g == 0:
                arr = z_p.reshape(b_p, 1, s_p, _P_WIDTH)
                cols = (_COL_P["aq"] // A_WIDTH, _COL_P["ak"] // A_WIDTH, _COL_P["av"] // A_WIDTH)
                kv_p[g].append(_kv_split(zf3[:, s_p - rows:, 0:2 * A_WIDTH]))
            else:
                arr, f_g = _in_proj_perm(hp, norm_pre[l], w_l, g, b=b_p, s=s_p, d=d, tm=tm_p)
                cols = (2, 0, 1)
                tail = f_g[:, :, (s_p - rows) // d:, :]
                kv_p[g].append(_kv_split(tail.transpose(0, 2, 1, 3).reshape(b_p, rows, 2 * A_WIDTH)))
            o, lse = _attn_prompt(arr, cols, tq=min(512, s_p // d))
            o_g.append(o)
            lse_g.append(lse)
        o_b, s_new = _hgrn_prompt(z3, _COL_P, hgrn_lb_logits, hgrn_norm[l], l, tl=min(1024, s_p))
        hg_p.append(s_new)
        o_c = _pool_prompt(z3, _COL_P, pool_w[l], pool_scale[l], tl=min(1024, s_p))
        pl_p.append(zf3[:, s_p - C_HIST:, _COL_P["cu"]:_COL_P["cu"] + C_WIDTH])
        hp = _out_proj(hp, z_p, _COL_P, o_g, lse_g, o_b.reshape(n_p, B_WIDTH),
                       o_c.reshape(n_p, C_WIDTH), wa, wb, wc, wo, norm_post[l], tm=min(512, s_p))

        (z_s,) = _in_proj(hs, norm_pre[l], w_l, col0=0, tm=min(1024, n_s), tn=1024, z_dtype=F32,
                          f32_cols=0)
        o_g, lse_g = _attn_sample(z_s, caches, l, n_seq=n_seq, t_new=t_new)
        zs3 = z_s.reshape(n_seq, t_new, IN_WIDTH)
        for g in range(A_GROUPS):
            k_off, v_off, _ = _attn_cols(g)
            kv_s[g].append(_kv_split(zs3[:, :, k_off:v_off + A_WIDTH]))
        o_b, hg_s = _hgrn_sample(z_s, state_hgrn, hg_s, hgrn_lb_logits, hgrn_norm[l], l,
                                 n_seq=n_seq, t_new=t_new)
        o_c = _pool_sample(z_s, state_pool, pool_w[l], pool_scale[l], l, n_seq=n_seq,
                           t_new=t_new, first_pos=past_len, nb=HGRN_ROWS // t_new)
        u_new = zs3[:, :, _COL_S["cu"]:_COL_S["cu"] + C_WIDTH]
        pl_s.append(jnp.concatenate([state_pool[l], u_new], axis=1)[:, -C_HIST:])
        as4 = lambda a: a.reshape(1, 1, n_s, A_WIDTH)
        hs = _out_proj(hs, z_s, _COL_S, [as4(a) for a in o_g], [as4(a) for a in lse_g], o_b, o_c,
                       wa, wb, wc, wo, norm_post[l], tm=min(256, n_s))

    return (hp.reshape(b_p, s_p, D_MODEL), hs.reshape(n_seq, t_new, D_MODEL),
            jnp.stack(kv_p[0]), jnp.stack(kv_p[1]), jnp.stack(kv_p[2]),
            jnp.stack(kv_s[0]), jnp.stack(kv_s[1]), jnp.stack(kv_s[2]),
            jnp.stack(hg_p), hg_s, jnp.stack(pl_p), jnp.stack(pl_s))
```

```python
import functools

import numpy as np
import jax
import jax.numpy as jnp
from jax import lax
from jax.experimental import pallas as pl
from jax.experimental.pallas import tpu as pltpu

F32 = jnp.float32
BF16 = jnp.bfloat16

D_MODEL = 1024
A_PATTERNS = ((128, 1), (512, 4), (2048, 16))
A_GROUPS = 3
A_HEADS = 4
A_HEAD_DIM = 128
A_WIDTH = A_HEADS * A_HEAD_DIM
N_STEPS = 128
B_HEADS = 8
B_DIM = 128
B_WIDTH = B_HEADS * B_DIM
C_WINDOWS = (2, 4, 8, 16)
C_GROUPS = 4
C_GROUP_DIM = 128
C_WIDTH = C_GROUPS * C_GROUP_DIM
C_HIST = 15
N_BRANCH = 3
EPS = 1e-6
NEG = -1e30
IN_WIDTH = 13312
LANE = 128
HGRN_ROWS = 128
HGRN_HEADS_PER_STEP = 8
HGRN_SAMPLE_HEADS_PER_STEP = 4

_REF_OFF = dict(aq=0, ak=1536, av=3072, ag=4608, bq=5120, bf=6144, bi=7168, bg=8192,
                cu=9216, cg=9728, mg=10240)
_WIDTHS = dict(aq=1536, ak=1536, av=1536, ag=512, bq=1024, bf=1024, bi=1024, bg=1024,
               cu=512, cg=512, mg=3072)

_PERM_WIDTH = 3 * A_WIDTH
_P_COL0 = (A_GROUPS - 1) * _PERM_WIDTH
_COL_P = dict(ak=0, av=512, cu=1024, cg=1536, aq=2048, ag=2560, bq=3072, bf=4096,
              bi=5120, bg=6144, mg=7168)
_P_WIDTH = IN_WIDTH - _P_COL0
_P_F32 = 2048
_COL_S = {name: _P_COL0 + off for name, off in _COL_P.items()}


def _attn_cols(g):
    if g == 0:
        return _COL_S["ak"], _COL_S["av"], _COL_S["aq"]
    base = (g - 1) * _PERM_WIDTH
    return base, base + A_WIDTH, base + 2 * A_WIDTH

_VMEM_LIMIT = 56 * 1024 * 1024


def _cparams(sem):
    return pltpu.CompilerParams(dimension_semantics=sem, vmem_limit_bytes=_VMEM_LIMIT)


def _nt(a, b):
    return lax.dot_general(a, b, (((1,), (1,)), ((), ())), preferred_element_type=F32)


def _tn(a, b):
    return lax.dot_general(a, b, (((0,), (0,)), ((), ())), preferred_element_type=F32)


def _sigmoid(x):
    return 1.0 / (1.0 + jnp.exp(-x))


def _rmsnorm_rows(x, gain):
    return x * lax.rsqrt(jnp.mean(x * x, axis=-1, keepdims=True) + EPS) * gain


def _in_proj_kernel(x_ref, g_ref, w_ref, *refs, n_f32_tiles, z_dtype):
    if n_f32_tiles:
        z_ref, zf_ref, xn_ref = refs
    else:
        z_ref, xn_ref = refs
    j = pl.program_id(1)

    @pl.when(j == 0)
    def _():
        xn_ref[...] = _rmsnorm_rows(x_ref[...], g_ref[...]).astype(BF16)

    acc = jnp.dot(xn_ref[...], w_ref[...], preferred_element_type=F32)
    z_ref[...] = acc.astype(z_dtype)
    if n_f32_tiles:
        @pl.when(j < n_f32_tiles)
        def _():
            zf_ref[...] = acc


def _in_proj(x, gain, w_bf16, *, col0, tm, tn, z_dtype, f32_cols):
    n = x.shape[0]
    width = w_bf16.shape[1] - col0
    j0 = col0 // tn
    n_f32_tiles = f32_cols // tn
    out_shape = [jax.ShapeDtypeStruct((n, width), z_dtype)]
    out_specs = [pl.BlockSpec((tm, tn), lambda i, j: (i, j))]
    if n_f32_tiles:
        out_shape.append(jax.ShapeDtypeStruct((n, f32_cols), F32))
        out_specs.append(pl.BlockSpec((tm, tn), lambda i, j: (i, jnp.minimum(j, n_f32_tiles - 1))))
    return pl.pallas_call(
        functools.partial(_in_proj_kernel, n_f32_tiles=n_f32_tiles, z_dtype=z_dtype),
        grid=(n // tm, width // tn),
        in_specs=[pl.BlockSpec((tm, D_MODEL), lambda i, j: (i, 0)),
                  pl.BlockSpec((1, D_MODEL), lambda i, j: (0, 0)),
                  pl.BlockSpec((D_MODEL, tn), lambda i, j: (0, j + j0))],
        out_specs=out_specs,
        out_shape=out_shape,
        scratch_shapes=[pltpu.VMEM((tm, D_MODEL), BF16)],
        compiler_params=_cparams(("parallel", "arbitrary")),
        name="in_proj",
    )(x, gain.reshape(1, D_MODEL), w_bf16)


def _in_proj_perm_kernel(x_ref, g_ref, w_ref, r_ref, f_ref, xs_ref, xn_ref, *, d, tm):
    xn = _rmsnorm_rows(x_ref[...], g_ref[...])
    n_chunk = D_MODEL // LANE
    rows = tm // d
    for c in range(n_chunk):
        xs_ref[c] = xn[:, c * LANE:(c + 1) * LANE]
    for r in range(d):
        for c in range(n_chunk):
            xn_ref[r * rows:(r + 1) * rows, c * LANE:(c + 1) * LANE] = (
                xs_ref[c, pl.ds(r, rows, stride=d), :].astype(BF16))
    acc = jnp.dot(xn_ref[...], w_ref[...], preferred_element_type=F32)
    r_ref[0] = acc.astype(BF16).reshape(d, rows, _PERM_WIDTH)
    f_ref[0] = acc[:, 0:2 * A_WIDTH].reshape(d, rows, 2 * A_WIDTH)


def _in_proj_perm(x, gain, w_bf16, g, *, b, s, d, tm):
    tpb = s // tm
    rows = tm // d
    idx = lambda i: (i // tpb, 0, i % tpb, 0)
    return pl.pallas_call(
        functools.partial(_in_proj_perm_kernel, d=d, tm=tm),
        grid=(b * tpb,),
        in_specs=[pl.BlockSpec((tm, D_MODEL), lambda i: (i, 0)),
                  pl.BlockSpec((1, D_MODEL), lambda i: (0, 0)),
                  pl.BlockSpec((D_MODEL, _PERM_WIDTH), lambda i: (0, g - 1))],
        out_specs=[pl.BlockSpec((1, d, rows, _PERM_WIDTH), idx),
                   pl.BlockSpec((1, d, rows, 2 * A_WIDTH), idx)],
        out_shape=[jax.ShapeDtypeStruct((b, d, s // d, _PERM_WIDTH), BF16),
                   jax.ShapeDtypeStruct((b, d, s // d, 2 * A_WIDTH), F32)],
        scratch_shapes=[pltpu.VMEM((D_MODEL // LANE, tm, LANE), F32),
                        pltpu.VMEM((tm, D_MODEL), BF16)],
        compiler_params=_cparams(("parallel",)),
        name="in_proj_d%d" % d,
    )(x, gain.reshape(1, D_MODEL), w_bf16)


def _attn_prompt_kernel(q_ref, kp_ref, kc_ref, vp_ref, vc_ref, o_ref, lse_ref, *, tq):
    i = pl.program_id(2)
    n = N_STEPS
    qi = lax.broadcasted_iota(jnp.int32, (n, 2 * n), 0)
    kj = lax.broadcasted_iota(jnp.int32, (n, 2 * n), 1)
    band = (kj >= qi) & (kj <= qi + n)
    band_first = band & ((i > 0) | (kj >= n))
    scale = A_HEAD_DIM ** -0.5
    for h in range(A_HEADS):
        hs = slice(h * A_HEAD_DIM, (h + 1) * A_HEAD_DIM)
        for j in range(tq // n):
            rows = slice(j * n, (j + 1) * n)
            q = q_ref[0, 0, rows, hs]
            if j == 0:
                k = jnp.concatenate([kp_ref[0, 0, :, hs], kc_ref[0, 0, rows, hs]], axis=0)
                v = jnp.concatenate([vp_ref[0, 0, :, hs], vc_ref[0, 0, rows, hs]], axis=0)
                valid = band_first
            else:
                k = kc_ref[0, 0, (j - 1) * n:(j + 1) * n, hs]
                v = vc_ref[0, 0, (j - 1) * n:(j + 1) * n, hs]
                valid = band
            s = jnp.where(valid, _nt(q, k) * scale, NEG)
            m = jnp.max(s, axis=-1, keepdims=True)
            p = jnp.exp(s - m)
            l = jnp.sum(p, axis=-1, keepdims=True)
            o = jnp.dot(p.astype(BF16), v, preferred_element_type=F32)
            o_ref[0, 0, rows, hs] = o / l
            lse_ref[0, 0, rows, hs] = jnp.broadcast_to(m + jnp.log(l), (n, A_HEAD_DIM))


def _attn_prompt(arr, cols, *, tq):
    b, d, rows, _ = arr.shape
    qc, kc, vc = cols
    prev_per_blk = tq // N_STEPS

    def cur(c):
        return pl.BlockSpec((1, 1, tq, A_WIDTH), lambda bb, r, i: (bb, r, i, c))

    def prev(c):
        return pl.BlockSpec((1, 1, N_STEPS, A_WIDTH),
                            lambda bb, r, i: (bb, r, jnp.maximum(i * prev_per_blk - 1, 0), c))

    out_spec = pl.BlockSpec((1, 1, tq, A_WIDTH), lambda bb, r, i: (bb, r, i, 0))
    return pl.pallas_call(
        functools.partial(_attn_prompt_kernel, tq=tq),
        grid=(b, d, rows // tq),
        in_specs=[cur(qc), prev(kc), cur(kc), prev(vc), cur(vc)],
        out_specs=[out_spec, out_spec],
        out_shape=[jax.ShapeDtypeStruct((b, d, rows, A_WIDTH), F32)] * 2,
        compiler_params=_cparams(("parallel", "parallel", "arbitrary")),
        name="attn_prompt_d%d" % d,
    )(arr, arr, arr, arr, arr)


def _gather_pitch(n_j):
    p = -(-n_j // 8)
    return 8 * (p if p % 2 else p + 1)


def _attn_sample_kernel(z_ref, c0_ref, c1_ref, c2_ref, *refs, t_new):
    out_refs, flat_refs = refs[:2 * A_GROUPS], refs[2 * A_GROUPS:]
    scale = A_HEAD_DIM ** -0.5
    nrow = A_HEADS * t_new
    per_res = 2 * A_HEADS
    for g, c_ref in enumerate((c0_ref, c1_ref, c2_ref)):
        d = A_PATTERNS[g][1]
        n_res = min(d, t_new)
        o_ref, lse_ref = out_refs[2 * g], out_refs[2 * g + 1]
        k_off, v_off, q_off = _attn_cols(g)
        q = z_ref[:, q_off:q_off + A_WIDTH]
        kn = z_ref[:, k_off:k_off + A_WIDTH]
        vn = z_ref[:, v_off:v_off + A_WIDTH]
        n_j = n_res * per_res
        pitch = _gather_pitch(n_j)
        flat_ref = flat_refs[g]
        for mm in range(N_STEPS):
            flat_ref[mm * pitch:mm * pitch + n_j, :] = c_ref[mm]
        gather = lambda j: flat_ref[pl.ds(j, N_STEPS, stride=pitch), :]
        kcat = jnp.concatenate([gather(r * per_res + h)
                                for r in range(n_res) for h in range(A_HEADS)], axis=1).astype(BF16)
        vcat = jnp.concatenate([gather(r * per_res + A_HEADS + h)
                                for r in range(n_res) for h in range(A_HEADS)], axis=1).astype(BF16)
        wide = n_res * A_WIDTH
        q4 = jnp.concatenate([q] * A_HEADS, axis=0)
        row = lax.broadcasted_iota(jnp.int32, (nrow, wide), 0)
        col = lax.broadcasted_iota(jnp.int32, (nrow, wide), 1)
        row_h, row_t = row // t_new, row % t_new
        sel = ((col // A_WIDTH) == (row_t % d)) & (((col % A_WIDTH) // A_HEAD_DIM) == row_h)
        qbd = jnp.where(sel, jnp.concatenate([q4] * n_res, axis=1), 0.0).astype(BF16)
        s_c = _nt(qbd, kcat) * scale
        r2 = lax.broadcasted_iota(jnp.int32, (nrow, N_STEPS), 0) % t_new
        c2 = lax.broadcasted_iota(jnp.int32, (nrow, N_STEPS), 1)
        s_c = jnp.where(c2 >= r2 // d, s_c, NEG)
        pad = jnp.zeros((N_STEPS - t_new, A_WIDTH), F32)
        knp = jnp.concatenate([kn, pad], axis=0).astype(BF16)
        vnp = jnp.concatenate([vn, pad], axis=0).astype(BF16)
        rowq = lax.broadcasted_iota(jnp.int32, (nrow, A_WIDTH), 0) // t_new
        colq = lax.broadcasted_iota(jnp.int32, (nrow, A_WIDTH), 1) // A_HEAD_DIM
        selq = rowq == colq
        qbd2 = jnp.where(selq, q4, 0.0).astype(BF16)
        s_n = _nt(qbd2, knp) * scale
        s_n = jnp.where((c2 <= r2) & ((r2 - c2) % d == 0), s_n, NEG)
        m = jnp.maximum(jnp.max(s_c, axis=-1, keepdims=True), jnp.max(s_n, axis=-1, keepdims=True))
        p_c = jnp.exp(s_c - m)
        p_n = jnp.exp(s_n - m)
        l = jnp.sum(p_c, axis=-1, keepdims=True) + jnp.sum(p_n, axis=-1, keepdims=True)
        o_wide = jnp.where(sel, jnp.dot(p_c.astype(BF16), vcat, preferred_element_type=F32), 0.0)
        o_new = jnp.where(selq, jnp.dot(p_n.astype(BF16), vnp, preferred_element_type=F32), 0.0)
        o = o_new[:, 0:A_HEAD_DIM]
        for c in range(1, A_HEADS):
            o = o + o_new[:, c * A_HEAD_DIM:(c + 1) * A_HEAD_DIM]
        for c in range(wide // A_HEAD_DIM):
            o = o + o_wide[:, c * A_HEAD_DIM:(c + 1) * A_HEAD_DIM]
        o = o / l
        lse = jnp.broadcast_to(m + jnp.log(l), (nrow, A_HEAD_DIM))
        for h in range(A_HEADS):
            hs = slice(h * A_HEAD_DIM, (h + 1) * A_HEAD_DIM)
            o_ref[:, hs] = o[h * t_new:(h + 1) * t_new]
            lse_ref[:, hs] = lse[h * t_new:(h + 1) * t_new]


def _attn_sample(z_s, caches, layer, *, n_seq, t_new):
    in_specs = [pl.BlockSpec((t_new, IN_WIDTH), lambda b: (b, 0))]
    views, scratch = [], []
    per_res = 2 * A_HEADS
    for g, c in enumerate(caches):
        win, d = A_PATTERNS[g]
        assert c.shape[2] == win == N_STEPS * d, "cache must hold one full window"
        n_res = min(d, t_new)
        views.append(c.reshape(c.shape[0], n_seq, N_STEPS, d * per_res, A_HEAD_DIM))
        in_specs.append(pl.BlockSpec((None, None, N_STEPS, n_res * per_res, A_HEAD_DIM),
                                     lambda b, layer=layer: (layer, b, 0, 0, 0)))
        scratch.append(pltpu.VMEM((N_STEPS * _gather_pitch(n_res * per_res), A_HEAD_DIM), F32))
    out_spec = pl.BlockSpec((t_new, A_WIDTH), lambda b: (b, 0))
    outs = pl.pallas_call(
        functools.partial(_attn_sample_kernel, t_new=t_new),
        grid=(n_seq,),
        in_specs=in_specs,
        out_specs=[out_spec] * (2 * A_GROUPS),
        out_shape=[jax.ShapeDtypeStruct((n_seq * t_new, A_WIDTH), F32)] * (2 * A_GROUPS),
        scratch_shapes=scratch,
        compiler_params=_cparams(("parallel",)),
        name="attn_sample",
    )(z_s, *views)
    return outs[0::2], outs[1::2]


def _hgrn_consts(seg):
    n = HGRN_ROWS
    t = np.arange(n)[:, None]
    u = np.arange(n)[None, :]
    seg_lo = (t // seg) * seg
    seg_hi = seg_lo + seg - 1
    mats = [(u >= seg_lo) & (u <= t),
            (u > t) & (u <= seg_hi)]
    level = np.full((n, n), -1, np.int32)
    s = u
    li = 0
    m = seg
    while m >= 2:
        half = m // 2
        upper = (t % m) >= half
        ref = (t // m) * m + half - 1
        if m > 2:
            mats.append(np.where(upper, (u > ref) & (u <= t), (u > t) & (u <= ref)))
        pair = (t // m == s // m) & ((t % m) >= half) & ((s % m) < half)
        level[pair] = li
        li += 1
        m //= 2
    level[np.arange(n), np.arange(n)] = li
    w = np.concatenate(mats, axis=0).astype(np.float32)
    w2 = np.concatenate([w, w], axis=1)
    return jnp.asarray(w2, BF16), jnp.asarray(level), li


def _hgrn_blocks(items, w2, level, n_lev, seg):
    n = HGRN_ROWS
    n_seg = n // seg
    row = lax.broadcasted_iota(jnp.int32, (n, B_DIM), 0)

    gates = []
    for q, xf, v, lb, states in items:
        f = lb + (1.0 - lb) * _sigmoid(xf)
        g = jnp.log2(f)
        g_hi = g.astype(BF16)
        g_lo = (g - g_hi.astype(F32)).astype(BF16)
        gates.append((f, jnp.concatenate([g_hi, g_lo], axis=0)))
    stage1 = []
    for i in range(0, len(items), 2):
        pair = gates[i:i + 2]
        rhs = pair[0][1] if len(pair) == 1 else jnp.concatenate([pair[0][1], pair[1][1]], axis=1)
        dsum = jnp.dot(w2, rhs, preferred_element_type=F32)
        for k, (f, _) in enumerate(pair):
            stage1.append((f, 1.0 - f, dsum[:, k * B_DIM:(k + 1) * B_DIM]))

    stage2 = []
    for (q, xf, v, lb, states), (f, kin, dsum) in zip(items, stage1):
        fac = jnp.exp2(dsum)
        ea = fac[0:n]
        ops = []
        m = seg
        for li in range(n_lev):
            half = m // 2
            if half % 8 == 0:
                src = jnp.concatenate([(kin if (r0 // half) % 2 == 0 else q)[r0:r0 + half]
                                       for r0 in range(0, n, half)], axis=0)
                tm = src * fac[(2 + li) * n:(3 + li) * n]
            elif m > 2:
                tm = jnp.where((row & half) != 0, q, kin) * fac[(2 + li) * n:(3 + li) * n]
            else:
                tm = jnp.where((row & half) != 0, q * f, kin)
            ops.append(tm.astype(BF16))
            m //= 2
        stage2.append((ea, (q * ea).astype(BF16), kin * fac[n:2 * n], ops))

    stage3 = []
    in_level = [level == li for li in range(n_lev + 1)]
    for (q, xf, v, lb, states), (f, kin, dsum), (ea, qt, kh, ops) in zip(items, stage1, stage2):
        att = jnp.zeros((n, n), F32)
        for li, tm in enumerate(ops):
            att = jnp.where(in_level[li], _nt(tm, tm), att)
        att = jnp.where(in_level[n_lev], _nt(q.astype(BF16), kin.astype(BF16)), att)
        stage3.append(att.astype(BF16))

    results = []
    for (q, xf, v, lb, states), (ea, qt, kh, ops), att in zip(items, stage2, stage3):
        vb = v.astype(BF16)
        o = jnp.dot(att, vb, preferred_element_type=F32)
        new_states = []
        for si in range(n_seg):
            st = states[si]
            o_s = _nt(qt, st.astype(BF16))
            if n_seg == 1:
                o = o + o_s
                kh_s = kh
            else:
                in_seg = (row >= si * seg) & (row < (si + 1) * seg)
                o = o + jnp.where(in_seg, o_s, 0.0)
                kh_s = jnp.where(in_seg, kh, 0.0)
            decay = ea[(si + 1) * seg - 1:(si + 1) * seg, :]
            new_states.append(st * decay + _tn(vb, kh_s.astype(BF16)))
        results.append((o, new_states))
    return results


def _lower_bound(lbl_ref, layer):
    lg = lbl_ref[...]
    e = jnp.exp(lg - jnp.max(lg, axis=0, keepdims=True))
    p = e / jnp.sum(e, axis=0, keepdims=True)
    return jnp.sum(p[0:layer + 1], axis=0, keepdims=True) - p[0:1]


def _hgrn_prompt_kernel(q_ref, f_ref, v_ref, lbl_ref, gn_ref, w3_ref, lev_ref, o_ref, s_ref,
                        st_ref, *, layer, n_lev, tl, n_heads):
    li = pl.program_id(2)

    @pl.when(li == 0)
    def _():
        st_ref[...] = jnp.zeros_like(st_ref)

    lb = _lower_bound(lbl_ref, layer)
    w3 = w3_ref[...]
    level = lev_ref[...]
    gain = gn_ref[...]
    head = [slice(hh * B_DIM, (hh + 1) * B_DIM) for hh in range(n_heads)]

    def body(c, carry):
        r0 = pl.multiple_of(c * HGRN_ROWS, HGRN_ROWS)
        rows = pl.ds(r0, HGRN_ROWS)
        items = [(q_ref[0, rows, hs].astype(F32), f_ref[0, rows, hs].astype(F32),
                  v_ref[0, rows, hs].astype(F32), lb[:, hs], [st_ref[hh]])
                 for hh, hs in enumerate(head)]
        for hh, (o, (st,)) in enumerate(_hgrn_blocks(items, w3, level, n_lev, HGRN_ROWS)):
            st_ref[hh] = st
            o_ref[0, rows, head[hh]] = _rmsnorm_rows(o, gain)
        return carry

    lax.fori_loop(0, tl // HGRN_ROWS, body, 0)

    @pl.when(li == pl.num_programs(2) - 1)
    def _():
        for hh in range(n_heads):
            s_ref[0, hh] = st_ref[hh].T


def _hgrn_prompt(z3, col, lb_logits, hgrn_gain, layer, *, tl):
    b, s, _ = z3.shape
    w3, level, n_lev = _hgrn_consts(HGRN_ROWS)
    nh = HGRN_HEADS_PER_STEP
    wblk = nh * B_DIM

    def cspec(name):
        c0 = col[name] // wblk
        return pl.BlockSpec((1, tl, wblk), lambda bb, h, i: (bb, i, c0 + h))

    const = lambda shape: pl.BlockSpec(shape, lambda bb, h, i: (0,) * len(shape))
    return pl.pallas_call(
        functools.partial(_hgrn_prompt_kernel, layer=layer, n_lev=n_lev, tl=tl, n_heads=nh),
        grid=(b, B_HEADS // nh, s // tl),
        in_specs=[cspec("bq"), cspec("bf"), cspec("bi"),
                  pl.BlockSpec((lb_logits.shape[0], wblk), lambda bb, h, i: (0, h)),
                  const((1, B_DIM)), const(w3.shape), const(level.shape)],
        out_specs=[pl.BlockSpec((1, tl, wblk), lambda bb, h, i: (bb, i, h)),
                   pl.BlockSpec((1, nh, B_DIM, B_DIM), lambda bb, h, i: (bb, h, 0, 0))],
        out_shape=[jax.ShapeDtypeStruct((b, s, B_WIDTH), F32),
                   jax.ShapeDtypeStruct((b, B_HEADS, B_DIM, B_DIM), F32)],
        scratch_shapes=[pltpu.VMEM((nh, B_DIM, B_DIM), F32)],
        compiler_params=_cparams(("parallel", "parallel", "arbitrary")),
        name="hgrn_prompt",
    )(z3, z3, z3, lb_logits, hgrn_gain.reshape(1, B_DIM), w3, level)


def _hgrn_sample_kernel(q_ref, f_ref, v_ref, s0_ref, lbl_ref, gn_ref, w3_ref, lev_ref, acc_ref,
                        o_ref, s_ref, *, layer, n_lev, t_new, n_heads):
    del acc_ref
    lb = _lower_bound(lbl_ref, layer)
    n_seg = HGRN_ROWS // t_new
    head = [slice(hh * B_DIM, (hh + 1) * B_DIM) for hh in range(n_heads)]
    items = [(q_ref[:, hs], f_ref[:, hs], v_ref[:, hs], lb[:, hs],
              [s0_ref[0, si, hh].T for si in range(n_seg)])
             for hh, hs in enumerate(head)]
    results = _hgrn_blocks(items, w3_ref[...], lev_ref[...], n_lev, t_new)
    for hh, (o, new_states) in enumerate(results):
        o_ref[:, head[hh]] = _rmsnorm_rows(o, gn_ref[...])
        for si in range(n_seg):
            s_ref[0, si, hh] = new_states[si].T


def _hgrn_sample(z_s, state, new_state, lb_logits, hgrn_gain, layer, *, n_seq, t_new):
    w3, level, n_lev = _hgrn_consts(t_new)
    n_seg = HGRN_ROWS // t_new
    nh = HGRN_SAMPLE_HEADS_PER_STEP
    wblk = nh * B_DIM

    def cspec(name):
        c0 = _COL_S[name] // wblk
        return pl.BlockSpec((HGRN_ROWS, wblk), lambda i, h: (i, c0 + h))

    const = lambda shape: pl.BlockSpec(shape, lambda i, h: (0,) * len(shape))
    return pl.pallas_call(
        functools.partial(_hgrn_sample_kernel, layer=layer, n_lev=n_lev, t_new=t_new, n_heads=nh),
        grid=(n_seq // n_seg, B_HEADS // nh),
        in_specs=[cspec("bq"), cspec("bf"), cspec("bi"),
                  pl.BlockSpec((1, n_seg, nh, B_DIM, B_DIM), lambda i, h, layer=layer: (layer, i, h, 0, 0)),
                  pl.BlockSpec((lb_logits.shape[0], wblk), lambda i, h: (0, h)),
                  const((1, B_DIM)), const(w3.shape), const(level.shape),
                  pl.BlockSpec(memory_space=pl.ANY)],
        out_specs=[pl.BlockSpec((HGRN_ROWS, wblk), lambda i, h: (i, h)),
                   pl.BlockSpec((1, n_seg, nh, B_DIM, B_DIM), lambda i, h, layer=layer: (layer, i, h, 0, 0))],
        out_shape=[jax.ShapeDtypeStruct((n_seq * t_new, B_WIDTH), F32),
                   jax.ShapeDtypeStruct(new_state.shape, F32)],
        input_output_aliases={8: 1},
        compiler_params=_cparams(("parallel", "parallel")),
        name="hgrn_sample",
    )(z_s, z_s, z_s, state, lb_logits, hgrn_gain.reshape(1, B_DIM), w3, level, new_state)


_POOL_PAD = 16


def _pool_mix(ext_at, u, pos, pw_ref, ps_ref):
    outs = []
    for g, win in enumerate(C_WINDOWS):
        gs = slice(g * C_GROUP_DIM, (g + 1) * C_GROUP_DIM)
        acc = u[:, gs]
        for k in range(1, win):
            acc = acc + ext_at(k, gs)
        cnt = jnp.minimum(pos + 1, win).astype(F32)
        pooled = acc * (1.0 / cnt) - u[:, gs]
        mixed = jnp.dot(pooled.astype(BF16), pw_ref[g].astype(BF16), preferred_element_type=F32)
        outs.append(mixed * ps_ref[:, gs])
    return outs


def _pool_prompt_kernel(u_ref, h_ref, pw_ref, ps_ref, o_ref, ext_ref, *, tl):
    i = pl.program_id(1)
    halo = h_ref[0].astype(F32)
    ext_ref[0:_POOL_PAD, :] = jnp.where(i > 0, halo, 0.0)
    u = u_ref[0].astype(F32)
    ext_ref[_POOL_PAD:, :] = u
    pos = i * tl + lax.broadcasted_iota(jnp.int32, (tl, 1), 0)
    outs = _pool_mix(lambda k, gs: ext_ref[_POOL_PAD - k:_POOL_PAD - k + tl, gs], u, pos, pw_ref, ps_ref)
    for g, og in enumerate(outs):
        o_ref[0, :, g * C_GROUP_DIM:(g + 1) * C_GROUP_DIM] = og


def _pool_prompt(z3, col, pool_w, pool_scale, *, tl):
    b, s, _ = z3.shape
    c0 = col["cu"] // C_WIDTH
    per = tl // _POOL_PAD
    return pl.pallas_call(
        functools.partial(_pool_prompt_kernel, tl=tl),
        grid=(b, s // tl),
        in_specs=[pl.BlockSpec((1, tl, C_WIDTH), lambda bb, i: (bb, i, c0)),
                  pl.BlockSpec((1, _POOL_PAD, C_WIDTH), lambda bb, i: (bb, jnp.maximum(i * per - 1, 0), c0)),
                  pl.BlockSpec((C_GROUPS, C_GROUP_DIM, C_GROUP_DIM), lambda bb, i: (0, 0, 0)),
                  pl.BlockSpec((1, C_WIDTH), lambda bb, i: (0, 0))],
        out_specs=pl.BlockSpec((1, tl, C_WIDTH), lambda bb, i: (bb, i, 0)),
        out_shape=jax.ShapeDtypeStruct((b, s, C_WIDTH), F32),
        scratch_shapes=[pltpu.VMEM((_POOL_PAD + tl, C_WIDTH), F32)],
        compiler_params=_cparams(("parallel", "arbitrary")),
        name="pool_prompt",
    )(z3, z3, pool_w, pool_scale.reshape(1, C_WIDTH))


def _pool_sample_kernel(u_ref, h_ref, pw_ref, ps_ref, o_ref, ext_ref, *, nb, t_new, first_pos):
    ext_ref[:, 0:1, :] = jnp.zeros((nb, 1, C_WIDTH), F32)
    ext_ref[:, 1:_POOL_PAD, :] = h_ref[0]
    u3 = u_ref[...].reshape(nb, t_new, C_WIDTH)
    ext_ref[:, _POOL_PAD:, :] = u3
    u = u_ref[...]
    pos = first_pos + lax.broadcasted_iota(jnp.int32, (nb * t_new, 1), 0) % t_new

    def ext_at(k, gs):
        return ext_ref[:, _POOL_PAD - k:_POOL_PAD - k + t_new, gs].reshape(nb * t_new, C_GROUP_DIM)

    outs = _pool_mix(ext_at, u, pos, pw_ref, ps_ref)
    for g, og in enumerate(outs):
        o_ref[:, g * C_GROUP_DIM:(g + 1) * C_GROUP_DIM] = og


def _pool_sample(z_s, state_pool, pool_w, pool_scale, layer, *, n_seq, t_new, first_pos, nb):
    c0 = _COL_S["cu"] // C_WIDTH
    return pl.pallas_call(
        functools.partial(_pool_sample_kernel, nb=nb, t_new=t_new, first_pos=first_pos),
        grid=(n_seq // nb,),
        in_specs=[pl.BlockSpec((nb * t_new, C_WIDTH), lambda i: (i, c0)),
                  pl.BlockSpec((1, nb, C_HIST, C_WIDTH), lambda i, layer=layer: (layer, i, 0, 0)),
                  pl.BlockSpec((C_GROUPS, C_GROUP_DIM, C_GROUP_DIM), lambda i: (0, 0, 0)),
                  pl.BlockSpec((1, C_WIDTH), lambda i: (0, 0))],
        out_specs=pl.BlockSpec((nb * t_new, C_WIDTH), lambda i: (i, 0)),
        out_shape=jax.ShapeDtypeStruct((n_seq * t_new, C_WIDTH), F32),
        scratch_shapes=[pltpu.VMEM((nb, _POOL_PAD + t_new, C_WIDTH), F32)],
        compiler_params=_cparams(("parallel",)),
        name="pool_sample",
    )(z_s, state_pool, pool_w, pool_scale.reshape(1, C_WIDTH))


def _token_order(ref, scr_ref, d, tm):
    if d == 1:
        return ref[0, 0]
    n_chunk = A_WIDTH // LANE
    for r in range(d):
        for c in range(n_chunk):
            scr_ref[c, pl.ds(r, tm // d, stride=d), :] = ref[0, r, :, c * LANE:(c + 1) * LANE]
    return jnp.concatenate([scr_ref[c] for c in range(n_chunk)], axis=1)


def _out_proj_kernel(x_ref, ag_ref, bg_ref, cg_ref, m0_ref, m1_ref, m2_ref, o0_ref, o1_ref, o2_ref,
                     l0_ref, l1_ref, l2_ref, ob_ref, oc_ref, wa_ref, wb_ref, wc_ref, wo_ref,
                     gp_ref, y_ref, *scratch, dil, tm):
    scr = iter(scratch)
    grp = []
    for o_ref, l_ref, d in zip((o0_ref, o1_ref, o2_ref), (l0_ref, l1_ref, l2_ref), dil):
        so = next(scr) if d > 1 else None
        sl = next(scr) if d > 1 else None
        grp.append((_token_order(o_ref, so, d, tm), _token_order(l_ref, sl, d, tm)))
    (o0, l0), (o1, l1), (o2, l2) = grp
    mx = jnp.maximum(jnp.maximum(l0, l1), l2)
    e0, e1, e2 = jnp.exp(l0 - mx), jnp.exp(l1 - mx), jnp.exp(l2 - mx)
    o_a = (e0 * o0 + e1 * o1 + e2 * o2) / (e0 + e1 + e2)
    def silu_gated(o, g_ref):
        h = g_ref[...].astype(F32) * 0.5
        return (o * (h + h * jnp.tanh(h))).astype(BF16)

    def merge_gated(m_ref, y, w_ref):
        p = jnp.dot(y, w_ref[...], preferred_element_type=F32)
        return p + jnp.tanh(m_ref[...].astype(F32) * 0.5) * p

    merged = (merge_gated(m0_ref, silu_gated(o_a, ag_ref), wa_ref)
              + merge_gated(m1_ref, silu_gated(ob_ref[...], bg_ref), wb_ref)
              + merge_gated(m2_ref, silu_gated(oc_ref[...], cg_ref), wc_ref)) * 0.5
    out = jnp.dot(merged.astype(BF16), wo_ref[...], preferred_element_type=F32)
    y_ref[...] = x_ref[...] + _rmsnorm_rows(out, gp_ref[...])


def _out_proj(x, z, col, o_g, lse_g, o_b, o_c, wa, wb, wc, wo, gain_post, *, tm):
    n = x.shape[0]
    dil = tuple(int(a.shape[1]) for a in o_g)
    tpb = (n // o_g[0].shape[0]) // tm

    def zcol(name, width, k=0):
        assert col[name] % width == 0
        c0 = col[name] // width + k
        return pl.BlockSpec((tm, width), lambda i: (i, c0))

    def gspec(d):
        return pl.BlockSpec((1, d, tm // d, A_WIDTH), lambda i: (i // tpb, 0, i % tpb, 0))

    rows = lambda width: pl.BlockSpec((tm, width), lambda i: (i, 0))
    const = lambda a: pl.BlockSpec(a.shape, lambda i: (0,) * a.ndim)
    gp = gain_post.reshape(1, D_MODEL)
    scratch = []
    for d in dil:
        if d > 1:
            scratch += [pltpu.VMEM((A_WIDTH // LANE, tm, LANE), F32)] * 2
    return pl.pallas_call(
        functools.partial(_out_proj_kernel, dil=dil, tm=tm),
        grid=(n // tm,),
        in_specs=[rows(D_MODEL), zcol("ag", A_WIDTH), zcol("bg", B_WIDTH), zcol("cg", C_WIDTH),
                  zcol("mg", D_MODEL, 0), zcol("mg", D_MODEL, 1), zcol("mg", D_MODEL, 2)]
                 + [gspec(d) for d in dil] * 2 + [rows(B_WIDTH), rows(C_WIDTH)]
                 + [const(wa), const(wb), const(wc), const(wo), const(gp)],
        out_specs=rows(D_MODEL),
        out_shape=jax.ShapeDtypeStruct((n, D_MODEL), F32),
        scratch_shapes=scratch,
        compiler_params=_cparams(("parallel",)),
        name="out_proj",
    )(x, z, z, z, z, z, z, *o_g, *lse_g, o_b, o_c, wa, wb, wc, wo, gp)


def _ref_cols(w, name, g=None):
    off = _REF_OFF[name]
    if g is None:
        return w[:, off:off + _WIDTHS[name]]
    return w[:, off + g * A_WIDTH:off + (g + 1) * A_WIDTH]


def _prep_w_in(w):
    parts = []
    for g in range(1, A_GROUPS):
        parts += [_ref_cols(w, "ak", g), _ref_cols(w, "av", g), _ref_cols(w, "aq", g)]
    parts += [_ref_cols(w, "ak", 0), _ref_cols(w, "av", 0), _ref_cols(w, "cu"), _ref_cols(w, "cg"),
              _ref_cols(w, "aq", 0)] + [_ref_cols(w, k) for k in ("ag", "bq", "bf", "bi", "bg", "mg")]
    return jnp.concatenate(parts, axis=1).astype(BF16)


def _kv_split(kv):
    return kv.reshape(kv.shape[:-1] + (2, A_HEADS, A_HEAD_DIM))


def kernel(x_prompt, x_sample, cache_kv_w128, cache_kv_w512, cache_kv_w2048, state_hgrn, state_pool,
           norm_pre, norm_post, w_in, hgrn_lb_logits, hgrn_norm, pool_w, pool_scale,
           w_br_a, w_br_b, w_br_c, w_out):
    depth = w_in.shape[0]
    b_p, s_p, _ = x_prompt.shape
    n_seq, t_new, _ = x_sample.shape
    caches = (cache_kv_w128, cache_kv_w512, cache_kv_w2048)
    past_len = cache_kv_w2048.shape[2]
    assert HGRN_ROWS % t_new == 0 and n_seq % (HGRN_ROWS // t_new) == 0
    n_p, n_s = b_p * s_p, n_seq * t_new

    hp = x_prompt.reshape(n_p, D_MODEL)
    hs = x_sample.reshape(n_s, D_MODEL)
    kv_p = [[] for _ in range(A_GROUPS)]
    kv_s = [[] for _ in range(A_GROUPS)]
    hg_p, pl_p, pl_s = [], [], []
    hg_s = jnp.zeros(state_hgrn.shape, F32)
    tm_p = min(1024, s_p)
    for l in range(depth):
        w_l = _prep_w_in(w_in[l])
        wa, wb, wc, wo = (w[l].astype(BF16) for w in (w_br_a, w_br_b, w_br_c, w_out))

        z_p, zf_p = _in_proj(hp, norm_pre[l], w_l, col0=_P_COL0, tm=min(2048, s_p), tn=512,
                             z_dtype=BF16, f32_cols=_P_F32)
        z3 = z_p.reshape(b_p, s_p, _P_WIDTH)
        zf3 = zf_p.reshape(b_p, s_p, _P_F32)
        o_g, lse_g = [], []
        for g, (win, d) in enumerate(A_PATTERNS):
            rows = min(win, s_p)
            if g == 0:
                arr = z_p.reshape(b_p, 1, s_p, _P_WIDTH)
                cols = (_COL_P["aq"] // A_WIDTH, _COL_P["ak"] // A_WIDTH, _COL_P["av"] // A_WIDTH)
                kv_p[g].append(_kv_split(zf3[:, s_p - rows:, 0:2 * A_WIDTH]))
            else:
                arr, f_g = _in_proj_perm(hp, norm_pre[l], w_l, g, b=b_p, s=s_p, d=d, tm=tm_p)
                cols = (2, 0, 1)
                tail = f_g[:, :, (s_p - rows) // d:, :]
                kv_p[g].append(_kv_split(tail.transpose(0, 2, 1, 3).reshape(b_p, rows, 2 * A_WIDTH)))
            o, lse = _attn_prompt(arr, cols, tq=min(512, s_p // d))
            o_g.append(o)
            lse_g.append(lse)
        o_b, s_new = _hgrn_prompt(z3, _COL_P, hgrn_lb_logits, hgrn_norm[l], l, tl=min(1024, s_p))
        hg_p.append(s_new)
        o_c = _pool_prompt(z3, _COL_P, pool_w[l], pool_scale[l], tl=min(1024, s_p))
        pl_p.append(zf3[:, s_p - C_HIST:, _COL_P["cu"]:_COL_P["cu"] + C_WIDTH])
        hp = _out_proj(hp, z_p, _COL_P, o_g, lse_g, o_b.reshape(n_p, B_WIDTH),
                       o_c.reshape(n_p, C_WIDTH), wa, wb, wc, wo, norm_post[l], tm=min(512, s_p))

        (z_s,) = _in_proj(hs, norm_pre[l], w_l, col0=0, tm=min(1024, n_s), tn=1024, z_dtype=F32,
                          f32_cols=0)
        o_g, lse_g = _attn_sample(z_s, caches, l, n_seq=n_seq, t_new=t_new)
        zs3 = z_s.reshape(n_seq, t_new, IN_WIDTH)
        for g in range(A_GROUPS):
            k_off, v_off, _ = _attn_cols(g)
            kv_s[g].append(_kv_split(zs3[:, :, k_off:v_off + A_WIDTH]))
        o_b, hg_s = _hgrn_sample(z_s, state_hgrn, hg_s, hgrn_lb_logits, hgrn_norm[l], l,
                                 n_seq=n_seq, t_new=t_new)
        o_c = _pool_sample(z_s, state_pool, pool_w[l], pool_scale[l], l, n_seq=n_seq,
                           t_new=t_new, first_pos=past_len, nb=HGRN_ROWS // t_new)
        u_new = zs3[:, :, _COL_S["cu"]:_COL_S["cu"] + C_WIDTH]
        pl_s.append(jnp.concatenate([state_pool[l], u_new], axis=1)[:, -C_HIST:])
        as4 = lambda a: a.reshape(1, 1, n_s, A_WIDTH)
        hs = _out_proj(hs, z_s, _COL_S, [as4(a) for a in o_g], [as4(a) for a in lse_g], o_b, o_c,
                       wa, wb, wc, wo, norm_post[l], tm=min(256, n_s))

    return (hp.reshape(b_p, s_p, D_MODEL), hs.reshape(n_seq, t_new, D_MODEL),
            jnp.stack(kv_p[0]), jnp.stack(kv_p[1]), jnp.stack(kv_p[2]),
            jnp.stack(kv_s[0]), jnp.stack(kv_s[1]), jnp.stack(kv_s[2]),
            jnp.stack(hg_p), hg_s, jnp.stack(pl_p), jnp.stack(pl_s))
```

```python
import functools

import numpy as np
import jax
import jax.numpy as jnp
from jax import lax
from jax.experimental import pallas as pl
from jax.experimental.pallas import tpu as pltpu

F32 = jnp.float32
BF16 = jnp.bfloat16

D_MODEL = 1024
A_PATTERNS = ((128, 1), (512, 4), (2048, 16))
A_GROUPS = 3
A_HEADS = 4
A_HEAD_DIM = 128
A_WIDTH = A_HEADS * A_HEAD_DIM
N_STEPS = 128
B_HEADS = 8
B_DIM = 128
B_WIDTH = B_HEADS * B_DIM
C_WINDOWS = (2, 4, 8, 16)
C_GROUPS = 4
C_GROUP_DIM = 128
C_WIDTH = C_GROUPS * C_GROUP_DIM
C_HIST = 15
N_BRANCH = 3
EPS = 1e-6
NEG = -1e30
IN_WIDTH = 13312
LANE = 128
HGRN_ROWS = 128
HGRN_HEADS_PER_STEP = 8
HGRN_SAMPLE_HEADS_PER_STEP = 4

_REF_OFF = dict(aq=0, ak=1536, av=3072, ag=4608, bq=5120, bf=6144, bi=7168, bg=8192,
                cu=9216, cg=9728, mg=10240)
_WIDTHS = dict(aq=1536, ak=1536, av=1536, ag=512, bq=1024, bf=1024, bi=1024, bg=1024,
               cu=512, cg=512, mg=3072)

_PERM_WIDTH = 3 * A_WIDTH
_P_COL0 = (A_GROUPS - 1) * _PERM_WIDTH
_COL_P = dict(ak=0, av=512, cu=1024, cg=1536, aq=2048, ag=2560, bq=3072, bf=4096,
              bi=5120, bg=6144, mg=7168)
_P_WIDTH = IN_WIDTH - _P_COL0
_COL_S = {name: _P_COL0 + off for name, off in _COL_P.items()}


def _attn_cols(g):
    if g == 0:
        return _COL_S["ak"], _COL_S["av"], _COL_S["aq"]
    base = (g - 1) * _PERM_WIDTH
    return base, base + A_WIDTH, base + 2 * A_WIDTH

_VMEM_LIMIT = 56 * 1024 * 1024


def _cparams(sem):
    return pltpu.CompilerParams(dimension_semantics=sem, vmem_limit_bytes=_VMEM_LIMIT)


def _nt(a, b):
    return lax.dot_general(a, b, (((1,), (1,)), ((), ())), preferred_element_type=F32)


def _tn(a, b):
    return lax.dot_general(a, b, (((0,), (0,)), ((), ())), preferred_element_type=F32)


def _sigmoid(x):
    return 1.0 / (1.0 + jnp.exp(-x))


def _rmsnorm_rows(x, gain):
    return x * lax.rsqrt(jnp.mean(x * x, axis=-1, keepdims=True) + EPS) * gain


def _in_proj_kernel(x_ref, g_ref, w_ref, *refs, n_f32_tiles, z_dtype):
    if n_f32_tiles:
        z_ref, zf_ref, xn_ref = refs
    else:
        z_ref, xn_ref = refs
    j = pl.program_id(1)

    @pl.when(j == 0)
    def _():
        xn_ref[...] = _rmsnorm_rows(x_ref[...], g_ref[...]).astype(BF16)

    acc = jnp.dot(xn_ref[...], w_ref[...], preferred_element_type=F32)
    z_ref[...] = acc.astype(z_dtype)
    if n_f32_tiles:
        @pl.when(j < n_f32_tiles)
        def _():
            zf_ref[...] = acc


def _in_proj(x, gain, w_bf16, *, col0, tm, tn, z_dtype, f32_cols, width=None):
    n = x.shape[0]
    width = w_bf16.shape[1] - col0 if width is None else width
    j0 = col0 // tn
    n_f32_tiles = f32_cols // tn
    out_shape = [jax.ShapeDtypeStruct((n, width), z_dtype)]
    out_specs = [pl.BlockSpec((tm, tn), lambda i, j: (i, j))]
    if n_f32_tiles:
        out_shape.append(jax.ShapeDtypeStruct((n, f32_cols), F32))
        out_specs.append(pl.BlockSpec((tm, tn), lambda i, j: (i, jnp.minimum(j, n_f32_tiles - 1))))
    return pl.pallas_call(
        functools.partial(_in_proj_kernel, n_f32_tiles=n_f32_tiles, z_dtype=z_dtype),
        grid=(n // tm, width // tn),
        in_specs=[pl.BlockSpec((tm, D_MODEL), lambda i, j: (i, 0)),
                  pl.BlockSpec((1, D_MODEL), lambda i, j: (0, 0)),
                  pl.BlockSpec((D_MODEL, tn), lambda i, j: (0, j + j0))],
        out_specs=out_specs,
        out_shape=out_shape,
        scratch_shapes=[pltpu.VMEM((tm, D_MODEL), BF16)],
        compiler_params=_cparams(("parallel", "arbitrary")),
        name="in_proj",
    )(x, gain.reshape(1, D_MODEL), w_bf16)


def _in_proj_perm_kernel(x_ref, g_ref, w_ref, r_ref, xs_ref, xn_ref, *, d, tm):
    xn = _rmsnorm_rows(x_ref[...], g_ref[...])
    n_chunk = D_MODEL // LANE
    rows = tm // d
    for c in range(n_chunk):
        xs_ref[c] = xn[:, c * LANE:(c + 1) * LANE]
    for r in range(d):
        for c in range(n_chunk):
            xn_ref[r * rows:(r + 1) * rows, c * LANE:(c + 1) * LANE] = (
                xs_ref[c, pl.ds(r, rows, stride=d), :].astype(BF16))
    acc = jnp.dot(xn_ref[...], w_ref[...], preferred_element_type=F32)
    r_ref[0] = acc.astype(BF16).reshape(d, rows, _PERM_WIDTH)


def _in_proj_perm(x, gain, w_bf16, g, *, b, s, d, tm):
    tpb = s // tm
    rows = tm // d
    idx = lambda i: (i // tpb, 0, i % tpb, 0)
    return pl.pallas_call(
        functools.partial(_in_proj_perm_kernel, d=d, tm=tm),
        grid=(b * tpb,),
        in_specs=[pl.BlockSpec((tm, D_MODEL), lambda i: (i, 0)),
                  pl.BlockSpec((1, D_MODEL), lambda i: (0, 0)),
                  pl.BlockSpec((D_MODEL, _PERM_WIDTH), lambda i: (0, g - 1))],
        out_specs=pl.BlockSpec((1, d, rows, _PERM_WIDTH), idx),
        out_shape=jax.ShapeDtypeStruct((b, d, s // d, _PERM_WIDTH), BF16),
        scratch_shapes=[pltpu.VMEM((D_MODEL // LANE, tm, LANE), F32),
                        pltpu.VMEM((tm, D_MODEL), BF16)],
        compiler_params=_cparams(("parallel",)),
        name="in_proj_d%d" % d,
    )(x, gain.reshape(1, D_MODEL), w_bf16)


def _attn_prompt_kernel(q_ref, kp_ref, kc_ref, vp_ref, vc_ref, o_ref, lse_ref, *, tq):
    i = pl.program_id(2)
    n = N_STEPS
    qi = lax.broadcasted_iota(jnp.int32, (n, 2 * n), 0)
    kj = lax.broadcasted_iota(jnp.int32, (n, 2 * n), 1)
    band = (kj >= qi) & (kj <= qi + n)
    band_first = band & ((i > 0) | (kj >= n))
    scale = A_HEAD_DIM ** -0.5
    lane = lax.broadcasted_iota(jnp.int32, (n, LANE), 1)
    for j in range(tq // n):
        rows = slice(j * n, (j + 1) * n)
        lse = jnp.zeros((n, LANE), F32)
        for h in range(A_HEADS):
            hs = slice(h * A_HEAD_DIM, (h + 1) * A_HEAD_DIM)
            q = q_ref[0, 0, rows, hs]
            if j == 0:
                k = jnp.concatenate([kp_ref[0, 0, :, hs], kc_ref[0, 0, rows, hs]], axis=0)
                v = jnp.concatenate([vp_ref[0, 0, :, hs], vc_ref[0, 0, rows, hs]], axis=0)
                valid = band_first
            else:
                k = kc_ref[0, 0, (j - 1) * n:(j + 1) * n, hs]
                v = vc_ref[0, 0, (j - 1) * n:(j + 1) * n, hs]
                valid = band
            s = jnp.where(valid, _nt(q, k) * scale, NEG)
            m = jnp.max(s, axis=-1, keepdims=True)
            p = jnp.exp(s - m)
            l = jnp.sum(p, axis=-1, keepdims=True)
            o = jnp.dot(p.astype(BF16), v, preferred_element_type=F32)
            o_ref[0, 0, rows, hs] = o / l
            lse = jnp.where(lane == h, m + jnp.log(l), lse)
        lse_ref[0, 0, rows, :] = lse


def _attn_prompt(arr, cols, *, tq):
    b, d, rows, _ = arr.shape
    qc, kc, vc = cols
    prev_per_blk = tq // N_STEPS

    def cur(c):
        return pl.BlockSpec((1, 1, tq, A_WIDTH), lambda bb, r, i: (bb, r, i, c))

    def prev(c):
        return pl.BlockSpec((1, 1, N_STEPS, A_WIDTH),
                            lambda bb, r, i: (bb, r, jnp.maximum(i * prev_per_blk - 1, 0), c))

    out_spec = pl.BlockSpec((1, 1, tq, A_WIDTH), lambda bb, r, i: (bb, r, i, 0))
    return pl.pallas_call(
        functools.partial(_attn_prompt_kernel, tq=tq),
        grid=(b, d, rows // tq),
        in_specs=[cur(qc), prev(kc), cur(kc), prev(vc), cur(vc)],
        out_specs=[out_spec, pl.BlockSpec((1, 1, tq, LANE), lambda bb, r, i: (bb, r, i, 0))],
        out_shape=[jax.ShapeDtypeStruct((b, d, rows, A_WIDTH), F32),
                   jax.ShapeDtypeStruct((b, d, rows, LANE), F32)],
        compiler_params=_cparams(("parallel", "parallel", "arbitrary")),
        name="attn_prompt_d%d" % d,
    )(arr, arr, arr, arr, arr)


def _gather_pitch(n_j):
    p = -(-n_j // 8)
    return 8 * (p if p % 2 else p + 1)


def _attn_sample_kernel(z_ref, c0_ref, c1_ref, c2_ref, *refs, t_new):
    out_refs, flat_refs = refs[:2 * A_GROUPS], refs[2 * A_GROUPS:]
    scale = A_HEAD_DIM ** -0.5
    nrow = A_HEADS * t_new
    per_res = 2 * A_HEADS
    for g, c_ref in enumerate((c0_ref, c1_ref, c2_ref)):
        d = A_PATTERNS[g][1]
        n_res = min(d, t_new)
        o_ref, lse_ref = out_refs[2 * g], out_refs[2 * g + 1]
        k_off, v_off, q_off = _attn_cols(g)
        q = z_ref[:, q_off:q_off + A_WIDTH]
        kn = z_ref[:, k_off:k_off + A_WIDTH]
        vn = z_ref[:, v_off:v_off + A_WIDTH]
        n_j = n_res * per_res
        pitch = _gather_pitch(n_j)
        flat_ref = flat_refs[g]
        for mm in range(N_STEPS):
            flat_ref[mm * pitch:mm * pitch + n_j, :] = c_ref[mm]
        gather = lambda j: flat_ref[pl.ds(j, N_STEPS, stride=pitch), :]
        kcat = jnp.concatenate([gather(r * per_res + h)
                                for r in range(n_res) for h in range(A_HEADS)], axis=1).astype(BF16)
        vcat = jnp.concatenate([gather(r * per_res + A_HEADS + h)
                                for r in range(n_res) for h in range(A_HEADS)], axis=1).astype(BF16)
        wide = n_res * A_WIDTH
        q4 = jnp.concatenate([q] * A_HEADS, axis=0)
        row = lax.broadcasted_iota(jnp.int32, (nrow, wide), 0)
        col = lax.broadcasted_iota(jnp.int32, (nrow, wide), 1)
        row_h, row_t = row // t_new, row % t_new
        sel = ((col // A_WIDTH) == (row_t % d)) & (((col % A_WIDTH) // A_HEAD_DIM) == row_h)
        qbd = jnp.where(sel, jnp.concatenate([q4] * n_res, axis=1), 0.0).astype(BF16)
        s_c = _nt(qbd, kcat) * scale
        r2 = lax.broadcasted_iota(jnp.int32, (nrow, N_STEPS), 0) % t_new
        c2 = lax.broadcasted_iota(jnp.int32, (nrow, N_STEPS), 1)
        s_c = jnp.where(c2 >= r2 // d, s_c, NEG)
        pad = jnp.zeros((N_STEPS - t_new, A_WIDTH), F32)
        knp = jnp.concatenate([kn, pad], axis=0).astype(BF16)
        vnp = jnp.concatenate([vn, pad], axis=0).astype(BF16)
        rowq = lax.broadcasted_iota(jnp.int32, (nrow, A_WIDTH), 0) // t_new
        colq = lax.broadcasted_iota(jnp.int32, (nrow, A_WIDTH), 1) // A_HEAD_DIM
        selq = rowq == colq
        qbd2 = jnp.where(selq, q4, 0.0).astype(BF16)
        s_n = _nt(qbd2, knp) * scale
        s_n = jnp.where((c2 <= r2) & ((r2 - c2) % d == 0), s_n, NEG)
        m = jnp.maximum(jnp.max(s_c, axis=-1, keepdims=True), jnp.max(s_n, axis=-1, keepdims=True))
        p_c = jnp.exp(s_c - m)
        p_n = jnp.exp(s_n - m)
        l = jnp.sum(p_c, axis=-1, keepdims=True) + jnp.sum(p_n, axis=-1, keepdims=True)
        o_wide = jnp.where(sel, jnp.dot(p_c.astype(BF16), vcat, preferred_element_type=F32), 0.0)
        o_new = jnp.where(selq, jnp.dot(p_n.astype(BF16), vnp, preferred_element_type=F32), 0.0)
        o = o_new[:, 0:A_HEAD_DIM]
        for c in range(1, A_HEADS):
            o = o + o_new[:, c * A_HEAD_DIM:(c + 1) * A_HEAD_DIM]
        for c in range(wide // A_HEAD_DIM):
            o = o + o_wide[:, c * A_HEAD_DIM:(c + 1) * A_HEAD_DIM]
        o = o / l
        lse = m + jnp.log(l)
        lane = lax.broadcasted_iota(jnp.int32, (t_new, LANE), 1)
        lse_out = jnp.zeros((t_new, LANE), F32)
        for h in range(A_HEADS):
            hs = slice(h * A_HEAD_DIM, (h + 1) * A_HEAD_DIM)
            o_ref[:, hs] = o[h * t_new:(h + 1) * t_new]
            lse_out = jnp.where(lane == h, lse[h * t_new:(h + 1) * t_new], lse_out)
        lse_ref[...] = lse_out


def _attn_sample(z_s, caches, layer, *, n_seq, t_new):
    in_specs = [pl.BlockSpec((t_new, IN_WIDTH), lambda b: (b, 0))]
    views, scratch = [], []
    per_res = 2 * A_HEADS
    for g, c in enumerate(caches):
        win, d = A_PATTERNS[g]
        assert c.shape[2] == win == N_STEPS * d, "cache must hold one full window"
        n_res = min(d, t_new)
        views.append(c.reshape(c.shape[0], n_seq, N_STEPS, d * per_res, A_HEAD_DIM))
        in_specs.append(pl.BlockSpec((None, None, N_STEPS, n_res * per_res, A_HEAD_DIM),
                                     lambda b, layer=layer: (layer, b, 0, 0, 0)))
        scratch.append(pltpu.VMEM((N_STEPS * _gather_pitch(n_res * per_res), A_HEAD_DIM), F32))
    out_specs = [pl.BlockSpec((t_new, A_WIDTH), lambda b: (b, 0)),
                 pl.BlockSpec((t_new, LANE), lambda b: (b, 0))] * A_GROUPS
    out_shape = [jax.ShapeDtypeStruct((n_seq * t_new, A_WIDTH), F32),
                 jax.ShapeDtypeStruct((n_seq * t_new, LANE), F32)] * A_GROUPS
    outs = pl.pallas_call(
        functools.partial(_attn_sample_kernel, t_new=t_new),
        grid=(n_seq,),
        in_specs=in_specs,
        out_specs=out_specs,
        out_shape=out_shape,
        scratch_shapes=scratch,
        compiler_params=_cparams(("parallel",)),
        name="attn_sample",
    )(z_s, *views)
    return outs[0::2], outs[1::2]


def _hgrn_consts(seg):
    n = HGRN_ROWS
    t = np.arange(n)[:, None]
    u = np.arange(n)[None, :]
    seg_lo = (t // seg) * seg
    seg_hi = seg_lo + seg - 1
    mats = [(u >= seg_lo) & (u <= t),
            (u > t) & (u <= seg_hi)]
    level = np.full((n, n), -1, np.int32)
    s = u
    li = 0
    m = seg
    while m >= 2:
        half = m // 2
        upper = (t % m) >= half
        ref = (t // m) * m + half - 1
        if m > 2:
            mats.append(np.where(upper, (u > ref) & (u <= t), (u > t) & (u <= ref)))
        pair = (t // m == s // m) & ((t % m) >= half) & ((s % m) < half)
        level[pair] = li
        li += 1
        m //= 2
    level[np.arange(n), np.arange(n)] = li
    w = np.concatenate(mats, axis=0).astype(np.float32)
    w2 = np.concatenate([w, w], axis=1)
    return jnp.asarray(w2, BF16), jnp.asarray(level), li


def _hgrn_blocks(items, w2, level, n_lev, seg):
    n = HGRN_ROWS
    n_seg = n // seg
    row = lax.broadcasted_iota(jnp.int32, (n, B_DIM), 0)

    gates = []
    for q, xf, v, lb, states in items:
        f = lb + (1.0 - lb) * _sigmoid(xf)
        g = jnp.log2(f)
        g_hi = g.astype(BF16)
        g_lo = (g - g_hi.astype(F32)).astype(BF16)
        gates.append((f, jnp.concatenate([g_hi, g_lo], axis=0)))
    stage1 = []
    for i in range(0, len(items), 2):
        pair = gates[i:i + 2]
        rhs = pair[0][1] if len(pair) == 1 else jnp.concatenate([pair[0][1], pair[1][1]], axis=1)
        dsum = jnp.dot(w2, rhs, preferred_element_type=F32)
        for k, (f, _) in enumerate(pair):
            stage1.append((f, 1.0 - f, dsum[:, k * B_DIM:(k + 1) * B_DIM]))

    stage2 = []
    for (q, xf, v, lb, states), (f, kin, dsum) in zip(items, stage1):
        fac = jnp.exp2(dsum)
        ea = fac[0:n]
        ops = []
        m = seg
        for li in range(n_lev):
            half = m // 2
            if half % 8 == 0:
                src = jnp.concatenate([(kin if (r0 // half) % 2 == 0 else q)[r0:r0 + half]
                                       for r0 in range(0, n, half)], axis=0)
                tm = src * fac[(2 + li) * n:(3 + li) * n]
            elif m > 2:
                tm = jnp.where((row & half) != 0, q, kin) * fac[(2 + li) * n:(3 + li) * n]
            else:
                tm = jnp.where((row & half) != 0, q * f, kin)
            ops.append(tm.astype(BF16))
            m //= 2
        stage2.append((ea, (q * ea).astype(BF16), kin * fac[n:2 * n], ops))

    stage3 = []
    in_level = [level == li for li in range(n_lev + 1)]
    for (q, xf, v, lb, states), (f, kin, dsum), (ea, qt, kh, ops) in zip(items, stage1, stage2):
        att = jnp.zeros((n, n), F32)
        for li, tm in enumerate(ops):
            att = jnp.where(in_level[li], _nt(tm, tm), att)
        att = jnp.where(in_level[n_lev], _nt(q.astype(BF16), kin.astype(BF16)), att)
        stage3.append(att.astype(BF16))

    results = []
    for (q, xf, v, lb, states), (ea, qt, kh, ops), att in zip(items, stage2, stage3):
        vb = v.astype(BF16)
        o = jnp.dot(att, vb, preferred_element_type=F32)
        new_states = []
        for si in range(n_seg):
            st = states[si]
            o_s = _nt(qt, st.astype(BF16))
            if n_seg == 1:
                o = o + o_s
                kh_s = kh
            else:
                in_seg = (row >= si * seg) & (row < (si + 1) * seg)
                o = o + jnp.where(in_seg, o_s, 0.0)
                kh_s = jnp.where(in_seg, kh, 0.0)
            decay = ea[(si + 1) * seg - 1:(si + 1) * seg, :]
            new_states.append(st * decay + _tn(vb, kh_s.astype(BF16)))
        results.append((o, new_states))
    return results


def _lower_bound(lbl_ref, layer):
    lg = lbl_ref[...]
    e = jnp.exp(lg - jnp.max(lg, axis=0, keepdims=True))
    p = e / jnp.sum(e, axis=0, keepdims=True)
    return jnp.sum(p[0:layer + 1], axis=0, keepdims=True) - p[0:1]


def _hgrn_prompt_kernel(q_ref, f_ref, v_ref, lbl_ref, gn_ref, w3_ref, lev_ref, o_ref, s_ref,
                        st_ref, *, layer, n_lev, tl, n_heads):
    li = pl.program_id(2)

    @pl.when(li == 0)
    def _():
        st_ref[...] = jnp.zeros_like(st_ref)

    lb = _lower_bound(lbl_ref, layer)
    w3 = w3_ref[...]
    level = lev_ref[...]
    gain = gn_ref[...]
    head = [slice(hh * B_DIM, (hh + 1) * B_DIM) for hh in range(n_heads)]

    def body(c, carry):
        r0 = pl.multiple_of(c * HGRN_ROWS, HGRN_ROWS)
        rows = pl.ds(r0, HGRN_ROWS)
        items = [(q_ref[0, rows, hs].astype(F32), f_ref[0, rows, hs].astype(F32),
                  v_ref[0, rows, hs].astype(F32), lb[:, hs], [st_ref[hh]])
                 for hh, hs in enumerate(head)]
        for hh, (o, (st,)) in enumerate(_hgrn_blocks(items, w3, level, n_lev, HGRN_ROWS)):
            st_ref[hh] = st
            o_ref[0, rows, head[hh]] = _rmsnorm_rows(o, gain).astype(o_ref.dtype)
        return carry

    lax.fori_loop(0, tl // HGRN_ROWS, body, 0)

    @pl.when(li == pl.num_programs(2) - 1)
    def _():
        for hh in range(n_heads):
            s_ref[0, hh] = st_ref[hh].T


def _hgrn_prompt(z3, col, lb_logits, hgrn_gain, layer, *, tl):
    b, s, _ = z3.shape
    w3, level, n_lev = _hgrn_consts(HGRN_ROWS)
    nh = HGRN_HEADS_PER_STEP
    wblk = nh * B_DIM

    def cspec(name):
        c0 = col[name] // wblk
        return pl.BlockSpec((1, tl, wblk), lambda bb, h, i: (bb, i, c0 + h))

    const = lambda shape: pl.BlockSpec(shape, lambda bb, h, i: (0,) * len(shape))
    return pl.pallas_call(
        functools.partial(_hgrn_prompt_kernel, layer=layer, n_lev=n_lev, tl=tl, n_heads=nh),
        grid=(b, B_HEADS // nh, s // tl),
        in_specs=[cspec("bq"), cspec("bf"), cspec("bi"),
                  pl.BlockSpec((lb_logits.shape[0], wblk), lambda bb, h, i: (0, h)),
                  const((1, B_DIM)), const(w3.shape), const(level.shape)],
        out_specs=[pl.BlockSpec((1, tl, wblk), lambda bb, h, i: (bb, i, h)),
                   pl.BlockSpec((1, nh, B_DIM, B_DIM), lambda bb, h, i: (bb, h, 0, 0))],
        out_shape=[jax.ShapeDtypeStruct((b, s, B_WIDTH), BF16),
                   jax.ShapeDtypeStruct((b, B_HEADS, B_DIM, B_DIM), F32)],
        scratch_shapes=[pltpu.VMEM((nh, B_DIM, B_DIM), F32)],
        compiler_params=_cparams(("parallel", "parallel", "arbitrary")),
        name="hgrn_prompt",
    )(z3, z3, z3, lb_logits, hgrn_gain.reshape(1, B_DIM), w3, level)


def _hgrn_sample_kernel(q_ref, f_ref, v_ref, s0_ref, lbl_ref, gn_ref, w3_ref, lev_ref, acc_ref,
                        o_ref, s_ref, *, layer, n_lev, t_new, n_heads):
    del acc_ref
    lb = _lower_bound(lbl_ref, layer)
    n_seg = HGRN_ROWS // t_new
    head = [slice(hh * B_DIM, (hh + 1) * B_DIM) for hh in range(n_heads)]
    items = [(q_ref[:, hs], f_ref[:, hs], v_ref[:, hs], lb[:, hs],
              [s0_ref[0, si, hh].T for si in range(n_seg)])
             for hh, hs in enumerate(head)]
    results = _hgrn_blocks(items, w3_ref[...], lev_ref[...], n_lev, t_new)
    for hh, (o, new_states) in enumerate(results):
        o_ref[:, head[hh]] = _rmsnorm_rows(o, gn_ref[...]).astype(o_ref.dtype)
        for si in range(n_seg):
            s_ref[0, si, hh] = new_states[si].T


def _hgrn_sample(z_s, state, new_state, lb_logits, hgrn_gain, layer, *, n_seq, t_new):
    w3, level, n_lev = _hgrn_consts(t_new)
    n_seg = HGRN_ROWS // t_new
    nh = HGRN_SAMPLE_HEADS_PER_STEP
    wblk = nh * B_DIM

    def cspec(name):
        c0 = _COL_S[name] // wblk
        return pl.BlockSpec((HGRN_ROWS, wblk), lambda i, h: (i, c0 + h))

    const = lambda shape: pl.BlockSpec(shape, lambda i, h: (0,) * len(shape))
    return pl.pallas_call(
        functools.partial(_hgrn_sample_kernel, layer=layer, n_lev=n_lev, t_new=t_new, n_heads=nh),
        grid=(n_seq // n_seg, B_HEADS // nh),
        in_specs=[cspec("bq"), cspec("bf"), cspec("bi"),
                  pl.BlockSpec((1, n_seg, nh, B_DIM, B_DIM), lambda i, h, layer=layer: (layer, i, h, 0, 0)),
                  pl.BlockSpec((lb_logits.shape[0], wblk), lambda i, h: (0, h)),
                  const((1, B_DIM)), const(w3.shape), const(level.shape),
                  pl.BlockSpec(memory_space=pl.ANY)],
        out_specs=[pl.BlockSpec((HGRN_ROWS, wblk), lambda i, h: (i, h)),
                   pl.BlockSpec((1, n_seg, nh, B_DIM, B_DIM), lambda i, h, layer=layer: (layer, i, h, 0, 0))],
        out_shape=[jax.ShapeDtypeStruct((n_seq * t_new, B_WIDTH), BF16),
                   jax.ShapeDtypeStruct(new_state.shape, F32)],
        input_output_aliases={8: 1},
        compiler_params=_cparams(("parallel", "parallel")),
        name="hgrn_sample",
    )(z_s, z_s, z_s, state, lb_logits, hgrn_gain.reshape(1, B_DIM), w3, level, new_state)


_POOL_PAD = 16


def _pool_mix(ext_at, u, pos, pw_ref, ps_ref):
    outs = []
    for g, win in enumerate(C_WINDOWS):
        gs = slice(g * C_GROUP_DIM, (g + 1) * C_GROUP_DIM)
        acc = u[:, gs]
        for k in range(1, win):
            acc = acc + ext_at(k, gs)
        cnt = jnp.minimum(pos + 1, win).astype(F32)
        pooled = acc * (1.0 / cnt) - u[:, gs]
        mixed = jnp.dot(pooled.astype(BF16), pw_ref[g].astype(BF16), preferred_element_type=F32)
        outs.append(mixed * ps_ref[:, gs])
    return outs


def _pool_prompt_kernel(u_ref, h_ref, pw_ref, ps_ref, o_ref, ext_ref, *, tl):
    i = pl.program_id(1)
    halo = h_ref[0].astype(F32)
    ext_ref[0:_POOL_PAD, :] = jnp.where(i > 0, halo, 0.0)
    u = u_ref[0].astype(F32)
    ext_ref[_POOL_PAD:, :] = u
    pos = i * tl + lax.broadcasted_iota(jnp.int32, (tl, 1), 0)
    outs = _pool_mix(lambda k, gs: ext_ref[_POOL_PAD - k:_POOL_PAD - k + tl, gs], u, pos, pw_ref, ps_ref)
    for g, og in enumerate(outs):
        o_ref[0, :, g * C_GROUP_DIM:(g + 1) * C_GROUP_DIM] = og.astype(o_ref.dtype)


def _pool_prompt(z3, col, pool_w, pool_scale, *, tl):
    b, s, _ = z3.shape
    c0 = col["cu"] // C_WIDTH
    per = tl // _POOL_PAD
    return pl.pallas_call(
        functools.partial(_pool_prompt_kernel, tl=tl),
        grid=(b, s // tl),
        in_specs=[pl.BlockSpec((1, tl, C_WIDTH), lambda bb, i: (bb, i, c0)),
                  pl.BlockSpec((1, _POOL_PAD, C_WIDTH), lambda bb, i: (bb, jnp.maximum(i * per - 1, 0), c0)),
                  pl.BlockSpec((C_GROUPS, C_GROUP_DIM, C_GROUP_DIM), lambda bb, i: (0, 0, 0)),
                  pl.BlockSpec((1, C_WIDTH), lambda bb, i: (0, 0))],
        out_specs=pl.BlockSpec((1, tl, C_WIDTH), lambda bb, i: (bb, i, 0)),
        out_shape=jax.ShapeDtypeStruct((b, s, C_WIDTH), BF16),
        scratch_shapes=[pltpu.VMEM((_POOL_PAD + tl, C_WIDTH), F32)],
        compiler_params=_cparams(("parallel", "arbitrary")),
        name="pool_prompt",
    )(z3, z3, pool_w, pool_scale.reshape(1, C_WIDTH))


def _pool_sample_kernel(u_ref, h_ref, pw_ref, ps_ref, o_ref, ext_ref, *, nb, t_new, first_pos):
    ext_ref[:, 0:1, :] = jnp.zeros((nb, 1, C_WIDTH), F32)
    ext_ref[:, 1:_POOL_PAD, :] = h_ref[0]
    u3 = u_ref[...].reshape(nb, t_new, C_WIDTH)
    ext_ref[:, _POOL_PAD:, :] = u3
    u = u_ref[...]
    pos = first_pos + lax.broadcasted_iota(jnp.int32, (nb * t_new, 1), 0) % t_new

    def ext_at(k, gs):
        return ext_ref[:, _POOL_PAD - k:_POOL_PAD - k + t_new, gs].reshape(nb * t_new, C_GROUP_DIM)

    outs = _pool_mix(ext_at, u, pos, pw_ref, ps_ref)
    for g, og in enumerate(outs):
        o_ref[:, g * C_GROUP_DIM:(g + 1) * C_GROUP_DIM] = og.astype(o_ref.dtype)


def _pool_sample(z_s, state_pool, pool_w, pool_scale, layer, *, n_seq, t_new, first_pos, nb):
    c0 = _COL_S["cu"] // C_WIDTH
    return pl.pallas_call(
        functools.partial(_pool_sample_kernel, nb=nb, t_new=t_new, first_pos=first_pos),
        grid=(n_seq // nb,),
        in_specs=[pl.BlockSpec((nb * t_new, C_WIDTH), lambda i: (i, c0)),
                  pl.BlockSpec((1, nb, C_HIST, C_WIDTH), lambda i, layer=layer: (layer, i, 0, 0)),
                  pl.BlockSpec((C_GROUPS, C_GROUP_DIM, C_GROUP_DIM), lambda i: (0, 0, 0)),
                  pl.BlockSpec((1, C_WIDTH), lambda i: (0, 0))],
        out_specs=pl.BlockSpec((nb * t_new, C_WIDTH), lambda i: (i, 0)),
        out_shape=jax.ShapeDtypeStruct((n_seq * t_new, C_WIDTH), BF16),
        scratch_shapes=[pltpu.VMEM((nb, _POOL_PAD + t_new, C_WIDTH), F32)],
        compiler_params=_cparams(("parallel",)),
        name="pool_sample",
    )(z_s, state_pool, pool_w, pool_scale.reshape(1, C_WIDTH))


def _token_order(ref, scr_ref, d, tm):
    n_chunk = ref.shape[-1] // LANE
    if d == 1:
        return [ref[0, 0, :, c * LANE:(c + 1) * LANE] for c in range(n_chunk)]
    for r in range(d):
        for c in range(n_chunk):
            scr_ref[c, pl.ds(r, tm // d, stride=d), :] = ref[0, r, :, c * LANE:(c + 1) * LANE]
    return [scr_ref[c] for c in range(n_chunk)]


def _out_proj_kernel(x_ref, ag_ref, bg_ref, cg_ref, m0_ref, m1_ref, m2_ref, o0_ref, o1_ref, o2_ref,
                     l0_ref, l1_ref, l2_ref, ob_ref, oc_ref, wa_ref, wb_ref, wc_ref, wo_ref,
                     gp_ref, y_ref, *scratch, dil, tm):
    scr = iter(scratch)
    o_g, l_g = [], []
    for o_ref, l_ref, d in zip((o0_ref, o1_ref, o2_ref), (l0_ref, l1_ref, l2_ref), dil):
        so = next(scr) if d > 1 else None
        sl = next(scr) if d > 1 else None
        o_g.append(_token_order(o_ref, so, d, tm))
        l_g.append(_token_order(l_ref, sl, d, tm)[0])
    heads = []
    for h in range(A_HEADS):
        l0, l1, l2 = (l[:, h:h + 1] for l in l_g)
        mx = jnp.maximum(jnp.maximum(l0, l1), l2)
        e0, e1, e2 = jnp.exp(l0 - mx), jnp.exp(l1 - mx), jnp.exp(l2 - mx)
        inv = 1.0 / (e0 + e1 + e2)
        heads.append((e0 * inv) * o_g[0][h] + (e1 * inv) * o_g[1][h] + (e2 * inv) * o_g[2][h])
    o_a = jnp.concatenate(heads, axis=1)

    def silu_gated(o, g_ref):
        h = g_ref[...].astype(F32) * 0.5
        return (o.astype(F32) * (h + h * jnp.tanh(h))).astype(BF16)

    def merge_gated(m_ref, y, w_ref):
        p = jnp.dot(y, w_ref[...], preferred_element_type=F32)
        return p + jnp.tanh(m_ref[...].astype(F32) * 0.5) * p

    merged = (merge_gated(m0_ref, silu_gated(o_a, ag_ref), wa_ref)
              + merge_gated(m1_ref, silu_gated(ob_ref[...], bg_ref), wb_ref)
              + merge_gated(m2_ref, silu_gated(oc_ref[...], cg_ref), wc_ref)) * 0.5
    out = jnp.dot(merged.astype(BF16), wo_ref[...], preferred_element_type=F32)
    y_ref[...] = x_ref[...] + _rmsnorm_rows(out, gp_ref[...])


def _out_proj(x, z, col, o_g, lse_g, o_b, o_c, wa, wb, wc, wo, gain_post, *, tm):
    n = x.shape[0]
    dil = tuple(int(a.shape[1]) for a in o_g)
    tpb = (n // o_g[0].shape[0]) // tm

    def zcol(name, width, k=0):
        assert col[name] % width == 0
        c0 = col[name] // width + k
        return pl.BlockSpec((tm, width), lambda i: (i, c0))

    def gspec(d, width):
        return pl.BlockSpec((1, d, tm // d, width), lambda i: (i // tpb, 0, i % tpb, 0))

    rows = lambda width: pl.BlockSpec((tm, width), lambda i: (i, 0))
    const = lambda a: pl.BlockSpec(a.shape, lambda i: (0,) * a.ndim)
    gp = gain_post.reshape(1, D_MODEL)
    scratch = []
    for d in dil:
        if d > 1:
            scratch += [pltpu.VMEM((A_WIDTH // LANE, tm, LANE), F32), pltpu.VMEM((1, tm, LANE), F32)]
    return pl.pallas_call(
        functools.partial(_out_proj_kernel, dil=dil, tm=tm),
        grid=(n // tm,),
        in_specs=[rows(D_MODEL), zcol("ag", A_WIDTH), zcol("bg", B_WIDTH), zcol("cg", C_WIDTH),
                  zcol("mg", D_MODEL, 0), zcol("mg", D_MODEL, 1), zcol("mg", D_MODEL, 2)]
                 + [gspec(d, A_WIDTH) for d in dil] + [gspec(d, LANE) for d in dil]
                 + [rows(B_WIDTH), rows(C_WIDTH)]
                 + [const(wa), const(wb), const(wc), const(wo), const(gp)],
        out_specs=rows(D_MODEL),
        out_shape=jax.ShapeDtypeStruct((n, D_MODEL), F32),
        scratch_shapes=scratch,
        compiler_params=_cparams(("parallel",)),
        name="out_proj",
    )(x, z, z, z, z, z, z, *o_g, *lse_g, o_b, o_c, wa, wb, wc, wo, gp)


def _ref_cols(w, name, g=None):
    off = _REF_OFF[name]
    if g is None:
        return w[:, off:off + _WIDTHS[name]]
    return w[:, off + g * A_WIDTH:off + (g + 1) * A_WIDTH]


def _prep_w_in(w):
    parts = []
    for g in range(1, A_GROUPS):
        parts += [_ref_cols(w, "ak", g), _ref_cols(w, "av", g), _ref_cols(w, "aq", g)]
    parts += [_ref_cols(w, "ak", 0), _ref_cols(w, "av", 0), _ref_cols(w, "cu"), _ref_cols(w, "cg"),
              _ref_cols(w, "aq", 0)] + [_ref_cols(w, k) for k in ("ag", "bq", "bf", "bi", "bg", "mg")]
    return jnp.concatenate(parts, axis=1).astype(BF16)


def _kv_split(kv):
    return kv.reshape(kv.shape[:-1] + (2, A_HEADS, A_HEAD_DIM))


def kernel(x_prompt, x_sample, cache_kv_w128, cache_kv_w512, cache_kv_w2048, state_hgrn, state_pool,
           norm_pre, norm_post, w_in, hgrn_lb_logits, hgrn_norm, pool_w, pool_scale,
           w_br_a, w_br_b, w_br_c, w_out):
    depth = w_in.shape[0]
    b_p, s_p, _ = x_prompt.shape
    n_seq, t_new, _ = x_sample.shape
    caches = (cache_kv_w128, cache_kv_w512, cache_kv_w2048)
    past_len = cache_kv_w2048.shape[2]
    assert HGRN_ROWS % t_new == 0 and n_seq % (HGRN_ROWS // t_new) == 0
    n_p, n_s = b_p * s_p, n_seq * t_new

    hp = x_prompt.reshape(n_p, D_MODEL)
    hs = x_sample.reshape(n_s, D_MODEL)
    kv_p = [[] for _ in range(A_GROUPS)]
    kv_s = [[] for _ in range(A_GROUPS)]
    hg_p, pl_p, pl_s = [], [], []
    hg_s = jnp.zeros(state_hgrn.shape, F32)
    tm_p = min(1024, s_p)
    for l in range(depth):
        w_l = _prep_w_in(w_in[l])
        wa, wb, wc, wo = (w[l].astype(BF16) for w in (w_br_a, w_br_b, w_br_c, w_out))

        (z_p,) = _in_proj(hp, norm_pre[l], w_l, col0=_P_COL0, tm=min(2048, s_p), tn=512,
                          z_dtype=BF16, f32_cols=0)
        z3 = z_p.reshape(b_p, s_p, _P_WIDTH)
        hp3 = hp.reshape(b_p, s_p, D_MODEL)

        def tail_f32(rows, col0, width):
            x_t = hp3[:, s_p - rows:].reshape(b_p * rows, D_MODEL)
            (out,) = _in_proj(x_t, norm_pre[l], w_l, col0=col0, width=width, tm=min(1024, b_p * rows),
                              tn=A_WIDTH, z_dtype=F32, f32_cols=0)
            return out.reshape(b_p, rows, width)

        o_g, lse_g = [], []
        for g, (win, d) in enumerate(A_PATTERNS):
            rows = min(win, s_p)
            if g == 0:
                arr = z_p.reshape(b_p, 1, s_p, _P_WIDTH)
                cols = (_COL_P["aq"] // A_WIDTH, _COL_P["ak"] // A_WIDTH, _COL_P["av"] // A_WIDTH)
                t0 = tail_f32(max(rows, _POOL_PAD), _P_COL0, 3 * A_WIDTH)
                kv_p[g].append(_kv_split(t0[:, t0.shape[1] - rows:, 0:2 * A_WIDTH]))
                pl_p.append(t0[:, t0.shape[1] - C_HIST:, _COL_P["cu"]:_COL_P["cu"] + C_WIDTH])
            else:
                arr = _in_proj_perm(hp, norm_pre[l], w_l, g, b=b_p, s=s_p, d=d, tm=tm_p)
                cols = (2, 0, 1)
                kv_p[g].append(_kv_split(tail_f32(rows, (g - 1) * _PERM_WIDTH, 2 * A_WIDTH)))
            o, lse = _attn_prompt(arr, cols, tq=min(512, s_p // d))
            o_g.append(o)
            lse_g.append(lse)
        o_b, s_new = _hgrn_prompt(z3, _COL_P, hgrn_lb_logits, hgrn_norm[l], l, tl=min(1024, s_p))
        hg_p.append(s_new)
        o_c = _pool_prompt(z3, _COL_P, pool_w[l], pool_scale[l], tl=min(1024, s_p))
        hp = _out_proj(hp, z_p, _COL_P, o_g, lse_g, o_b.reshape(n_p, B_WIDTH),
                       o_c.reshape(n_p, C_WIDTH), wa, wb, wc, wo, norm_post[l], tm=min(512, s_p))

        (z_s,) = _in_proj(hs, norm_pre[l], w_l, col0=0, tm=min(1024, n_s), tn=1024, z_dtype=F32,
                          f32_cols=0)
        o_g, lse_g = _attn_sample(z_s, caches, l, n_seq=n_seq, t_new=t_new)
        zs3 = z_s.reshape(n_seq, t_new, IN_WIDTH)
        for g in range(A_GROUPS):
            k_off, v_off, _ = _attn_cols(g)
            kv_s[g].append(_kv_split(zs3[:, :, k_off:v_off + A_WIDTH]))
        o_b, hg_s = _hgrn_sample(z_s, state_hgrn, hg_s, hgrn_lb_logits, hgrn_norm[l], l,
                                 n_seq=n_seq, t_new=t_new)
        o_c = _pool_sample(z_s, state_pool, pool_w[l], pool_scale[l], l, n_seq=n_seq,
                           t_new=t_new, first_pos=past_len, nb=HGRN_ROWS // t_new)
        u_new = zs3[:, :, _COL_S["cu"]:_COL_S["cu"] + C_WIDTH]
        pl_s.append(jnp.concatenate([state_pool[l], u_new], axis=1)[:, -C_HIST:])
        as4 = lambda a: a.reshape(1, 1, n_s, a.shape[-1])
        hs = _out_proj(hs, z_s, _COL_S, [as4(a) for a in o_g], [as4(a) for a in lse_g], o_b, o_c,
                       wa, wb, wc, wo, norm_post[l], tm=min(256, n_s))

    return (hp.reshape(b_p, s_p, D_MODEL), hs.reshape(n_seq, t_new, D_MODEL),
            jnp.stack(kv_p[0]), jnp.stack(kv_p[1]), jnp.stack(kv_p[2]),
            jnp.stack(kv_s[0]), jnp.stack(kv_s[1]), jnp.stack(kv_s[2]),
            jnp.stack(hg_p), hg_s, jnp.stack(pl_p), jnp.stack(pl_s))
```

```python
import functools

import numpy as np
import jax
import jax.numpy as jnp
from jax import lax
from jax.experimental import pallas as pl
from jax.experimental.pallas import tpu as pltpu

F32 = jnp.float32
BF16 = jnp.bfloat16

D_MODEL = 1024
A_PATTERNS = ((128, 1), (512, 4), (2048, 16))
A_GROUPS = 3
A_HEADS = 4
A_HEAD_DIM = 128
A_WIDTH = A_HEADS * A_HEAD_DIM
N_STEPS = 128
B_HEADS = 8
B_DIM = 128
B_WIDTH = B_HEADS * B_DIM
C_WINDOWS = (2, 4, 8, 16)
C_GROUPS = 4
C_GROUP_DIM = 128
C_WIDTH = C_GROUPS * C_GROUP_DIM
C_HIST = 15
N_BRANCH = 3
EPS = 1e-6
NEG = -1e30
IN_WIDTH = 13312
LANE = 128
HGRN_ROWS = 128
HGRN_HEADS_PER_STEP = 8
HGRN_SAMPLE_HEADS_PER_STEP = 4

_REF_OFF = dict(aq=0, ak=1536, av=3072, ag=4608, bq=5120, bf=6144, bi=7168, bg=8192,
                cu=9216, cg=9728, mg=10240)
_WIDTHS = dict(aq=1536, ak=1536, av=1536, ag=512, bq=1024, bf=1024, bi=1024, bg=1024,
               cu=512, cg=512, mg=3072)

_PERM_WIDTH = 3 * A_WIDTH
_P_COL0 = (A_GROUPS - 1) * _PERM_WIDTH
_COL_P = dict(ak=0, av=512, cu=1024, cg=1536, aq=2048, ag=2560, bq=3072, bf=4096,
              bi=5120, bg=6144, mg=7168)
_P_WIDTH = IN_WIDTH - _P_COL0
_COL_S = {name: _P_COL0 + off for name, off in _COL_P.items()}


def _attn_cols(g):
    if g == 0:
        return _COL_S["ak"], _COL_S["av"], _COL_S["aq"]
    base = (g - 1) * _PERM_WIDTH
    return base, base + A_WIDTH, base + 2 * A_WIDTH

_VMEM_LIMIT = 56 * 1024 * 1024


def _cparams(sem):
    return pltpu.CompilerParams(dimension_semantics=sem, vmem_limit_bytes=_VMEM_LIMIT)


def _nt(a, b):
    return lax.dot_general(a, b, (((1,), (1,)), ((), ())), preferred_element_type=F32)


def _tn(a, b):
    return lax.dot_general(a, b, (((0,), (0,)), ((), ())), preferred_element_type=F32)


def _sigmoid(x):
    return 1.0 / (1.0 + jnp.exp(-x))


def _rmsnorm_rows(x, gain):
    return x * lax.rsqrt(jnp.mean(x * x, axis=-1, keepdims=True) + EPS) * gain


def _in_proj_kernel(x_ref, g_ref, w_ref, *refs, n_f32_tiles, z_dtype):
    if n_f32_tiles:
        z_ref, zf_ref, xn_ref = refs
    else:
        z_ref, xn_ref = refs
    j = pl.program_id(1)

    @pl.when(j == 0)
    def _():
        xn_ref[...] = _rmsnorm_rows(x_ref[...], g_ref[...]).astype(BF16)

    acc = jnp.dot(xn_ref[...], w_ref[...], preferred_element_type=F32)
    z_ref[...] = acc.astype(z_dtype)
    if n_f32_tiles:
        @pl.when(j < n_f32_tiles)
        def _():
            zf_ref[...] = acc


def _in_proj(x, gain, w_bf16, *, col0, tm, tn, z_dtype, f32_cols, width=None):
    n = x.shape[0]
    width = w_bf16.shape[1] - col0 if width is None else width
    j0 = col0 // tn
    n_f32_tiles = f32_cols // tn
    out_shape = [jax.ShapeDtypeStruct((n, width), z_dtype)]
    out_specs = [pl.BlockSpec((tm, tn), lambda i, j: (i, j))]
    if n_f32_tiles:
        out_shape.append(jax.ShapeDtypeStruct((n, f32_cols), F32))
        out_specs.append(pl.BlockSpec((tm, tn), lambda i, j: (i, jnp.minimum(j, n_f32_tiles - 1))))
    return pl.pallas_call(
        functools.partial(_in_proj_kernel, n_f32_tiles=n_f32_tiles, z_dtype=z_dtype),
        grid=(n // tm, width // tn),
        in_specs=[pl.BlockSpec((tm, D_MODEL), lambda i, j: (i, 0)),
                  pl.BlockSpec((1, D_MODEL), lambda i, j: (0, 0)),
                  pl.BlockSpec((D_MODEL, tn), lambda i, j: (0, j + j0))],
        out_specs=out_specs,
        out_shape=out_shape,
        scratch_shapes=[pltpu.VMEM((tm, D_MODEL), BF16)],
        compiler_params=_cparams(("parallel", "arbitrary")),
        name="in_proj",
    )(x, gain.reshape(1, D_MODEL), w_bf16)


def _kv_rows_kernel(x_ref, g_ref, wk_ref, wv_ref, o_ref, *, tm):
    xn = _rmsnorm_rows(x_ref[...], g_ref[...]).astype(BF16)
    per_tok = 2 * A_HEADS
    for kv, w_ref in enumerate((wk_ref, wv_ref)):
        acc = jnp.dot(xn, w_ref[...], preferred_element_type=F32)
        for h in range(A_HEADS):
            o_ref[pl.ds(kv * A_HEADS + h, tm, stride=per_tok), :] = (
                acc[:, h * A_HEAD_DIM:(h + 1) * A_HEAD_DIM])


def _kv_rows(x, gain, w_bf16, k_col, *, tm):
    n = x.shape[0]
    per_tok = 2 * A_HEADS
    kb = k_col // A_WIDTH
    return pl.pallas_call(
        functools.partial(_kv_rows_kernel, tm=tm),
        grid=(n // tm,),
        in_specs=[pl.BlockSpec((tm, D_MODEL), lambda i: (i, 0)),
                  pl.BlockSpec((1, D_MODEL), lambda i: (0, 0)),
                  pl.BlockSpec((D_MODEL, A_WIDTH), lambda i: (0, kb)),
                  pl.BlockSpec((D_MODEL, A_WIDTH), lambda i: (0, kb + 1))],
        out_specs=pl.BlockSpec((tm * per_tok, A_HEAD_DIM), lambda i: (i, 0)),
        out_shape=jax.ShapeDtypeStruct((n * per_tok, A_HEAD_DIM), F32),
        compiler_params=_cparams(("parallel",)),
        name="kv_rows",
    )(x, gain.reshape(1, D_MODEL), w_bf16, w_bf16)


def _in_proj_perm_kernel(x_ref, g_ref, w_ref, r_ref, xs_ref, xn_ref, *, d, tm):
    xn = _rmsnorm_rows(x_ref[...], g_ref[...])
    n_chunk = D_MODEL // LANE
    rows = tm // d
    for c in range(n_chunk):
        xs_ref[c] = xn[:, c * LANE:(c + 1) * LANE]
    for r in range(d):
        for c in range(n_chunk):
            xn_ref[r * rows:(r + 1) * rows, c * LANE:(c + 1) * LANE] = (
                xs_ref[c, pl.ds(r, rows, stride=d), :].astype(BF16))
    acc = jnp.dot(xn_ref[...], w_ref[...], preferred_element_type=F32)
    r_ref[0] = acc.astype(BF16).reshape(d, rows, _PERM_WIDTH)


def _in_proj_perm(x, gain, w_bf16, g, *, b, s, d, tm):
    tpb = s // tm
    rows = tm // d
    idx = lambda i: (i // tpb, 0, i % tpb, 0)
    return pl.pallas_call(
        functools.partial(_in_proj_perm_kernel, d=d, tm=tm),
        grid=(b * tpb,),
        in_specs=[pl.BlockSpec((tm, D_MODEL), lambda i: (i, 0)),
                  pl.BlockSpec((1, D_MODEL), lambda i: (0, 0)),
                  pl.BlockSpec((D_MODEL, _PERM_WIDTH), lambda i: (0, g - 1))],
        out_specs=pl.BlockSpec((1, d, rows, _PERM_WIDTH), idx),
        out_shape=jax.ShapeDtypeStruct((b, d, s // d, _PERM_WIDTH), BF16),
        scratch_shapes=[pltpu.VMEM((D_MODEL // LANE, tm, LANE), F32),
                        pltpu.VMEM((tm, D_MODEL), BF16)],
        compiler_params=_cparams(("parallel",)),
        name="in_proj_d%d" % d,
    )(x, gain.reshape(1, D_MODEL), w_bf16)


def _attn_prompt_kernel(q_ref, kp_ref, kc_ref, vp_ref, vc_ref, o_ref, lse_ref, *, tq):
    i = pl.program_id(2)
    n = N_STEPS
    qi = lax.broadcasted_iota(jnp.int32, (n, 2 * n), 0)
    kj = lax.broadcasted_iota(jnp.int32, (n, 2 * n), 1)
    band = (kj >= qi) & (kj <= qi + n)
    band_first = band & ((i > 0) | (kj >= n))
    scale = A_HEAD_DIM ** -0.5
    lane = lax.broadcasted_iota(jnp.int32, (n, LANE), 1)
    for j in range(tq // n):
        rows = slice(j * n, (j + 1) * n)
        lse = jnp.zeros((n, LANE), F32)
        for h in range(A_HEADS):
            hs = slice(h * A_HEAD_DIM, (h + 1) * A_HEAD_DIM)
            q = q_ref[0, 0, rows, hs]
            if j == 0:
                k = jnp.concatenate([kp_ref[0, 0, :, hs], kc_ref[0, 0, rows, hs]], axis=0)
                v = jnp.concatenate([vp_ref[0, 0, :, hs], vc_ref[0, 0, rows, hs]], axis=0)
                valid = band_first
            else:
                k = kc_ref[0, 0, (j - 1) * n:(j + 1) * n, hs]
                v = vc_ref[0, 0, (j - 1) * n:(j + 1) * n, hs]
                valid = band
            s = jnp.where(valid, _nt(q, k) * scale, NEG)
            m = jnp.max(s, axis=-1, keepdims=True)
            p = jnp.exp(s - m)
            l = jnp.sum(p, axis=-1, keepdims=True)
            o = jnp.dot(p.astype(BF16), v, preferred_element_type=F32)
            o_ref[0, 0, rows, hs] = o / l
            lse = jnp.where(lane == h, m + jnp.log(l), lse)
        lse_ref[0, 0, rows, :] = lse


def _attn_prompt(arr, cols, *, tq):
    b, d, rows, _ = arr.shape
    qc, kc, vc = cols
    prev_per_blk = tq // N_STEPS

    def cur(c):
        return pl.BlockSpec((1, 1, tq, A_WIDTH), lambda bb, r, i: (bb, r, i, c))

    def prev(c):
        return pl.BlockSpec((1, 1, N_STEPS, A_WIDTH),
                            lambda bb, r, i: (bb, r, jnp.maximum(i * prev_per_blk - 1, 0), c))

    out_spec = pl.BlockSpec((1, 1, tq, A_WIDTH), lambda bb, r, i: (bb, r, i, 0))
    return pl.pallas_call(
        functools.partial(_attn_prompt_kernel, tq=tq),
        grid=(b, d, rows // tq),
        in_specs=[cur(qc), prev(kc), cur(kc), prev(vc), cur(vc)],
        out_specs=[out_spec, pl.BlockSpec((1, 1, tq, LANE), lambda bb, r, i: (bb, r, i, 0))],
        out_shape=[jax.ShapeDtypeStruct((b, d, rows, A_WIDTH), F32),
                   jax.ShapeDtypeStruct((b, d, rows, LANE), F32)],
        compiler_params=_cparams(("parallel", "parallel", "arbitrary")),
        name="attn_prompt_d%d" % d,
    )(arr, arr, arr, arr, arr)


def _gather_pitch(n_j):
    p = -(-n_j // 4)
    return 4 * (p if p % 2 else p + 1)


def _attn_sample_kernel(z_ref, c0_ref, c1_ref, c2_ref, *refs, t_new):
    out_refs, flat_refs = refs[:2 * A_GROUPS], refs[2 * A_GROUPS:]
    scale = A_HEAD_DIM ** -0.5
    nrow = A_HEADS * t_new
    per_res = 2 * A_HEADS
    for g, c_ref in enumerate((c0_ref, c1_ref, c2_ref)):
        d = A_PATTERNS[g][1]
        n_res = min(d, t_new)
        o_ref, lse_ref = out_refs[2 * g], out_refs[2 * g + 1]
        k_off, v_off, q_off = _attn_cols(g)
        q = z_ref[:, q_off:q_off + A_WIDTH]
        kn = z_ref[:, k_off:k_off + A_WIDTH]
        vn = z_ref[:, v_off:v_off + A_WIDTH]
        n_j = n_res * per_res
        pitch = _gather_pitch(n_j)
        flat_ref = flat_refs[g]
        for mm in range(N_STEPS):
            flat_ref[mm * pitch:mm * pitch + n_j, :] = c_ref[mm]
        gather = lambda j: flat_ref[pl.ds(j, N_STEPS, stride=pitch), :]
        kcat = jnp.concatenate([gather(r * per_res + h)
                                for r in range(n_res) for h in range(A_HEADS)], axis=1).astype(BF16)
        vcat = jnp.concatenate([gather(r * per_res + A_HEADS + h)
                                for r in range(n_res) for h in range(A_HEADS)], axis=1).astype(BF16)
        wide = n_res * A_WIDTH
        q4 = jnp.concatenate([q] * A_HEADS, axis=0)
        row = lax.broadcasted_iota(jnp.int32, (nrow, wide), 0)
        col = lax.broadcasted_iota(jnp.int32, (nrow, wide), 1)
        row_h, row_t = row // t_new, row % t_new
        sel = ((col // A_WIDTH) == (row_t % d)) & (((col % A_WIDTH) // A_HEAD_DIM) == row_h)
        qbd = jnp.where(sel, jnp.concatenate([q4] * n_res, axis=1), 0.0).astype(BF16)
        s_c = _nt(qbd, kcat) * scale
        r2 = lax.broadcasted_iota(jnp.int32, (nrow, N_STEPS), 0) % t_new
        c2 = lax.broadcasted_iota(jnp.int32, (nrow, N_STEPS), 1)
        s_c = jnp.where(c2 >= r2 // d, s_c, NEG)
        pad = jnp.zeros((N_STEPS - t_new, A_WIDTH), F32)
        knp = jnp.concatenate([kn, pad], axis=0).astype(BF16)
        vnp = jnp.concatenate([vn, pad], axis=0).astype(BF16)
        rowq = lax.broadcasted_iota(jnp.int32, (nrow, A_WIDTH), 0) // t_new
        colq = lax.broadcasted_iota(jnp.int32, (nrow, A_WIDTH), 1) // A_HEAD_DIM
        selq = rowq == colq
        qbd2 = jnp.where(selq, q4, 0.0).astype(BF16)
        s_n = _nt(qbd2, knp) * scale
        s_n = jnp.where((c2 <= r2) & ((r2 - c2) % d == 0), s_n, NEG)
        m = jnp.maximum(jnp.max(s_c, axis=-1, keepdims=True), jnp.max(s_n, axis=-1, keepdims=True))
        p_c = jnp.exp(s_c - m)
        p_n = jnp.exp(s_n - m)
        l = jnp.sum(p_c, axis=-1, keepdims=True) + jnp.sum(p_n, axis=-1, keepdims=True)
        o_wide = jnp.where(sel, jnp.dot(p_c.astype(BF16), vcat, preferred_element_type=F32), 0.0)
        o_new = jnp.where(selq, jnp.dot(p_n.astype(BF16), vnp, preferred_element_type=F32), 0.0)
        o = o_new[:, 0:A_HEAD_DIM]
        for c in range(1, A_HEADS):
            o = o + o_new[:, c * A_HEAD_DIM:(c + 1) * A_HEAD_DIM]
        for c in range(wide // A_HEAD_DIM):
            o = o + o_wide[:, c * A_HEAD_DIM:(c + 1) * A_HEAD_DIM]
        o = o / l
        lse = m + jnp.log(l)
        lane = lax.broadcasted_iota(jnp.int32, (t_new, LANE), 1)
        lse_out = jnp.zeros((t_new, LANE), F32)
        for h in range(A_HEADS):
            hs = slice(h * A_HEAD_DIM, (h + 1) * A_HEAD_DIM)
            o_ref[:, hs] = o[h * t_new:(h + 1) * t_new]
            lse_out = jnp.where(lane == h, lse[h * t_new:(h + 1) * t_new], lse_out)
        lse_ref[...] = lse_out


def _attn_sample(z_s, caches, layer, *, n_seq, t_new):
    in_specs = [pl.BlockSpec((t_new, IN_WIDTH), lambda b: (b, 0))]
    views, scratch = [], []
    per_res = 2 * A_HEADS
    for g, c in enumerate(caches):
        win, d = A_PATTERNS[g]
        assert c.shape[2] == win == N_STEPS * d, "cache must hold one full window"
        n_res = min(d, t_new)
        views.append(c.reshape(c.shape[0], n_seq, N_STEPS, d * per_res, A_HEAD_DIM))
        in_specs.append(pl.BlockSpec((None, None, N_STEPS, n_res * per_res, A_HEAD_DIM),
                                     lambda b, layer=layer: (layer, b, 0, 0, 0)))
        scratch.append(pltpu.VMEM((N_STEPS * _gather_pitch(n_res * per_res), A_HEAD_DIM), F32))
    out_specs = [pl.BlockSpec((t_new, A_WIDTH), lambda b: (b, 0)),
                 pl.BlockSpec((t_new, LANE), lambda b: (b, 0))] * A_GROUPS
    out_shape = [jax.ShapeDtypeStruct((n_seq * t_new, A_WIDTH), F32),
                 jax.ShapeDtypeStruct((n_seq * t_new, LANE), F32)] * A_GROUPS
    outs = pl.pallas_call(
        functools.partial(_attn_sample_kernel, t_new=t_new),
        grid=(n_seq,),
        in_specs=in_specs,
        out_specs=out_specs,
        out_shape=out_shape,
        scratch_shapes=scratch,
        compiler_params=_cparams(("parallel",)),
        name="attn_sample",
    )(z_s, *views)
    return outs[0::2], outs[1::2]


def _hgrn_consts(seg):
    n = HGRN_ROWS
    t = np.arange(n)[:, None]
    u = np.arange(n)[None, :]
    seg_lo = (t // seg) * seg
    seg_hi = seg_lo + seg - 1
    mats = [(u >= seg_lo) & (u <= t),
            (u > t) & (u <= seg_hi)]
    level = np.full((n, n), -1, np.int32)
    s = u
    li = 0
    m = seg
    while m >= 2:
        half = m // 2
        upper = (t % m) >= half
        ref = (t // m) * m + half - 1
        if m > 2:
            mats.append(np.where(upper, (u > ref) & (u <= t), (u > t) & (u <= ref)))
        pair = (t // m == s // m) & ((t % m) >= half) & ((s % m) < half)
        level[pair] = li
        li += 1
        m //= 2
    level[np.arange(n), np.arange(n)] = li
    w = np.concatenate(mats, axis=0).astype(np.float32)
    w2 = np.concatenate([w, w], axis=1)
    return jnp.asarray(w2, BF16), jnp.asarray(level), li


def _hgrn_blocks(items, w2, level, n_lev, seg):
    n = HGRN_ROWS
    n_seg = n // seg
    row = lax.broadcasted_iota(jnp.int32, (n, B_DIM), 0)

    gates = []
    for q, xf, v, lb, states in items:
        f = lb + (1.0 - lb) * _sigmoid(xf)
        g = jnp.log2(f)
        g_hi = g.astype(BF16)
        g_lo = (g - g_hi.astype(F32)).astype(BF16)
        gates.append((f, jnp.concatenate([g_hi, g_lo], axis=0)))
    stage1 = []
    for i in range(0, len(items), 2):
        pair = gates[i:i + 2]
        rhs = pair[0][1] if len(pair) == 1 else jnp.concatenate([pair[0][1], pair[1][1]], axis=1)
        dsum = jnp.dot(w2, rhs, preferred_element_type=F32)
        for k, (f, _) in enumerate(pair):
            stage1.append((f, 1.0 - f, dsum[:, k * B_DIM:(k + 1) * B_DIM]))

    stage2 = []
    for (q, xf, v, lb, states), (f, kin, dsum) in zip(items, stage1):
        fac = jnp.exp2(dsum)
        ea = fac[0:n]
        ops = []
        m = seg
        for li in range(n_lev):
            half = m // 2
            if half % 8 == 0:
                src = jnp.concatenate([(kin if (r0 // half) % 2 == 0 else q)[r0:r0 + half]
                                       for r0 in range(0, n, half)], axis=0)
                tm = src * fac[(2 + li) * n:(3 + li) * n]
            elif m > 2:
                tm = jnp.where((row & half) != 0, q, kin) * fac[(2 + li) * n:(3 + li) * n]
            else:
                tm = jnp.where((row & half) != 0, q * f, kin)
            ops.append(tm.astype(BF16))
            m //= 2
        stage2.append((ea, (q * ea).astype(BF16), kin * fac[n:2 * n], ops))

    stage3 = []
    in_level = [level == li for li in range(n_lev + 1)]
    for (q, xf, v, lb, states), (f, kin, dsum), (ea, qt, kh, ops) in zip(items, stage1, stage2):
        att = jnp.zeros((n, n), F32)
        for li, tm in enumerate(ops):
            att = jnp.where(in_level[li], _nt(tm, tm), att)
        att = jnp.where(in_level[n_lev], _nt(q.astype(BF16), kin.astype(BF16)), att)
        stage3.append(att.astype(BF16))

    results = []
    for (q, xf, v, lb, states), (ea, qt, kh, ops), att in zip(items, stage2, stage3):
        vb = v.astype(BF16)
        o = jnp.dot(att, vb, preferred_element_type=F32)
        new_states = []
        for si in range(n_seg):
            st = states[si]
            o_s = _nt(qt, st.astype(BF16))
            if n_seg == 1:
                o = o + o_s
                kh_s = kh
            else:
                in_seg = (row >= si * seg) & (row < (si + 1) * seg)
                o = o + jnp.where(in_seg, o_s, 0.0)
                kh_s = jnp.where(in_seg, kh, 0.0)
            decay = ea[(si + 1) * seg - 1:(si + 1) * seg, :]
            new_states.append(st * decay + _tn(vb, kh_s.astype(BF16)))
        results.append((o, new_states))
    return results


def _lower_bound(lbl_ref, layer):
    lg = lbl_ref[...]
    e = jnp.exp(lg - jnp.max(lg, axis=0, keepdims=True))
    p = e / jnp.sum(e, axis=0, keepdims=True)
    return jnp.sum(p[0:layer + 1], axis=0, keepdims=True) - p[0:1]


def _hgrn_prompt_kernel(q_ref, f_ref, v_ref, lbl_ref, gn_ref, w3_ref, lev_ref, o_ref, s_ref,
                        st_ref, *, layer, n_lev, tl, n_heads):
    li = pl.program_id(2)

    @pl.when(li == 0)
    def _():
        st_ref[...] = jnp.zeros_like(st_ref)

    lb = _lower_bound(lbl_ref, layer)
    w3 = w3_ref[...]
    level = lev_ref[...]
    gain = gn_ref[...]
    head = [slice(hh * B_DIM, (hh + 1) * B_DIM) for hh in range(n_heads)]

    def body(c, carry):
        r0 = pl.multiple_of(c * HGRN_ROWS, HGRN_ROWS)
        rows = pl.ds(r0, HGRN_ROWS)
        items = [(q_ref[0, rows, hs].astype(F32), f_ref[0, rows, hs].astype(F32),
                  v_ref[0, rows, hs].astype(F32), lb[:, hs], [st_ref[hh]])
                 for hh, hs in enumerate(head)]
        for hh, (o, (st,)) in enumerate(_hgrn_blocks(items, w3, level, n_lev, HGRN_ROWS)):
            st_ref[hh] = st
            o_ref[0, rows, head[hh]] = _rmsnorm_rows(o, gain).astype(o_ref.dtype)
        return carry

    lax.fori_loop(0, tl // HGRN_ROWS, body, 0)

    @pl.when(li == pl.num_programs(2) - 1)
    def _():
        for hh in range(n_heads):
            s_ref[0, hh] = st_ref[hh].T


def _hgrn_prompt(z3, col, lb_logits, hgrn_gain, layer, *, tl):
    b, s, _ = z3.shape
    w3, level, n_lev = _hgrn_consts(HGRN_ROWS)
    nh = HGRN_HEADS_PER_STEP
    wblk = nh * B_DIM

    def cspec(name):
        c0 = col[name] // wblk
        return pl.BlockSpec((1, tl, wblk), lambda bb, h, i: (bb, i, c0 + h))

    const = lambda shape: pl.BlockSpec(shape, lambda bb, h, i: (0,) * len(shape))
    return pl.pallas_call(
        functools.partial(_hgrn_prompt_kernel, layer=layer, n_lev=n_lev, tl=tl, n_heads=nh),
        grid=(b, B_HEADS // nh, s // tl),
        in_specs=[cspec("bq"), cspec("bf"), cspec("bi"),
                  pl.BlockSpec((lb_logits.shape[0], wblk), lambda bb, h, i: (0, h)),
                  const((1, B_DIM)), const(w3.shape), const(level.shape)],
        out_specs=[pl.BlockSpec((1, tl, wblk), lambda bb, h, i: (bb, i, h)),
                   pl.BlockSpec((1, nh, B_DIM, B_DIM), lambda bb, h, i: (bb, h, 0, 0))],
        out_shape=[jax.ShapeDtypeStruct((b, s, B_WIDTH), BF16),
                   jax.ShapeDtypeStruct((b, B_HEADS, B_DIM, B_DIM), F32)],
        scratch_shapes=[pltpu.VMEM((nh, B_DIM, B_DIM), F32)],
        compiler_params=_cparams(("parallel", "parallel", "arbitrary")),
        name="hgrn_prompt",
    )(z3, z3, z3, lb_logits, hgrn_gain.reshape(1, B_DIM), w3, level)


def _hgrn_sample_kernel(q_ref, f_ref, v_ref, s0_ref, lbl_ref, gn_ref, w3_ref, lev_ref, acc_ref,
                        o_ref, s_ref, *, layer, n_lev, t_new, n_heads):
    del acc_ref
    lb = _lower_bound(lbl_ref, layer)
    n_seg = HGRN_ROWS // t_new
    head = [slice(hh * B_DIM, (hh + 1) * B_DIM) for hh in range(n_heads)]
    items = [(q_ref[:, hs], f_ref[:, hs], v_ref[:, hs], lb[:, hs],
              [s0_ref[0, si, hh].T for si in range(n_seg)])
             for hh, hs in enumerate(head)]
    results = _hgrn_blocks(items, w3_ref[...], lev_ref[...], n_lev, t_new)
    for hh, (o, new_states) in enumerate(results):
        o_ref[:, head[hh]] = _rmsnorm_rows(o, gn_ref[...]).astype(o_ref.dtype)
        for si in range(n_seg):
            s_ref[0, si, hh] = new_states[si].T


def _hgrn_sample(z_s, state, new_state, lb_logits, hgrn_gain, layer, *, n_seq, t_new):
    w3, level, n_lev = _hgrn_consts(t_new)
    n_seg = HGRN_ROWS // t_new
    nh = HGRN_SAMPLE_HEADS_PER_STEP
    wblk = nh * B_DIM

    def cspec(name):
        c0 = _COL_S[name] // wblk
        return pl.BlockSpec((HGRN_ROWS, wblk), lambda i, h: (i, c0 + h))

    const = lambda shape: pl.BlockSpec(shape, lambda i, h: (0,) * len(shape))
    return pl.pallas_call(
        functools.partial(_hgrn_sample_kernel, layer=layer, n_lev=n_lev, t_new=t_new, n_heads=nh),
        grid=(n_seq // n_seg, B_HEADS // nh),
        in_specs=[cspec("bq"), cspec("bf"), cspec("bi"),
                  pl.BlockSpec((1, n_seg, nh, B_DIM, B_DIM), lambda i, h, layer=layer: (layer, i, h, 0, 0)),
                  pl.BlockSpec((lb_logits.shape[0], wblk), lambda i, h: (0, h)),
                  const((1, B_DIM)), const(w3.shape), const(level.shape),
                  pl.BlockSpec(memory_space=pl.ANY)],
        out_specs=[pl.BlockSpec((HGRN_ROWS, wblk), lambda i, h: (i, h)),
                   pl.BlockSpec((1, n_seg, nh, B_DIM, B_DIM), lambda i, h, layer=layer: (layer, i, h, 0, 0))],
        out_shape=[jax.ShapeDtypeStruct((n_seq * t_new, B_WIDTH), BF16),
                   jax.ShapeDtypeStruct(new_state.shape, F32)],
        input_output_aliases={8: 1},
        compiler_params=_cparams(("parallel", "parallel")),
        name="hgrn_sample",
    )(z_s, z_s, z_s, state, lb_logits, hgrn_gain.reshape(1, B_DIM), w3, level, new_state)


_POOL_PAD = 16


def _pool_mix(ext_at, u, pos, pw_ref, ps_ref):
    outs = []
    for g, win in enumerate(C_WINDOWS):
        gs = slice(g * C_GROUP_DIM, (g + 1) * C_GROUP_DIM)
        acc = u[:, gs]
        for k in range(1, win):
            acc = acc + ext_at(k, gs)
        cnt = jnp.minimum(pos + 1, win).astype(F32)
        pooled = acc * (1.0 / cnt) - u[:, gs]
        mixed = jnp.dot(pooled.astype(BF16), pw_ref[g].astype(BF16), preferred_element_type=F32)
        outs.append(mixed * ps_ref[:, gs])
    return outs


def _pool_prompt_kernel(u_ref, h_ref, pw_ref, ps_ref, o_ref, ext_ref, *, tl):
    i = pl.program_id(1)
    halo = h_ref[0].astype(F32)
    ext_ref[0:_POOL_PAD, :] = jnp.where(i > 0, halo, 0.0)
    u = u_ref[0].astype(F32)
    ext_ref[_POOL_PAD:, :] = u
    pos = i * tl + lax.broadcasted_iota(jnp.int32, (tl, 1), 0)
    outs = _pool_mix(lambda k, gs: ext_ref[_POOL_PAD - k:_POOL_PAD - k + tl, gs], u, pos, pw_ref, ps_ref)
    for g, og in enumerate(outs):
        o_ref[0, :, g * C_GROUP_DIM:(g + 1) * C_GROUP_DIM] = og.astype(o_ref.dtype)


def _pool_prompt(z3, col, pool_w, pool_scale, *, tl):
    b, s, _ = z3.shape
    c0 = col["cu"] // C_WIDTH
    per = tl // _POOL_PAD
    return pl.pallas_call(
        functools.partial(_pool_prompt_kernel, tl=tl),
        grid=(b, s // tl),
        in_specs=[pl.BlockSpec((1, tl, C_WIDTH), lambda bb, i: (bb, i, c0)),
                  pl.BlockSpec((1, _POOL_PAD, C_WIDTH), lambda bb, i: (bb, jnp.maximum(i * per - 1, 0), c0)),
                  pl.BlockSpec((C_GROUPS, C_GROUP_DIM, C_GROUP_DIM), lambda bb, i: (0, 0, 0)),
                  pl.BlockSpec((1, C_WIDTH), lambda bb, i: (0, 0))],
        out_specs=pl.BlockSpec((1, tl, C_WIDTH), lambda bb, i: (bb, i, 0)),
        out_shape=jax.ShapeDtypeStruct((b, s, C_WIDTH), BF16),
        scratch_shapes=[pltpu.VMEM((_POOL_PAD + tl, C_WIDTH), F32)],
        compiler_params=_cparams(("parallel", "arbitrary")),
        name="pool_prompt",
    )(z3, z3, pool_w, pool_scale.reshape(1, C_WIDTH))


def _pool_sample_kernel(u_ref, h_ref, pw_ref, ps_ref, o_ref, ext_ref, *, nb, t_new, first_pos):
    ext_ref[:, 0:1, :] = jnp.zeros((nb, 1, C_WIDTH), F32)
    ext_ref[:, 1:_POOL_PAD, :] = h_ref[0]
    u3 = u_ref[...].reshape(nb, t_new, C_WIDTH)
    ext_ref[:, _POOL_PAD:, :] = u3
    u = u_ref[...]
    pos = first_pos + lax.broadcasted_iota(jnp.int32, (nb * t_new, 1), 0) % t_new

    def ext_at(k, gs):
        return ext_ref[:, _POOL_PAD - k:_POOL_PAD - k + t_new, gs].reshape(nb * t_new, C_GROUP_DIM)

    outs = _pool_mix(ext_at, u, pos, pw_ref, ps_ref)
    for g, og in enumerate(outs):
        o_ref[:, g * C_GROUP_DIM:(g + 1) * C_GROUP_DIM] = og.astype(o_ref.dtype)


def _pool_sample(z_s, state_pool, pool_w, pool_scale, layer, *, n_seq, t_new, first_pos, nb):
    c0 = _COL_S["cu"] // C_WIDTH
    return pl.pallas_call(
        functools.partial(_pool_sample_kernel, nb=nb, t_new=t_new, first_pos=first_pos),
        grid=(n_seq // nb,),
        in_specs=[pl.BlockSpec((nb * t_new, C_WIDTH), lambda i: (i, c0)),
                  pl.BlockSpec((1, nb, C_HIST, C_WIDTH), lambda i, layer=layer: (layer, i, 0, 0)),
                  pl.BlockSpec((C_GROUPS, C_GROUP_DIM, C_GROUP_DIM), lambda i: (0, 0, 0)),
                  pl.BlockSpec((1, C_WIDTH), lambda i: (0, 0))],
        out_specs=pl.BlockSpec((nb * t_new, C_WIDTH), lambda i: (i, 0)),
        out_shape=jax.ShapeDtypeStruct((n_seq * t_new, C_WIDTH), BF16),
        scratch_shapes=[pltpu.VMEM((nb, _POOL_PAD + t_new, C_WIDTH), F32)],
        compiler_params=_cparams(("parallel",)),
        name="pool_sample",
    )(z_s, state_pool, pool_w, pool_scale.reshape(1, C_WIDTH))


def _token_order(ref, scr_ref, d, tm):
    n_chunk = ref.shape[-1] // LANE
    if d == 1:
        return [ref[0, 0, :, c * LANE:(c + 1) * LANE] for c in range(n_chunk)]
    for r in range(d):
        for c in range(n_chunk):
            scr_ref[c, pl.ds(r, tm // d, stride=d), :] = ref[0, r, :, c * LANE:(c + 1) * LANE]
    return [scr_ref[c] for c in range(n_chunk)]


def _out_proj_kernel(x_ref, ag_ref, bg_ref, cg_ref, m0_ref, m1_ref, m2_ref, o0_ref, o1_ref, o2_ref,
                     l0_ref, l1_ref, l2_ref, ob_ref, oc_ref, wa_ref, wb_ref, wc_ref, wo_ref,
                     gp_ref, y_ref, *scratch, dil, tm):
    scr = iter(scratch)
    o_g, l_g = [], []
    for o_ref, l_ref, d in zip((o0_ref, o1_ref, o2_ref), (l0_ref, l1_ref, l2_ref), dil):
        so = next(scr) if d > 1 else None
        sl = next(scr) if d > 1 else None
        o_g.append(_token_order(o_ref, so, d, tm))
        l_g.append(_token_order(l_ref, sl, d, tm)[0])
    heads = []
    for h in range(A_HEADS):
        l0, l1, l2 = (l[:, h:h + 1] for l in l_g)
        mx = jnp.maximum(jnp.maximum(l0, l1), l2)
        e0, e1, e2 = jnp.exp(l0 - mx), jnp.exp(l1 - mx), jnp.exp(l2 - mx)
        inv = 1.0 / (e0 + e1 + e2)
        heads.append((e0 * inv) * o_g[0][h] + (e1 * inv) * o_g[1][h] + (e2 * inv) * o_g[2][h])
    o_a = jnp.concatenate(heads, axis=1)

    def silu_gated(o, g_ref):
        h = g_ref[...].astype(F32) * 0.5
        return (o.astype(F32) * (h + h * jnp.tanh(h))).astype(BF16)

    def merge_gated(m_ref, y, w_ref):
        p = jnp.dot(y, w_ref[...], preferred_element_type=F32)
        return p + jnp.tanh(m_ref[...].astype(F32) * 0.5) * p

    merged = (merge_gated(m0_ref, silu_gated(o_a, ag_ref), wa_ref)
              + merge_gated(m1_ref, silu_gated(ob_ref[...], bg_ref), wb_ref)
              + merge_gated(m2_ref, silu_gated(oc_ref[...], cg_ref), wc_ref)) * 0.5
    out = jnp.dot(merged.astype(BF16), wo_ref[...], preferred_element_type=F32)
    y_ref[...] = x_ref[...] + _rmsnorm_rows(out, gp_ref[...])


def _out_proj(x, z, col, o_g, lse_g, o_b, o_c, wa, wb, wc, wo, gain_post, *, tm):
    n = x.shape[0]
    dil = tuple(int(a.shape[1]) for a in o_g)
    tpb = (n // o_g[0].shape[0]) // tm

    def zcol(name, width, k=0):
        assert col[name] % width == 0
        c0 = col[name] // width + k
        return pl.BlockSpec((tm, width), lambda i: (i, c0))

    def gspec(d, width):
        return pl.BlockSpec((1, d, tm // d, width), lambda i: (i // tpb, 0, i % tpb, 0))

    rows = lambda width: pl.BlockSpec((tm, width), lambda i: (i, 0))
    const = lambda a: pl.BlockSpec(a.shape, lambda i: (0,) * a.ndim)
    gp = gain_post.reshape(1, D_MODEL)
    scratch = []
    for d in dil:
        if d > 1:
            scratch += [pltpu.VMEM((A_WIDTH // LANE, tm, LANE), F32), pltpu.VMEM((1, tm, LANE), F32)]
    return pl.pallas_call(
        functools.partial(_out_proj_kernel, dil=dil, tm=tm),
        grid=(n // tm,),
        in_specs=[rows(D_MODEL), zcol("ag", A_WIDTH), zcol("bg", B_WIDTH), zcol("cg", C_WIDTH),
                  zcol("mg", D_MODEL, 0), zcol("mg", D_MODEL, 1), zcol("mg", D_MODEL, 2)]
                 + [gspec(d, A_WIDTH) for d in dil] + [gspec(d, LANE) for d in dil]
                 + [rows(B_WIDTH), rows(C_WIDTH)]
                 + [const(wa), const(wb), const(wc), const(wo), const(gp)],
        out_specs=rows(D_MODEL),
        out_shape=jax.ShapeDtypeStruct((n, D_MODEL), F32),
        scratch_shapes=scratch,
        compiler_params=_cparams(("parallel",)),
        name="out_proj",
    )(x, z, z, z, z, z, z, *o_g, *lse_g, o_b, o_c, wa, wb, wc, wo, gp)


def _ref_cols(w, name, g=None):
    off = _REF_OFF[name]
    if g is None:
        return w[:, off:off + _WIDTHS[name]]
    return w[:, off + g * A_WIDTH:off + (g + 1) * A_WIDTH]


def _prep_w_in(w):
    parts = []
    for g in range(1, A_GROUPS):
        parts += [_ref_cols(w, "ak", g), _ref_cols(w, "av", g), _ref_cols(w, "aq", g)]
    parts += [_ref_cols(w, "ak", 0), _ref_cols(w, "av", 0), _ref_cols(w, "cu"), _ref_cols(w, "cg"),
              _ref_cols(w, "aq", 0)] + [_ref_cols(w, k) for k in ("ag", "bq", "bf", "bi", "bg", "mg")]
    return jnp.concatenate(parts, axis=1).astype(BF16)


def _kv_split(kv):
    return kv.reshape(kv.shape[:-1] + (2, A_HEADS, A_HEAD_DIM))


def kernel(x_prompt, x_sample, cache_kv_w128, cache_kv_w512, cache_kv_w2048, state_hgrn, state_pool,
           norm_pre, norm_post, w_in, hgrn_lb_logits, hgrn_norm, pool_w, pool_scale,
           w_br_a, w_br_b, w_br_c, w_out):
    depth = w_in.shape[0]
    b_p, s_p, _ = x_prompt.shape
    n_seq, t_new, _ = x_sample.shape
    caches = (cache_kv_w128, cache_kv_w512, cache_kv_w2048)
    past_len = cache_kv_w2048.shape[2]
    assert HGRN_ROWS % t_new == 0 and n_seq % (HGRN_ROWS // t_new) == 0
    n_p, n_s = b_p * s_p, n_seq * t_new

    hp = x_prompt.reshape(n_p, D_MODEL)
    hs = x_sample.reshape(n_s, D_MODEL)
    kv_p = [[] for _ in range(A_GROUPS)]
    kv_s = [[] for _ in range(A_GROUPS)]
    hg_p, pl_p, pl_s = [], [], []
    hg_s = jnp.zeros(state_hgrn.shape, F32)
    tm_p = min(1024, s_p)
    for l in range(depth):
        w_l = _prep_w_in(w_in[l])
        wa, wb, wc, wo = (w[l].astype(BF16) for w in (w_br_a, w_br_b, w_br_c, w_out))

        (z_p,) = _in_proj(hp, norm_pre[l], w_l, col0=_P_COL0, tm=min(2048, s_p), tn=512,
                          z_dtype=BF16, f32_cols=0)
        z3 = z_p.reshape(b_p, s_p, _P_WIDTH)
        hp3 = hp.reshape(b_p, s_p, D_MODEL)

        def tail(rows):
            return hp3[:, s_p - rows:].reshape(b_p * rows, D_MODEL)

        (u_tail,) = _in_proj(tail(_POOL_PAD), norm_pre[l], w_l, col0=_COL_S["cu"], width=C_WIDTH,
                             tm=b_p * _POOL_PAD, tn=C_WIDTH, z_dtype=F32, f32_cols=0)
        pl_p.append(u_tail.reshape(b_p, _POOL_PAD, C_WIDTH)[:, _POOL_PAD - C_HIST:])
        o_g, lse_g = [], []
        for g, (win, d) in enumerate(A_PATTERNS):
            rows = min(win, s_p)
            kv = _kv_rows(tail(rows), norm_pre[l], w_l, _attn_cols(g)[0], tm=min(1024, b_p * rows))
            kv_p[g].append(kv.reshape(b_p, rows, 2, A_HEADS, A_HEAD_DIM))
            if g == 0:
                arr = z_p.reshape(b_p, 1, s_p, _P_WIDTH)
                cols = (_COL_P["aq"] // A_WIDTH, _COL_P["ak"] // A_WIDTH, _COL_P["av"] // A_WIDTH)
            else:
                arr = _in_proj_perm(hp, norm_pre[l], w_l, g, b=b_p, s=s_p, d=d, tm=tm_p)
                cols = (2, 0, 1)
            o, lse = _attn_prompt(arr, cols, tq=min(512, s_p // d))
            o_g.append(o)
            lse_g.append(lse)
        o_b, s_new = _hgrn_prompt(z3, _COL_P, hgrn_lb_logits, hgrn_norm[l], l, tl=min(1024, s_p))
        hg_p.append(s_new)
        o_c = _pool_prompt(z3, _COL_P, pool_w[l], pool_scale[l], tl=min(1024, s_p))
        hp = _out_proj(hp, z_p, _COL_P, o_g, lse_g, o_b.reshape(n_p, B_WIDTH),
                       o_c.reshape(n_p, C_WIDTH), wa, wb, wc, wo, norm_post[l], tm=min(512, s_p))

        (z_s,) = _in_proj(hs, norm_pre[l], w_l, col0=0, tm=min(1024, n_s), tn=1024, z_dtype=F32,
                          f32_cols=0)
        o_g, lse_g = _attn_sample(z_s, caches, l, n_seq=n_seq, t_new=t_new)
        zs3 = z_s.reshape(n_seq, t_new, IN_WIDTH)
        for g in range(A_GROUPS):
            k_off, v_off, _ = _attn_cols(g)
            kv_s[g].append(_kv_split(zs3[:, :, k_off:v_off + A_WIDTH]))
        o_b, hg_s = _hgrn_sample(z_s, state_hgrn, hg_s, hgrn_lb_logits, hgrn_norm[l], l,
                                 n_seq=n_seq, t_new=t_new)
        o_c = _pool_sample(z_s, state_pool, pool_w[l], pool_scale[l], l, n_seq=n_seq,
                           t_new=t_new, first_pos=past_len, nb=HGRN_ROWS // t_new)
        u_new = zs3[:, :, _COL_S["cu"]:_COL_S["cu"] + C_WIDTH]
        pl_s.append(jnp.concatenate([state_pool[l], u_new], axis=1)[:, -C_HIST:])
        as4 = lambda a: a.reshape(1, 1, n_s, a.shape[-1])
        hs = _out_proj(hs, z_s, _COL_S, [as4(a) for a in o_g], [as4(a) for a in lse_g], o_b, o_c,
                       wa, wb, wc, wo, norm_post[l], tm=min(256, n_s))

    return (hp.reshape(b_p, s_p, D_MODEL), hs.reshape(n_seq, t_new, D_MODEL),
            jnp.stack(kv_p[0]), jnp.stack(kv_p[1]), jnp.stack(kv_p[2]),
            jnp.stack(kv_s[0]), jnp.stack(kv_s[1]), jnp.stack(kv_s[2]),
            jnp.stack(hg_p), hg_s, jnp.stack(pl_p), jnp.stack(pl_s))
```

```python
import functools

import numpy as np
import jax
import jax.numpy as jnp
from jax import lax
from jax.experimental import pallas as pl
from jax.experimental.pallas import tpu as pltpu

F32 = jnp.float32
BF16 = jnp.bfloat16

D_MODEL = 1024
A_PATTERNS = ((128, 1), (512, 4), (2048, 16))
A_GROUPS = 3
A_HEADS = 4
A_HEAD_DIM = 128
A_WIDTH = A_HEADS * A_HEAD_DIM
N_STEPS = 128
B_HEADS = 8
B_DIM = 128
B_WIDTH = B_HEADS * B_DIM
C_WINDOWS = (2, 4, 8, 16)
C_GROUPS = 4
C_GROUP_DIM = 128
C_WIDTH = C_GROUPS * C_GROUP_DIM
C_HIST = 15
N_BRANCH = 3
EPS = 1e-6
NEG = -1e30
IN_WIDTH = 13312
LANE = 128
HGRN_ROWS = 128
HGRN_HEADS_PER_STEP = 8
HGRN_SAMPLE_HEADS_PER_STEP = 4

_REF_OFF = dict(aq=0, ak=1536, av=3072, ag=4608, bq=5120, bf=6144, bi=7168, bg=8192,
                cu=9216, cg=9728, mg=10240)
_WIDTHS = dict(aq=1536, ak=1536, av=1536, ag=512, bq=1024, bf=1024, bi=1024, bg=1024,
               cu=512, cg=512, mg=3072)

_PERM_WIDTH = 3 * A_WIDTH
_P_COL0 = (A_GROUPS - 1) * _PERM_WIDTH
_COL_P = dict(ak=0, av=512, cu=1024, cg=1536, aq=2048, ag=2560, bq=3072, bf=4096,
              bi=5120, bg=6144, mg=7168)
_P_WIDTH = IN_WIDTH - _P_COL0
_COL_S = {name: _P_COL0 + off for name, off in _COL_P.items()}


def _attn_cols(g):
    if g == 0:
        return _COL_S["ak"], _COL_S["av"], _COL_S["aq"]
    base = (g - 1) * _PERM_WIDTH
    return base, base + A_WIDTH, base + 2 * A_WIDTH

_VMEM_LIMIT = 56 * 1024 * 1024


def _cparams(sem):
    return pltpu.CompilerParams(dimension_semantics=sem, vmem_limit_bytes=_VMEM_LIMIT)


def _nt(a, b):
    return lax.dot_general(a, b, (((1,), (1,)), ((), ())), preferred_element_type=F32)


def _tn(a, b):
    return lax.dot_general(a, b, (((0,), (0,)), ((), ())), preferred_element_type=F32)


def _sigmoid(x):
    return 1.0 / (1.0 + jnp.exp(-x))


def _rmsnorm_rows(x, gain):
    return x * lax.rsqrt(jnp.mean(x * x, axis=-1, keepdims=True) + EPS) * gain


def _in_proj_kernel(x_ref, g_ref, w_ref, *refs, n_f32_tiles, z_dtype):
    if n_f32_tiles:
        z_ref, zf_ref, xn_ref = refs
    else:
        z_ref, xn_ref = refs
    j = pl.program_id(1)

    @pl.when(j == 0)
    def _():
        xn_ref[...] = _rmsnorm_rows(x_ref[...], g_ref[...]).astype(BF16)

    acc = jnp.dot(xn_ref[...], w_ref[...], preferred_element_type=F32)
    z_ref[...] = acc.astype(z_dtype)
    if n_f32_tiles:
        @pl.when(j < n_f32_tiles)
        def _():
            zf_ref[...] = acc


def _tail_row_block(seq, rows, tm):
    per_seq, tiles, first = seq // tm, rows // tm, (seq - rows) // tm
    assert seq % tm == 0 and rows % tm == 0
    return lambda i: (i // tiles) * per_seq + first + i % tiles


def _in_proj(x, gain, w_bf16, *, col0, tm, tn, z_dtype, f32_cols, width=None, n_rows=None,
             row_block=lambda i: i):
    n = x.shape[0] if n_rows is None else n_rows
    width = w_bf16.shape[1] - col0 if width is None else width
    j0 = col0 // tn
    n_f32_tiles = f32_cols // tn
    out_shape = [jax.ShapeDtypeStruct((n, width), z_dtype)]
    out_specs = [pl.BlockSpec((tm, tn), lambda i, j: (i, j))]
    if n_f32_tiles:
        out_shape.append(jax.ShapeDtypeStruct((n, f32_cols), F32))
        out_specs.append(pl.BlockSpec((tm, tn), lambda i, j: (i, jnp.minimum(j, n_f32_tiles - 1))))
    return pl.pallas_call(
        functools.partial(_in_proj_kernel, n_f32_tiles=n_f32_tiles, z_dtype=z_dtype),
        grid=(n // tm, width // tn),
        in_specs=[pl.BlockSpec((tm, D_MODEL), lambda i, j: (row_block(i), 0)),
                  pl.BlockSpec((1, D_MODEL), lambda i, j: (0, 0)),
                  pl.BlockSpec((D_MODEL, tn), lambda i, j: (0, j + j0))],
        out_specs=out_specs,
        out_shape=out_shape,
        scratch_shapes=[pltpu.VMEM((tm, D_MODEL), BF16)],
        compiler_params=_cparams(("parallel", "arbitrary")),
        name="in_proj",
    )(x, gain.reshape(1, D_MODEL), w_bf16)


def _kv_rows_kernel(x_ref, g_ref, wk_ref, wv_ref, o_ref, *, tm):
    xn = _rmsnorm_rows(x_ref[...], g_ref[...]).astype(BF16)
    per_tok = 2 * A_HEADS
    for kv, w_ref in enumerate((wk_ref, wv_ref)):
        acc = jnp.dot(xn, w_ref[...], preferred_element_type=F32)
        for h in range(A_HEADS):
            o_ref[pl.ds(kv * A_HEADS + h, tm, stride=per_tok), :] = (
                acc[:, h * A_HEAD_DIM:(h + 1) * A_HEAD_DIM])


def _kv_rows(x, gain, w_bf16, k_col, *, n_seq, seq, rows, tm):
    n = n_seq * rows
    per_tok = 2 * A_HEADS
    kb = k_col // A_WIDTH
    row_block = _tail_row_block(seq, rows, tm)
    return pl.pallas_call(
        functools.partial(_kv_rows_kernel, tm=tm),
        grid=(n // tm,),
        in_specs=[pl.BlockSpec((tm, D_MODEL), lambda i: (row_block(i), 0)),
                  pl.BlockSpec((1, D_MODEL), lambda i: (0, 0)),
                  pl.BlockSpec((D_MODEL, A_WIDTH), lambda i: (0, kb)),
                  pl.BlockSpec((D_MODEL, A_WIDTH), lambda i: (0, kb + 1))],
        out_specs=pl.BlockSpec((tm * per_tok, A_HEAD_DIM), lambda i: (i, 0)),
        out_shape=jax.ShapeDtypeStruct((n * per_tok, A_HEAD_DIM), F32),
        compiler_params=_cparams(("parallel",)),
        name="kv_rows",
    )(x, gain.reshape(1, D_MODEL), w_bf16, w_bf16)


def _in_proj_perm_kernel(x_ref, g_ref, w_ref, r_ref, xs_ref, xn_ref, *, d, tm):
    xn = _rmsnorm_rows(x_ref[...], g_ref[...])
    n_chunk = D_MODEL // LANE
    rows = tm // d
    for c in range(n_chunk):
        xs_ref[c] = xn[:, c * LANE:(c + 1) * LANE]
    for r in range(d):
        for c in range(n_chunk):
            xn_ref[r * rows:(r + 1) * rows, c * LANE:(c + 1) * LANE] = (
                xs_ref[c, pl.ds(r, rows, stride=d), :].astype(BF16))
    acc = jnp.dot(xn_ref[...], w_ref[...], preferred_element_type=F32)
    r_ref[0] = acc.astype(BF16).reshape(d, rows, _PERM_WIDTH)


def _in_proj_perm(x, gain, w_bf16, g, *, b, s, d, tm):
    tpb = s // tm
    rows = tm // d
    idx = lambda i: (i // tpb, 0, i % tpb, 0)
    return pl.pallas_call(
        functools.partial(_in_proj_perm_kernel, d=d, tm=tm),
        grid=(b * tpb,),
        in_specs=[pl.BlockSpec((tm, D_MODEL), lambda i: (i, 0)),
                  pl.BlockSpec((1, D_MODEL), lambda i: (0, 0)),
                  pl.BlockSpec((D_MODEL, _PERM_WIDTH), lambda i: (0, g - 1))],
        out_specs=pl.BlockSpec((1, d, rows, _PERM_WIDTH), idx),
        out_shape=jax.ShapeDtypeStruct((b, d, s // d, _PERM_WIDTH), BF16),
        scratch_shapes=[pltpu.VMEM((D_MODEL // LANE, tm, LANE), F32),
                        pltpu.VMEM((tm, D_MODEL), BF16)],
        compiler_params=_cparams(("parallel",)),
        name="in_proj_d%d" % d,
    )(x, gain.reshape(1, D_MODEL), w_bf16)


def _attn_prompt_kernel(q_ref, kp_ref, kc_ref, vp_ref, vc_ref, o_ref, lse_ref, *, tq):
    i = pl.program_id(2)
    n = N_STEPS
    qi = lax.broadcasted_iota(jnp.int32, (n, 2 * n), 0)
    kj = lax.broadcasted_iota(jnp.int32, (n, 2 * n), 1)
    band = (kj >= qi) & (kj <= qi + n)
    band_first = band & ((i > 0) | (kj >= n))
    scale = A_HEAD_DIM ** -0.5
    lane = lax.broadcasted_iota(jnp.int32, (n, LANE), 1)
    for j in range(tq // n):
        rows = slice(j * n, (j + 1) * n)
        lse = jnp.zeros((n, LANE), F32)
        for h in range(A_HEADS):
            hs = slice(h * A_HEAD_DIM, (h + 1) * A_HEAD_DIM)
            q = q_ref[0, 0, rows, hs]
            if j == 0:
                k = jnp.concatenate([kp_ref[0, 0, :, hs], kc_ref[0, 0, rows, hs]], axis=0)
                v = jnp.concatenate([vp_ref[0, 0, :, hs], vc_ref[0, 0, rows, hs]], axis=0)
                valid = band_first
            else:
                k = kc_ref[0, 0, (j - 1) * n:(j + 1) * n, hs]
                v = vc_ref[0, 0, (j - 1) * n:(j + 1) * n, hs]
                valid = band
            s = jnp.where(valid, _nt(q, k) * scale, NEG)
            m = jnp.max(s, axis=-1, keepdims=True)
            p = jnp.exp(s - m)
            l = jnp.sum(p, axis=-1, keepdims=True)
            o = jnp.dot(p.astype(BF16), v, preferred_element_type=F32)
            o_ref[0, 0, rows, hs] = o / l
            lse = jnp.where(lane == h, m + jnp.log(l), lse)
        lse_ref[0, 0, rows, :] = lse


def _attn_prompt(arr, cols, *, tq):
    b, d, rows, _ = arr.shape
    qc, kc, vc = cols
    prev_per_blk = tq // N_STEPS

    def cur(c):
        return pl.BlockSpec((1, 1, tq, A_WIDTH), lambda bb, r, i: (bb, r, i, c))

    def prev(c):
        return pl.BlockSpec((1, 1, N_STEPS, A_WIDTH),
                            lambda bb, r, i: (bb, r, jnp.maximum(i * prev_per_blk - 1, 0), c))

    out_spec = pl.BlockSpec((1, 1, tq, A_WIDTH), lambda bb, r, i: (bb, r, i, 0))
    return pl.pallas_call(
        functools.partial(_attn_prompt_kernel, tq=tq),
        grid=(b, d, rows // tq),
        in_specs=[cur(qc), prev(kc), cur(kc), prev(vc), cur(vc)],
        out_specs=[out_spec, pl.BlockSpec((1, 1, tq, LANE), lambda bb, r, i: (bb, r, i, 0))],
        out_shape=[jax.ShapeDtypeStruct((b, d, rows, A_WIDTH), F32),
                   jax.ShapeDtypeStruct((b, d, rows, LANE), F32)],
        compiler_params=_cparams(("parallel", "parallel", "arbitrary")),
        name="attn_prompt_d%d" % d,
    )(arr, arr, arr, arr, arr)


def _gather_pitch(n_j):
    p = -(-n_j // 4)
    return 4 * (p if p % 2 else p + 1)


def _attn_sample_kernel(z_ref, c0_ref, c1_ref, c2_ref, *refs, t_new):
    out_refs, flat_refs = refs[:2 * A_GROUPS], refs[2 * A_GROUPS:]
    scale = A_HEAD_DIM ** -0.5
    nrow = A_HEADS * t_new
    per_res = 2 * A_HEADS
    for g, c_ref in enumerate((c0_ref, c1_ref, c2_ref)):
        d = A_PATTERNS[g][1]
        n_res = min(d, t_new)
        o_ref, lse_ref = out_refs[2 * g], out_refs[2 * g + 1]
        k_off, v_off, q_off = _attn_cols(g)
        q = z_ref[:, q_off:q_off + A_WIDTH]
        kn = z_ref[:, k_off:k_off + A_WIDTH]
        vn = z_ref[:, v_off:v_off + A_WIDTH]
        n_j = n_res * per_res
        pitch = _gather_pitch(n_j)
        flat_ref = flat_refs[g]
        for mm in range(N_STEPS):
            flat_ref[mm * pitch:mm * pitch + n_j, :] = c_ref[mm]
        gather = lambda j: flat_ref[pl.ds(j, N_STEPS, stride=pitch), :]
        kcat = jnp.concatenate([gather(r * per_res + h)
                                for r in range(n_res) for h in range(A_HEADS)], axis=1).astype(BF16)
        vcat = jnp.concatenate([gather(r * per_res + A_HEADS + h)
                                for r in range(n_res) for h in range(A_HEADS)], axis=1).astype(BF16)
        wide = n_res * A_WIDTH
        q4 = jnp.concatenate([q] * A_HEADS, axis=0)
        row = lax.broadcasted_iota(jnp.int32, (nrow, wide), 0)
        col = lax.broadcasted_iota(jnp.int32, (nrow, wide), 1)
        row_h, row_t = row // t_new, row % t_new
        sel = ((col // A_WIDTH) == (row_t % d)) & (((col % A_WIDTH) // A_HEAD_DIM) == row_h)
        qbd = jnp.where(sel, jnp.concatenate([q4] * n_res, axis=1), 0.0).astype(BF16)
        s_c = _nt(qbd, kcat) * scale
        r2 = lax.broadcasted_iota(jnp.int32, (nrow, N_STEPS), 0) % t_new
        c2 = lax.broadcasted_iota(jnp.int32, (nrow, N_STEPS), 1)
        s_c = jnp.where(c2 >= r2 // d, s_c, NEG)
        pad = jnp.zeros((N_STEPS - t_new, A_WIDTH), F32)
        knp = jnp.concatenate([kn, pad], axis=0).astype(BF16)
        vnp = jnp.concatenate([vn, pad], axis=0).astype(BF16)
        rowq = lax.broadcasted_iota(jnp.int32, (nrow, A_WIDTH), 0) // t_new
        colq = lax.broadcasted_iota(jnp.int32, (nrow, A_WIDTH), 1) // A_HEAD_DIM
        selq = rowq == colq
        qbd2 = jnp.where(selq, q4, 0.0).astype(BF16)
        s_n = _nt(qbd2, knp) * scale
        s_n = jnp.where((c2 <= r2) & ((r2 - c2) % d == 0), s_n, NEG)
        m = jnp.maximum(jnp.max(s_c, axis=-1, keepdims=True), jnp.max(s_n, axis=-1, keepdims=True))
        p_c = jnp.exp(s_c - m)
        p_n = jnp.exp(s_n - m)
        l = jnp.sum(p_c, axis=-1, keepdims=True) + jnp.sum(p_n, axis=-1, keepdims=True)
        o_wide = jnp.where(sel, jnp.dot(p_c.astype(BF16), vcat, preferred_element_type=F32), 0.0)
        o_new = jnp.where(selq, jnp.dot(p_n.astype(BF16), vnp, preferred_element_type=F32), 0.0)
        o = o_new[:, 0:A_HEAD_DIM]
        for c in range(1, A_HEADS):
            o = o + o_new[:, c * A_HEAD_DIM:(c + 1) * A_HEAD_DIM]
        for c in range(wide // A_HEAD_DIM):
            o = o + o_wide[:, c * A_HEAD_DIM:(c + 1) * A_HEAD_DIM]
        o = o / l
        lse = m + jnp.log(l)
        lane = lax.broadcasted_iota(jnp.int32, (t_new, LANE), 1)
        lse_out = jnp.zeros((t_new, LANE), F32)
        for h in range(A_HEADS):
            hs = slice(h * A_HEAD_DIM, (h + 1) * A_HEAD_DIM)
            o_ref[:, hs] = o[h * t_new:(h + 1) * t_new]
            lse_out = jnp.where(lane == h, lse[h * t_new:(h + 1) * t_new], lse_out)
        lse_ref[...] = lse_out


def _attn_sample(z_s, caches, layer, *, n_seq, t_new):
    in_specs = [pl.BlockSpec((t_new, IN_WIDTH), lambda b: (b, 0))]
    views, scratch = [], []
    per_res = 2 * A_HEADS
    for g, c in enumerate(caches):
        win, d = A_PATTERNS[g]
        assert c.shape[2] == win == N_STEPS * d, "cache must hold one full window"
        n_res = min(d, t_new)
        views.append(c.reshape(c.shape[0], n_seq, N_STEPS, d * per_res, A_HEAD_DIM))
        in_specs.append(pl.BlockSpec((None, None, N_STEPS, n_res * per_res, A_HEAD_DIM),
                                     lambda b, layer=layer: (layer, b, 0, 0, 0)))
        scratch.append(pltpu.VMEM((N_STEPS * _gather_pitch(n_res * per_res), A_HEAD_DIM), F32))
    out_specs = [pl.BlockSpec((t_new, A_WIDTH), lambda b: (b, 0)),
                 pl.BlockSpec((t_new, LANE), lambda b: (b, 0))] * A_GROUPS
    out_shape = [jax.ShapeDtypeStruct((n_seq * t_new, A_WIDTH), F32),
                 jax.ShapeDtypeStruct((n_seq * t_new, LANE), F32)] * A_GROUPS
    outs = pl.pallas_call(
        functools.partial(_attn_sample_kernel, t_new=t_new),
        grid=(n_seq,),
        in_specs=in_specs,
        out_specs=out_specs,
        out_shape=out_shape,
        scratch_shapes=scratch,
        compiler_params=_cparams(("parallel",)),
        name="attn_sample",
    )(z_s, *views)
    return outs[0::2], outs[1::2]


def _hgrn_consts(seg):
    n = HGRN_ROWS
    t = np.arange(n)[:, None]
    u = np.arange(n)[None, :]
    seg_lo = (t // seg) * seg
    seg_hi = seg_lo + seg - 1
    mats = [(u >= seg_lo) & (u <= t),
            (u > t) & (u <= seg_hi)]
    level = np.full((n, n), -1, np.int32)
    s = u
    li = 0
    m = seg
    while m >= 2:
        half = m // 2
        upper = (t % m) >= half
        ref = (t // m) * m + half - 1
        if m > 2:
            mats.append(np.where(upper, (u > ref) & (u <= t), (u > t) & (u <= ref)))
        pair = (t // m == s // m) & ((t % m) >= half) & ((s % m) < half)
        level[pair] = li
        li += 1
        m //= 2
    level[np.arange(n), np.arange(n)] = li
    w = np.concatenate(mats, axis=0).astype(np.float32)
    w2 = np.concatenate([w, w], axis=1)
    return jnp.asarray(w2, BF16), jnp.asarray(level), li


def _hgrn_blocks(items, w2, level, n_lev, seg):
    n = HGRN_ROWS
    n_seg = n // seg
    row = lax.broadcasted_iota(jnp.int32, (n, B_DIM), 0)

    gates = []
    for q, xf, v, lb, states in items:
        f = lb + (1.0 - lb) * _sigmoid(xf)
        g = jnp.log2(f)
        g_hi = g.astype(BF16)
        g_lo = (g - g_hi.astype(F32)).astype(BF16)
        gates.append((f, jnp.concatenate([g_hi, g_lo], axis=0)))
    stage1 = []
    for i in range(0, len(items), 2):
        pair = gates[i:i + 2]
        rhs = pair[0][1] if len(pair) == 1 else jnp.concatenate([pair[0][1], pair[1][1]], axis=1)
        dsum = jnp.dot(w2, rhs, preferred_element_type=F32)
        for k, (f, _) in enumerate(pair):
            stage1.append((f, 1.0 - f, dsum[:, k * B_DIM:(k + 1) * B_DIM]))

    stage2 = []
    for (q, xf, v, lb, states), (f, kin, dsum) in zip(items, stage1):
        fac = jnp.exp2(dsum)
        ea = fac[0:n]
        ops = []
        m = seg
        for li in range(n_lev):
            half = m // 2
            if half % 8 == 0:
                src = jnp.concatenate([(kin if (r0 // half) % 2 == 0 else q)[r0:r0 + half]
                                       for r0 in range(0, n, half)], axis=0)
                tm = src * fac[(2 + li) * n:(3 + li) * n]
            elif m > 2:
                tm = jnp.where((row & half) != 0, q, kin) * fac[(2 + li) * n:(3 + li) * n]
            else:
                tm = jnp.where((row & half) != 0, q * f, kin)
            ops.append(tm.astype(BF16))
            m //= 2
        stage2.append((ea, (q * ea).astype(BF16), kin * fac[n:2 * n], ops))

    stage3 = []
    in_level = [level == li for li in range(n_lev + 1)]
    for (q, xf, v, lb, states), (f, kin, dsum), (ea, qt, kh, ops) in zip(items, stage1, stage2):
        att = jnp.zeros((n, n), F32)
        for li, tm in enumerate(ops):
            att = jnp.where(in_level[li], _nt(tm, tm), att)
        att = jnp.where(in_level[n_lev], _nt(q.astype(BF16), kin.astype(BF16)), att)
        stage3.append(att.astype(BF16))

    results = []
    for (q, xf, v, lb, states), (ea, qt, kh, ops), att in zip(items, stage2, stage3):
        vb = v.astype(BF16)
        o = jnp.dot(att, vb, preferred_element_type=F32)
        new_states = []
        for si in range(n_seg):
            st = states[si]
            o_s = _nt(qt, st.astype(BF16))
            if n_seg == 1:
                o = o + o_s
                kh_s = kh
            else:
                in_seg = (row >= si * seg) & (row < (si + 1) * seg)
                o = o + jnp.where(in_seg, o_s, 0.0)
                kh_s = jnp.where(in_seg, kh, 0.0)
            decay = ea[(si + 1) * seg - 1:(si + 1) * seg, :]
            new_states.append(st * decay + _tn(vb, kh_s.astype(BF16)))
        results.append((o, new_states))
    return results


def _lower_bound(lbl_ref, layer):
    lg = lbl_ref[...]
    e = jnp.exp(lg - jnp.max(lg, axis=0, keepdims=True))
    p = e / jnp.sum(e, axis=0, keepdims=True)
    return jnp.sum(p[0:layer + 1], axis=0, keepdims=True) - p[0:1]


def _hgrn_prompt_kernel(q_ref, f_ref, v_ref, lbl_ref, gn_ref, w3_ref, lev_ref, o_ref, s_ref,
                        st_ref, *, layer, n_lev, tl, n_heads):
    li = pl.program_id(2)

    @pl.when(li == 0)
    def _():
        st_ref[...] = jnp.zeros_like(st_ref)

    lb = _lower_bound(lbl_ref, layer)
    w3 = w3_ref[...]
    level = lev_ref[...]
    gain = gn_ref[...]
    head = [slice(hh * B_DIM, (hh + 1) * B_DIM) for hh in range(n_heads)]

    def body(c, carry):
        r0 = pl.multiple_of(c * HGRN_ROWS, HGRN_ROWS)
        rows = pl.ds(r0, HGRN_ROWS)
        items = [(q_ref[0, rows, hs].astype(F32), f_ref[0, rows, hs].astype(F32),
                  v_ref[0, rows, hs].astype(F32), lb[:, hs], [st_ref[hh]])
                 for hh, hs in enumerate(head)]
        for hh, (o, (st,)) in enumerate(_hgrn_blocks(items, w3, level, n_lev, HGRN_ROWS)):
            st_ref[hh] = st
            o_ref[0, rows, head[hh]] = _rmsnorm_rows(o, gain).astype(o_ref.dtype)
        return carry

    lax.fori_loop(0, tl // HGRN_ROWS, body, 0, unroll=2)

    @pl.when(li == pl.num_programs(2) - 1)
    def _():
        for hh in range(n_heads):
            s_ref[0, hh] = st_ref[hh].T


def _hgrn_prompt(z3, col, lb_logits, hgrn_gain, layer, *, tl):
    b, s, _ = z3.shape
    w3, level, n_lev = _hgrn_consts(HGRN_ROWS)
    nh = HGRN_HEADS_PER_STEP
    wblk = nh * B_DIM

    def cspec(name):
        c0 = col[name] // wblk
        return pl.BlockSpec((1, tl, wblk), lambda bb, h, i: (bb, i, c0 + h))

    const = lambda shape: pl.BlockSpec(shape, lambda bb, h, i: (0,) * len(shape))
    return pl.pallas_call(
        functools.partial(_hgrn_prompt_kernel, layer=layer, n_lev=n_lev, tl=tl, n_heads=nh),
        grid=(b, B_HEADS // nh, s // tl),
        in_specs=[cspec("bq"), cspec("bf"), cspec("bi"),
                  pl.BlockSpec((lb_logits.shape[0], wblk), lambda bb, h, i: (0, h)),
                  const((1, B_DIM)), const(w3.shape), const(level.shape)],
        out_specs=[pl.BlockSpec((1, tl, wblk), lambda bb, h, i: (bb, i, h)),
                   pl.BlockSpec((1, nh, B_DIM, B_DIM), lambda bb, h, i: (bb, h, 0, 0))],
        out_shape=[jax.ShapeDtypeStruct((b, s, B_WIDTH), BF16),
                   jax.ShapeDtypeStruct((b, B_HEADS, B_DIM, B_DIM), F32)],
        scratch_shapes=[pltpu.VMEM((nh, B_DIM, B_DIM), F32)],
        compiler_params=_cparams(("parallel", "parallel", "arbitrary")),
        name="hgrn_prompt",
    )(z3, z3, z3, lb_logits, hgrn_gain.reshape(1, B_DIM), w3, level)


def _hgrn_sample_kernel(q_ref, f_ref, v_ref, s0_ref, lbl_ref, gn_ref, w3_ref, lev_ref, acc_ref,
                        o_ref, s_ref, *, layer, n_lev, t_new, n_heads):
    del acc_ref
    lb = _lower_bound(lbl_ref, layer)
    n_seg = HGRN_ROWS // t_new
    head = [slice(hh * B_DIM, (hh + 1) * B_DIM) for hh in range(n_heads)]
    items = [(q_ref[:, hs], f_ref[:, hs], v_ref[:, hs], lb[:, hs],
              [s0_ref[0, si, hh].T for si in range(n_seg)])
             for hh, hs in enumerate(head)]
    results = _hgrn_blocks(items, w3_ref[...], lev_ref[...], n_lev, t_new)
    for hh, (o, new_states) in enumerate(results):
        o_ref[:, head[hh]] = _rmsnorm_rows(o, gn_ref[...]).astype(o_ref.dtype)
        for si in range(n_seg):
            s_ref[0, si, hh] = new_states[si].T


def _hgrn_sample(z_s, state, new_state, lb_logits, hgrn_gain, layer, *, n_seq, t_new):
    w3, level, n_lev = _hgrn_consts(t_new)
    n_seg = HGRN_ROWS // t_new
    nh = HGRN_SAMPLE_HEADS_PER_STEP
    wblk = nh * B_DIM

    def cspec(name):
        c0 = _COL_S[name] // wblk
        return pl.BlockSpec((HGRN_ROWS, wblk), lambda i, h: (i, c0 + h))

    const = lambda shape: pl.BlockSpec(shape, lambda i, h: (0,) * len(shape))
    return pl.pallas_call(
        functools.partial(_hgrn_sample_kernel, layer=layer, n_lev=n_lev, t_new=t_new, n_heads=nh),
        grid=(n_seq // n_seg, B_HEADS // nh),
        in_specs=[cspec("bq"), cspec("bf"), cspec("bi"),
                  pl.BlockSpec((1, n_seg, nh, B_DIM, B_DIM), lambda i, h, layer=layer: (layer, i, h, 0, 0)),
                  pl.BlockSpec((lb_logits.shape[0], wblk), lambda i, h: (0, h)),
                  const((1, B_DIM)), const(w3.shape), const(level.shape),
                  pl.BlockSpec(memory_space=pl.ANY)],
        out_specs=[pl.BlockSpec((HGRN_ROWS, wblk), lambda i, h: (i, h)),
                   pl.BlockSpec((1, n_seg, nh, B_DIM, B_DIM), lambda i, h, layer=layer: (layer, i, h, 0, 0))],
        out_shape=[jax.ShapeDtypeStruct((n_seq * t_new, B_WIDTH), BF16),
                   jax.ShapeDtypeStruct(new_state.shape, F32)],
        input_output_aliases={8: 1},
        compiler_params=_cparams(("parallel", "parallel")),
        name="hgrn_sample",
    )(z_s, z_s, z_s, state, lb_logits, hgrn_gain.reshape(1, B_DIM), w3, level, new_state)


_POOL_PAD = 16


def _pool_mix(ext_at, u, pos, pw_ref, ps_ref):
    outs = []
    for g, win in enumerate(C_WINDOWS):
        gs = slice(g * C_GROUP_DIM, (g + 1) * C_GROUP_DIM)
        acc = u[:, gs]
        for k in range(1, win):
            acc = acc + ext_at(k, gs)
        cnt = jnp.minimum(pos + 1, win).astype(F32)
        pooled = acc * (1.0 / cnt) - u[:, gs]
        mixed = jnp.dot(pooled.astype(BF16), pw_ref[g].astype(BF16), preferred_element_type=F32)
        outs.append(mixed * ps_ref[:, gs])
    return outs


def _pool_prompt_kernel(u_ref, h_ref, pw_ref, ps_ref, o_ref, ext_ref, *, tl):
    i = pl.program_id(1)
    halo = h_ref[0].astype(F32)
    ext_ref[0:_POOL_PAD, :] = jnp.where(i > 0, halo, 0.0)
    u = u_ref[0].astype(F32)
    ext_ref[_POOL_PAD:, :] = u
    pos = i * tl + lax.broadcasted_iota(jnp.int32, (tl, 1), 0)
    outs = _pool_mix(lambda k, gs: ext_ref[_POOL_PAD - k:_POOL_PAD - k + tl, gs], u, pos, pw_ref, ps_ref)
    for g, og in enumerate(outs):
        o_ref[0, :, g * C_GROUP_DIM:(g + 1) * C_GROUP_DIM] = og.astype(o_ref.dtype)


def _pool_prompt(z3, col, pool_w, pool_scale, *, tl):
    b, s, _ = z3.shape
    c0 = col["cu"] // C_WIDTH
    per = tl // _POOL_PAD
    return pl.pallas_call(
        functools.partial(_pool_prompt_kernel, tl=tl),
        grid=(b, s // tl),
        in_specs=[pl.BlockSpec((1, tl, C_WIDTH), lambda bb, i: (bb, i, c0)),
                  pl.BlockSpec((1, _POOL_PAD, C_WIDTH), lambda bb, i: (bb, jnp.maximum(i * per - 1, 0), c0)),
                  pl.BlockSpec((C_GROUPS, C_GROUP_DIM, C_GROUP_DIM), lambda bb, i: (0, 0, 0)),
                  pl.BlockSpec((1, C_WIDTH), lambda bb, i: (0, 0))],
        out_specs=pl.BlockSpec((1, tl, C_WIDTH), lambda bb, i: (bb, i, 0)),
        out_shape=jax.ShapeDtypeStruct((b, s, C_WIDTH), BF16),
        scratch_shapes=[pltpu.VMEM((_POOL_PAD + tl, C_WIDTH), F32)],
        compiler_params=_cparams(("parallel", "arbitrary")),
        name="pool_prompt",
    )(z3, z3, pool_w, pool_scale.reshape(1, C_WIDTH))


def _pool_sample_kernel(u_ref, h_ref, pw_ref, ps_ref, o_ref, ext_ref, *, nb, t_new, first_pos):
    ext_ref[:, 0:1, :] = jnp.zeros((nb, 1, C_WIDTH), F32)
    ext_ref[:, 1:_POOL_PAD, :] = h_ref[0]
    u3 = u_ref[...].reshape(nb, t_new, C_WIDTH)
    ext_ref[:, _POOL_PAD:, :] = u3
    u = u_ref[...]
    pos = first_pos + lax.broadcasted_iota(jnp.int32, (nb * t_new, 1), 0) % t_new

    def ext_at(k, gs):
        return ext_ref[:, _POOL_PAD - k:_POOL_PAD - k + t_new, gs].reshape(nb * t_new, C_GROUP_DIM)

    outs = _pool_mix(ext_at, u, pos, pw_ref, ps_ref)
    for g, og in enumerate(outs):
        o_ref[:, g * C_GROUP_DIM:(g + 1) * C_GROUP_DIM] = og.astype(o_ref.dtype)


def _pool_sample(z_s, state_pool, pool_w, pool_scale, layer, *, n_seq, t_new, first_pos, nb):
    c0 = _COL_S["cu"] // C_WIDTH
    return pl.pallas_call(
        functools.partial(_pool_sample_kernel, nb=nb, t_new=t_new, first_pos=first_pos),
        grid=(n_seq // nb,),
        in_specs=[pl.BlockSpec((nb * t_new, C_WIDTH), lambda i: (i, c0)),
                  pl.BlockSpec((1, nb, C_HIST, C_WIDTH), lambda i, layer=layer: (layer, i, 0, 0)),
                  pl.BlockSpec((C_GROUPS, C_GROUP_DIM, C_GROUP_DIM), lambda i: (0, 0, 0)),
                  pl.BlockSpec((1, C_WIDTH), lambda i: (0, 0))],
        out_specs=pl.BlockSpec((nb * t_new, C_WIDTH), lambda i: (i, 0)),
        out_shape=jax.ShapeDtypeStruct((n_seq * t_new, C_WIDTH), BF16),
        scratch_shapes=[pltpu.VMEM((nb, _POOL_PAD + t_new, C_WIDTH), F32)],
        compiler_params=_cparams(("parallel",)),
        name="pool_sample",
    )(z_s, state_pool, pool_w, pool_scale.reshape(1, C_WIDTH))


def _token_order(ref, scr_ref, d, tm):
    n_chunk = ref.shape[-1] // LANE
    if d == 1:
        return [ref[0, 0, :, c * LANE:(c + 1) * LANE] for c in range(n_chunk)]
    for r in range(d):
        for c in range(n_chunk):
            scr_ref[c, pl.ds(r, tm // d, stride=d), :] = ref[0, r, :, c * LANE:(c + 1) * LANE]
    return [scr_ref[c] for c in range(n_chunk)]


def _out_proj_kernel(x_ref, ag_ref, bg_ref, cg_ref, m0_ref, m1_ref, m2_ref, o0_ref, o1_ref, o2_ref,
                     l0_ref, l1_ref, l2_ref, ob_ref, oc_ref, wa_ref, wb_ref, wc_ref, wo_ref,
                     gp_ref, y_ref, *scratch, dil, tm):
    scr = iter(scratch)
    o_g, l_g = [], []
    for o_ref, l_ref, d in zip((o0_ref, o1_ref, o2_ref), (l0_ref, l1_ref, l2_ref), dil):
        so = next(scr) if d > 1 else None
        sl = next(scr) if d > 1 else None
        o_g.append(_token_order(o_ref, so, d, tm))
        l_g.append(_token_order(l_ref, sl, d, tm)[0])
    heads = []
    for h in range(A_HEADS):
        l0, l1, l2 = (l[:, h:h + 1] for l in l_g)
        mx = jnp.maximum(jnp.maximum(l0, l1), l2)
        e0, e1, e2 = jnp.exp(l0 - mx), jnp.exp(l1 - mx), jnp.exp(l2 - mx)
        inv = 1.0 / (e0 + e1 + e2)
        heads.append((e0 * inv) * o_g[0][h] + (e1 * inv) * o_g[1][h] + (e2 * inv) * o_g[2][h])
    o_a = jnp.concatenate(heads, axis=1)

    def silu_gated(o, h_ref):
        h = h_ref[...].astype(F32)
        return (o.astype(F32) * (h + h * jnp.tanh(h))).astype(BF16)

    def merge_gated(n_ref, y, w_ref):
        p = jnp.dot(y, w_ref[...], preferred_element_type=F32)
        return p + jnp.tanh(n_ref[...].astype(F32)) * p

    merged = (merge_gated(m0_ref, silu_gated(o_a, ag_ref), wa_ref)
              + merge_gated(m1_ref, silu_gated(ob_ref[...], bg_ref), wb_ref)
              + merge_gated(m2_ref, silu_gated(oc_ref[...], cg_ref), wc_ref)) * 0.5
    out = jnp.dot(merged.astype(BF16), wo_ref[...], preferred_element_type=F32)
    y_ref[...] = x_ref[...] + _rmsnorm_rows(out, gp_ref[...])


def _out_proj(x, z, col, o_g, lse_g, o_b, o_c, wa, wb, wc, wo, gain_post, *, tm):
    n = x.shape[0]
    dil = tuple(int(a.shape[1]) for a in o_g)
    tpb = (n // o_g[0].shape[0]) // tm

    def zcol(name, width, k=0):
        assert col[name] % width == 0
        c0 = col[name] // width + k
        return pl.BlockSpec((tm, width), lambda i: (i, c0))

    def gspec(d, width):
        return pl.BlockSpec((1, d, tm // d, width), lambda i: (i // tpb, 0, i % tpb, 0))

    rows = lambda width: pl.BlockSpec((tm, width), lambda i: (i, 0))
    const = lambda a: pl.BlockSpec(a.shape, lambda i: (0,) * a.ndim)
    gp = gain_post.reshape(1, D_MODEL)
    scratch = []
    for d in dil:
        if d > 1:
            scratch += [pltpu.VMEM((A_WIDTH // LANE, tm, LANE), F32), pltpu.VMEM((1, tm, LANE), F32)]
    return pl.pallas_call(
        functools.partial(_out_proj_kernel, dil=dil, tm=tm),
        grid=(n // tm,),
        in_specs=[rows(D_MODEL), zcol("ag", A_WIDTH), zcol("bg", B_WIDTH), zcol("cg", C_WIDTH),
                  zcol("mg", D_MODEL, 0), zcol("mg", D_MODEL, 1), zcol("mg", D_MODEL, 2)]
                 + [gspec(d, A_WIDTH) for d in dil] + [gspec(d, LANE) for d in dil]
                 + [rows(B_WIDTH), rows(C_WIDTH)]
                 + [const(wa), const(wb), const(wc), const(wo), const(gp)],
        out_specs=rows(D_MODEL),
        out_shape=jax.ShapeDtypeStruct((n, D_MODEL), F32),
        scratch_shapes=scratch,
        compiler_params=_cparams(("parallel",)),
        name="out_proj",
    )(x, z, z, z, z, z, z, *o_g, *lse_g, o_b, o_c, wa, wb, wc, wo, gp)


def _ref_cols(w, name, g=None):
    off = _REF_OFF[name]
    if g is None:
        return w[:, off:off + _WIDTHS[name]]
    return w[:, off + g * A_WIDTH:off + (g + 1) * A_WIDTH]


def _prep_w_in(w):
    half = lambda name: _ref_cols(w, name) * 0.5
    parts = []
    for g in range(1, A_GROUPS):
        parts += [_ref_cols(w, "ak", g), _ref_cols(w, "av", g), _ref_cols(w, "aq", g)]
    parts += [_ref_cols(w, "ak", 0), _ref_cols(w, "av", 0), _ref_cols(w, "cu"), half("cg"),
              _ref_cols(w, "aq", 0), half("ag"), _ref_cols(w, "bq"), _ref_cols(w, "bf"),
              _ref_cols(w, "bi"), half("bg"), half("mg")]
    return jnp.concatenate(parts, axis=1).astype(BF16)


def _kv_split(kv):
    return kv.reshape(kv.shape[:-1] + (2, A_HEADS, A_HEAD_DIM))


def kernel(x_prompt, x_sample, cache_kv_w128, cache_kv_w512, cache_kv_w2048, state_hgrn, state_pool,
           norm_pre, norm_post, w_in, hgrn_lb_logits, hgrn_norm, pool_w, pool_scale,
           w_br_a, w_br_b, w_br_c, w_out):
    depth = w_in.shape[0]
    b_p, s_p, _ = x_prompt.shape
    n_seq, t_new, _ = x_sample.shape
    caches = (cache_kv_w128, cache_kv_w512, cache_kv_w2048)
    past_len = cache_kv_w2048.shape[2]
    assert HGRN_ROWS % t_new == 0 and n_seq % (HGRN_ROWS // t_new) == 0
    n_p, n_s = b_p * s_p, n_seq * t_new

    hp = x_prompt.reshape(n_p, D_MODEL)
    hs = x_sample.reshape(n_s, D_MODEL)
    kv_p = [[] for _ in range(A_GROUPS)]
    kv_s = [[] for _ in range(A_GROUPS)]
    hg_p, pl_p, pl_s = [], [], []
    hg_s = jnp.zeros(state_hgrn.shape, F32)
    tm_p = min(1024, s_p)
    for l in range(depth):
        w_l = _prep_w_in(w_in[l])
        wa, wb, wc, wo = (w[l].astype(BF16) for w in (w_br_a, w_br_b, w_br_c, w_out))

        (z_p,) = _in_proj(hp, norm_pre[l], w_l, col0=_P_COL0, tm=min(2048, s_p), tn=512,
                          z_dtype=BF16, f32_cols=0)
        z3 = z_p.reshape(b_p, s_p, _P_WIDTH)

        (u_tail,) = _in_proj(hp, norm_pre[l], w_l, col0=_COL_S["cu"], width=C_WIDTH,
                             n_rows=b_p * _POOL_PAD, row_block=_tail_row_block(s_p, _POOL_PAD, _POOL_PAD),
                             tm=_POOL_PAD, tn=C_WIDTH, z_dtype=F32, f32_cols=0)
        pl_p.append(u_tail.reshape(b_p, _POOL_PAD, C_WIDTH)[:, _POOL_PAD - C_HIST:])
        o_g, lse_g = [], []
        for g, (win, d) in enumerate(A_PATTERNS):
            rows = min(win, s_p)
            kv = _kv_rows(hp, norm_pre[l], w_l, _attn_cols(g)[0], n_seq=b_p, seq=s_p, rows=rows,
                          tm=min(1024, rows))
            kv_p[g].append(kv.reshape(b_p, rows, 2, A_HEADS, A_HEAD_DIM))
            if g == 0:
                arr = z_p.reshape(b_p, 1, s_p, _P_WIDTH)
                cols = (_COL_P["aq"] // A_WIDTH, _COL_P["ak"] // A_WIDTH, _COL_P["av"] // A_WIDTH)
            else:
                arr = _in_proj_perm(hp, norm_pre[l], w_l, g, b=b_p, s=s_p, d=d, tm=tm_p)
                cols = (2, 0, 1)
            o, lse = _attn_prompt(arr, cols, tq=min(1024, s_p // d))
            o_g.append(o)
            lse_g.append(lse)
        o_b, s_new = _hgrn_prompt(z3, _COL_P, hgrn_lb_logits, hgrn_norm[l], l, tl=min(1024, s_p))
        hg_p.append(s_new)
        o_c = _pool_prompt(z3, _COL_P, pool_w[l], pool_scale[l], tl=min(1024, s_p))
        hp = _out_proj(hp, z_p, _COL_P, o_g, lse_g, o_b.reshape(n_p, B_WIDTH),
                       o_c.reshape(n_p, C_WIDTH), wa, wb, wc, wo, norm_post[l], tm=min(512, s_p))

        (z_s,) = _in_proj(hs, norm_pre[l], w_l, col0=0, tm=min(1024, n_s), tn=1024, z_dtype=F32,
                          f32_cols=0)
        o_g, lse_g = _attn_sample(z_s, caches, l, n_seq=n_seq, t_new=t_new)
        zs3 = z_s.reshape(n_seq, t_new, IN_WIDTH)
        for g in range(A_GROUPS):
            k_off, v_off, _ = _attn_cols(g)
            kv_s[g].append(_kv_split(zs3[:, :, k_off:v_off + A_WIDTH]))
        o_b, hg_s = _hgrn_sample(z_s, state_hgrn, hg_s, hgrn_lb_logits, hgrn_norm[l], l,
                                 n_seq=n_seq, t_new=t_new)
        o_c = _pool_sample(z_s, state_pool, pool_w[l], pool_scale[l], l, n_seq=n_seq,
                           t_new=t_new, first_pos=past_len, nb=HGRN_ROWS // t_new)
        u_new = zs3[:, :, _COL_S["cu"]:_COL_S["cu"] + C_WIDTH]
        pl_s.append(jnp.concatenate([state_pool[l], u_new], axis=1)[:, -C_HIST:])
        as4 = lambda a: a.reshape(1, 1, n_s, a.shape[-1])
        hs = _out_proj(hs, z_s, _COL_S, [as4(a) for a in o_g], [as4(a) for a in lse_g], o_b, o_c,
                       wa, wb, wc, wo, norm_post[l], tm=min(256, n_s))

    return (hp.reshape(b_p, s_p, D_MODEL), hs.reshape(n_seq, t_new, D_MODEL),
            jnp.stack(kv_p[0]), jnp.stack(kv_p[1]), jnp.stack(kv_p[2]),
            jnp.stack(kv_s[0]), jnp.stack(kv_s[1]), jnp.stack(kv_s[2]),
            jnp.stack(hg_p), hg_s, jnp.stack(pl_p), jnp.stack(pl_s))
```

```python
import functools

import numpy as np
import jax
import jax.numpy as jnp
from jax import lax
from jax.experimental import pallas as pl
from jax.experimental.pallas import tpu as pltpu

F32 = jnp.float32
BF16 = jnp.bfloat16

D_MODEL = 1024
A_PATTERNS = ((128, 1), (512, 4), (2048, 16))
A_GROUPS = 3
A_HEADS = 4
A_HEAD_DIM = 128
A_WIDTH = A_HEADS * A_HEAD_DIM
N_STEPS = 128
B_HEADS = 8
B_DIM = 128
B_WIDTH = B_HEADS * B_DIM
C_WINDOWS = (2, 4, 8, 16)
C_GROUPS = 4
C_GROUP_DIM = 128
C_WIDTH = C_GROUPS * C_GROUP_DIM
C_HIST = 15
N_BRANCH = 3
EPS = 1e-6
NEG = -1e30
IN_WIDTH = 13312
LANE = 128
HGRN_ROWS = 128
HGRN_HEADS_PER_STEP = 8
HGRN_SAMPLE_HEADS_PER_STEP = 4

_REF_OFF = dict(aq=0, ak=1536, av=3072, ag=4608, bq=5120, bf=6144, bi=7168, bg=8192,
                cu=9216, cg=9728, mg=10240)
_WIDTHS = dict(aq=1536, ak=1536, av=1536, ag=512, bq=1024, bf=1024, bi=1024, bg=1024,
               cu=512, cg=512, mg=3072)

_PERM_WIDTH = 3 * A_WIDTH
_P_COL0 = (A_GROUPS - 1) * _PERM_WIDTH
_COL_P = dict(ak=0, av=512, cu=1024, cg=1536, aq=2048, ag=2560, bq=3072, bf=4096,
              bi=5120, bg=6144, mg=7168)
_P_WIDTH = IN_WIDTH - _P_COL0
_COL_S = {name: _P_COL0 + off for name, off in _COL_P.items()}


def _attn_cols(g):
    if g == 0:
        return _COL_S["ak"], _COL_S["av"], _COL_S["aq"]
    base = (g - 1) * _PERM_WIDTH
    return base, base + A_WIDTH, base + 2 * A_WIDTH

_VMEM_LIMIT = 56 * 1024 * 1024


def _cparams(sem):
    return pltpu.CompilerParams(dimension_semantics=sem, vmem_limit_bytes=_VMEM_LIMIT)


def _nt(a, b):
    return lax.dot_general(a, b, (((1,), (1,)), ((), ())), preferred_element_type=F32)


def _tn(a, b):
    return lax.dot_general(a, b, (((0,), (0,)), ((), ())), preferred_element_type=F32)


def _sigmoid(x):
    return 1.0 / (1.0 + jnp.exp(-x))


def _rmsnorm_rows(x, gain):
    return x * lax.rsqrt(jnp.mean(x * x, axis=-1, keepdims=True) + EPS) * gain


def _in_proj_kernel(x_ref, g_ref, w_ref, *refs, n_f32_tiles, z_dtype):
    if n_f32_tiles:
        z_ref, zf_ref, xn_ref = refs
    else:
        z_ref, xn_ref = refs
    j = pl.program_id(1)

    @pl.when(j == 0)
    def _():
        xn_ref[...] = _rmsnorm_rows(x_ref[...], g_ref[...]).astype(BF16)

    acc = jnp.dot(xn_ref[...], w_ref[...], preferred_element_type=F32)
    z_ref[...] = acc.astype(z_dtype)
    if n_f32_tiles:
        @pl.when(j < n_f32_tiles)
        def _():
            zf_ref[...] = acc


def _tail_row_block(seq, rows, tm):
    per_seq, tiles, first = seq // tm, rows // tm, (seq - rows) // tm
    assert seq % tm == 0 and rows % tm == 0
    return lambda i: (i // tiles) * per_seq + first + i % tiles


def _in_proj(x, gain, w_bf16, *, col0, tm, tn, z_dtype, f32_cols, width=None, n_rows=None,
             row_block=lambda i: i):
    n = x.shape[0] if n_rows is None else n_rows
    width = w_bf16.shape[1] - col0 if width is None else width
    j0 = col0 // tn
    n_f32_tiles = f32_cols // tn
    out_shape = [jax.ShapeDtypeStruct((n, width), z_dtype)]
    out_specs = [pl.BlockSpec((tm, tn), lambda i, j: (i, j))]
    if n_f32_tiles:
        out_shape.append(jax.ShapeDtypeStruct((n, f32_cols), F32))
        out_specs.append(pl.BlockSpec((tm, tn), lambda i, j: (i, jnp.minimum(j, n_f32_tiles - 1))))
    return pl.pallas_call(
        functools.partial(_in_proj_kernel, n_f32_tiles=n_f32_tiles, z_dtype=z_dtype),
        grid=(n // tm, width // tn),
        in_specs=[pl.BlockSpec((tm, D_MODEL), lambda i, j: (row_block(i), 0)),
                  pl.BlockSpec((1, D_MODEL), lambda i, j: (0, 0)),
                  pl.BlockSpec((D_MODEL, tn), lambda i, j: (0, j + j0))],
        out_specs=out_specs,
        out_shape=out_shape,
        scratch_shapes=[pltpu.VMEM((tm, D_MODEL), BF16)],
        compiler_params=_cparams(("parallel", "arbitrary")),
        name="in_proj",
    )(x, gain.reshape(1, D_MODEL), w_bf16)


def _kv_rows_kernel(x_ref, g_ref, wk_ref, wv_ref, o_ref, *, tm):
    xn = _rmsnorm_rows(x_ref[...], g_ref[...]).astype(BF16)
    per_tok = 2 * A_HEADS
    for kv, w_ref in enumerate((wk_ref, wv_ref)):
        acc = jnp.dot(xn, w_ref[...], preferred_element_type=F32)
        for h in range(A_HEADS):
            o_ref[pl.ds(kv * A_HEADS + h, tm, stride=per_tok), :] = (
                acc[:, h * A_HEAD_DIM:(h + 1) * A_HEAD_DIM])


def _kv_rows(x, gain, w_bf16, k_col, *, n_seq, seq, rows, tm):
    n = n_seq * rows
    per_tok = 2 * A_HEADS
    kb = k_col // A_WIDTH
    row_block = _tail_row_block(seq, rows, tm)
    return pl.pallas_call(
        functools.partial(_kv_rows_kernel, tm=tm),
        grid=(n // tm,),
        in_specs=[pl.BlockSpec((tm, D_MODEL), lambda i: (row_block(i), 0)),
                  pl.BlockSpec((1, D_MODEL), lambda i: (0, 0)),
                  pl.BlockSpec((D_MODEL, A_WIDTH), lambda i: (0, kb)),
                  pl.BlockSpec((D_MODEL, A_WIDTH), lambda i: (0, kb + 1))],
        out_specs=pl.BlockSpec((tm * per_tok, A_HEAD_DIM), lambda i: (i, 0)),
        out_shape=jax.ShapeDtypeStruct((n * per_tok, A_HEAD_DIM), F32),
        compiler_params=_cparams(("parallel",)),
        name="kv_rows",
    )(x, gain.reshape(1, D_MODEL), w_bf16, w_bf16)


def _in_proj_perm_kernel(x_ref, g_ref, w_ref, r_ref, xs_ref, xn_ref, *, d, tm):
    xn = _rmsnorm_rows(x_ref[...], g_ref[...])
    n_chunk = D_MODEL // LANE
    rows = tm // d
    for c in range(n_chunk):
        xs_ref[c] = xn[:, c * LANE:(c + 1) * LANE]
    for r in range(d):
        for c in range(n_chunk):
            xn_ref[r * rows:(r + 1) * rows, c * LANE:(c + 1) * LANE] = (
                xs_ref[c, pl.ds(r, rows, stride=d), :].astype(BF16))
    acc = jnp.dot(xn_ref[...], w_ref[...], preferred_element_type=F32)
    r_ref[0] = acc.astype(BF16).reshape(d, rows, _PERM_WIDTH)


def _in_proj_perm(x, gain, w_bf16, g, *, b, s, d, tm):
    tpb = s // tm
    rows = tm // d
    idx = lambda i: (i // tpb, 0, i % tpb, 0)
    return pl.pallas_call(
        functools.partial(_in_proj_perm_kernel, d=d, tm=tm),
        grid=(b * tpb,),
        in_specs=[pl.BlockSpec((tm, D_MODEL), lambda i: (i, 0)),
                  pl.BlockSpec((1, D_MODEL), lambda i: (0, 0)),
                  pl.BlockSpec((D_MODEL, _PERM_WIDTH), lambda i: (0, g - 1))],
        out_specs=pl.BlockSpec((1, d, rows, _PERM_WIDTH), idx),
        out_shape=jax.ShapeDtypeStruct((b, d, s // d, _PERM_WIDTH), BF16),
        scratch_shapes=[pltpu.VMEM((D_MODEL // LANE, tm, LANE), F32),
                        pltpu.VMEM((tm, D_MODEL), BF16)],
        compiler_params=_cparams(("parallel",)),
        name="in_proj_d%d" % d,
    )(x, gain.reshape(1, D_MODEL), w_bf16)


def _attn_prompt_kernel(q_ref, kp_ref, kc_ref, vp_ref, vc_ref, o_ref, lse_ref, *, tq):
    i = pl.program_id(2)
    n = N_STEPS
    qi = lax.broadcasted_iota(jnp.int32, (n, 2 * n), 0)
    kj = lax.broadcasted_iota(jnp.int32, (n, 2 * n), 1)
    band = (kj >= qi) & (kj <= qi + n)
    band_first = band & ((i > 0) | (kj >= n))
    scale = A_HEAD_DIM ** -0.5
    lane = lax.broadcasted_iota(jnp.int32, (n, LANE), 1)
    for rr, j in [(rr, j) for rr in range(q_ref.shape[1]) for j in range(tq // n)]:
        rows = slice(j * n, (j + 1) * n)
        lse = jnp.zeros((n, LANE), F32)
        for h in range(A_HEADS):
            hs = slice(h * A_HEAD_DIM, (h + 1) * A_HEAD_DIM)
            q = q_ref[0, rr, rows, hs]
            if j == 0:
                k = jnp.concatenate([kp_ref[0, rr, :, hs], kc_ref[0, rr, rows, hs]], axis=0)
                v = jnp.concatenate([vp_ref[0, rr, :, hs], vc_ref[0, rr, rows, hs]], axis=0)
                valid = band_first
            else:
                k = kc_ref[0, rr, (j - 1) * n:(j + 1) * n, hs]
                v = vc_ref[0, rr, (j - 1) * n:(j + 1) * n, hs]
                valid = band
            s = jnp.where(valid, _nt(q, k) * scale, NEG)
            m = jnp.max(s, axis=-1, keepdims=True)
            p = jnp.exp(s - m)
            l = jnp.sum(p, axis=-1, keepdims=True)
            o = jnp.dot(p.astype(BF16), v, preferred_element_type=F32)
            o_ref[0, rr, rows, hs] = o / l
            lse = jnp.where(lane == h, m + jnp.log(l), lse)
        lse_ref[0, rr, rows, :] = lse


def _attn_prompt(arr, cols, *, tq, rpb):
    b, d, rows, _ = arr.shape
    qc, kc, vc = cols
    prev_per_blk = tq // N_STEPS

    def cur(c):
        return pl.BlockSpec((1, rpb, tq, A_WIDTH), lambda bb, r, i: (bb, r, i, c))

    def prev(c):
        return pl.BlockSpec((1, rpb, N_STEPS, A_WIDTH),
                            lambda bb, r, i: (bb, r, jnp.maximum(i * prev_per_blk - 1, 0), c))

    out_spec = pl.BlockSpec((1, rpb, tq, A_WIDTH), lambda bb, r, i: (bb, r, i, 0))
    return pl.pallas_call(
        functools.partial(_attn_prompt_kernel, tq=tq),
        grid=(b, d // rpb, rows // tq),
        in_specs=[cur(qc), prev(kc), cur(kc), prev(vc), cur(vc)],
        out_specs=[out_spec, pl.BlockSpec((1, rpb, tq, LANE), lambda bb, r, i: (bb, r, i, 0))],
        out_shape=[jax.ShapeDtypeStruct((b, d, rows, A_WIDTH), F32),
                   jax.ShapeDtypeStruct((b, d, rows, LANE), F32)],
        compiler_params=_cparams(("parallel", "parallel", "arbitrary")),
        name="attn_prompt_d%d" % d,
    )(arr, arr, arr, arr, arr)


def _gather_pitch(n_j):
    p = -(-n_j // 4)
    return 4 * (p if p % 2 else p + 1)


def _attn_sample_kernel(z_ref, c0_ref, c1_ref, c2_ref, *refs, t_new):
    out_refs, flat_refs = refs[:2 * A_GROUPS], refs[2 * A_GROUPS:]
    scale = A_HEAD_DIM ** -0.5
    nrow = A_HEADS * t_new
    per_res = 2 * A_HEADS
    for g, c_ref in enumerate((c0_ref, c1_ref, c2_ref)):
        d = A_PATTERNS[g][1]
        n_res = min(d, t_new)
        o_ref, lse_ref = out_refs[2 * g], out_refs[2 * g + 1]
        k_off, v_off, q_off = _attn_cols(g)
        q = z_ref[:, q_off:q_off + A_WIDTH]
        kn = z_ref[:, k_off:k_off + A_WIDTH]
        vn = z_ref[:, v_off:v_off + A_WIDTH]
        n_j = n_res * per_res
        pitch = _gather_pitch(n_j)
        flat_ref = flat_refs[g]
        for mm in range(N_STEPS):
            flat_ref[mm * pitch:mm * pitch + n_j, :] = c_ref[mm]
        gather = lambda j: flat_ref[pl.ds(j, N_STEPS, stride=pitch), :]
        kcat = jnp.concatenate([gather(r * per_res + h)
                                for r in range(n_res) for h in range(A_HEADS)], axis=1).astype(BF16)
        vcat = jnp.concatenate([gather(r * per_res + A_HEADS + h)
                                for r in range(n_res) for h in range(A_HEADS)], axis=1).astype(BF16)
        wide = n_res * A_WIDTH
        q4 = jnp.concatenate([q] * A_HEADS, axis=0)
        row = lax.broadcasted_iota(jnp.int32, (nrow, wide), 0)
        col = lax.broadcasted_iota(jnp.int32, (nrow, wide), 1)
        row_h, row_t = row // t_new, row % t_new
        sel = ((col // A_WIDTH) == (row_t % d)) & (((col % A_WIDTH) // A_HEAD_DIM) == row_h)
        qbd = jnp.where(sel, jnp.concatenate([q4] * n_res, axis=1), 0.0).astype(BF16)
        s_c = _nt(qbd, kcat) * scale
        r2 = lax.broadcasted_iota(jnp.int32, (nrow, N_STEPS), 0) % t_new
        c2 = lax.broadcasted_iota(jnp.int32, (nrow, N_STEPS), 1)
        s_c = jnp.where(c2 >= r2 // d, s_c, NEG)
        pad = jnp.zeros((N_STEPS - t_new, A_WIDTH), F32)
        knp = jnp.concatenate([kn, pad], axis=0).astype(BF16)
        vnp = jnp.concatenate([vn, pad], axis=0).astype(BF16)
        rowq = lax.broadcasted_iota(jnp.int32, (nrow, A_WIDTH), 0) // t_new
        colq = lax.broadcasted_iota(jnp.int32, (nrow, A_WIDTH), 1) // A_HEAD_DIM
        selq = rowq == colq
        qbd2 = jnp.where(selq, q4, 0.0).astype(BF16)
        s_n = _nt(qbd2, knp) * scale
        s_n = jnp.where((c2 <= r2) & ((r2 - c2) % d == 0), s_n, NEG)
        m = jnp.maximum(jnp.max(s_c, axis=-1, keepdims=True), jnp.max(s_n, axis=-1, keepdims=True))
        p_c = jnp.exp(s_c - m)
        p_n = jnp.exp(s_n - m)
        l = jnp.sum(p_c, axis=-1, keepdims=True) + jnp.sum(p_n, axis=-1, keepdims=True)
        o_wide = jnp.where(sel, jnp.dot(p_c.astype(BF16), vcat, preferred_element_type=F32), 0.0)
        o_new = jnp.where(selq, jnp.dot(p_n.astype(BF16), vnp, preferred_element_type=F32), 0.0)
        o = o_new[:, 0:A_HEAD_DIM]
        for c in range(1, A_HEADS):
            o = o + o_new[:, c * A_HEAD_DIM:(c + 1) * A_HEAD_DIM]
        for c in range(wide // A_HEAD_DIM):
            o = o + o_wide[:, c * A_HEAD_DIM:(c + 1) * A_HEAD_DIM]
        o = o / l
        lse = m + jnp.log(l)
        lane = lax.broadcasted_iota(jnp.int32, (t_new, LANE), 1)
        lse_out = jnp.zeros((t_new, LANE), F32)
        for h in range(A_HEADS):
            hs = slice(h * A_HEAD_DIM, (h + 1) * A_HEAD_DIM)
            o_ref[:, hs] = o[h * t_new:(h + 1) * t_new]
            lse_out = jnp.where(lane == h, lse[h * t_new:(h + 1) * t_new], lse_out)
        lse_ref[...] = lse_out


def _attn_sample(z_s, caches, layer, *, n_seq, t_new):
    in_specs = [pl.BlockSpec((t_new, IN_WIDTH), lambda b: (b, 0))]
    views, scratch = [], []
    per_res = 2 * A_HEADS
    for g, c in enumerate(caches):
        win, d = A_PATTERNS[g]
        assert c.shape[2] == win == N_STEPS * d, "cache must hold one full window"
        n_res = min(d, t_new)
        views.append(c.reshape(c.shape[0], n_seq, N_STEPS, d * per_res, A_HEAD_DIM))
        in_specs.append(pl.BlockSpec((None, None, N_STEPS, n_res * per_res, A_HEAD_DIM),
                                     lambda b, layer=layer: (layer, b, 0, 0, 0)))
        scratch.append(pltpu.VMEM((N_STEPS * _gather_pitch(n_res * per_res), A_HEAD_DIM), F32))
    out_specs = [pl.BlockSpec((t_new, A_WIDTH), lambda b: (b, 0)),
                 pl.BlockSpec((t_new, LANE), lambda b: (b, 0))] * A_GROUPS
    out_shape = [jax.ShapeDtypeStruct((n_seq * t_new, A_WIDTH), F32),
                 jax.ShapeDtypeStruct((n_seq * t_new, LANE), F32)] * A_GROUPS
    outs = pl.pallas_call(
        functools.partial(_attn_sample_kernel, t_new=t_new),
        grid=(n_seq,),
        in_specs=in_specs,
        out_specs=out_specs,
        out_shape=out_shape,
        scratch_shapes=scratch,
        compiler_params=_cparams(("parallel",)),
        name="attn_sample",
    )(z_s, *views)
    return outs[0::2], outs[1::2]


def _hgrn_consts(seg):
    n = HGRN_ROWS
    t = np.arange(n)[:, None]
    u = np.arange(n)[None, :]
    seg_lo = (t // seg) * seg
    seg_hi = seg_lo + seg - 1
    mats = [(u >= seg_lo) & (u <= t),
            (u > t) & (u <= seg_hi)]
    level = np.full((n, n), -1, np.int32)
    s = u
    li = 0
    m = seg
    while m >= 2:
        half = m // 2
        upper = (t % m) >= half
        ref = (t // m) * m + half - 1
        if m > 2:
            mats.append(np.where(upper, (u > ref) & (u <= t), (u > t) & (u <= ref)))
        pair = (t // m == s // m) & ((t % m) >= half) & ((s % m) < half)
        level[pair] = li
        li += 1
        m //= 2
    level[np.arange(n), np.arange(n)] = li
    w = np.concatenate(mats, axis=0).astype(np.float32)
    w2 = np.concatenate([w, w], axis=1)
    return jnp.asarray(w2, BF16), jnp.asarray(level), li


def _hgrn_blocks(items, w2, level, n_lev, seg):
    n = HGRN_ROWS
    n_seg = n // seg
    row = lax.broadcasted_iota(jnp.int32, (n, B_DIM), 0)

    gates = []
    for q, xf, v, lb, states in items:
        f = lb + (1.0 - lb) * _sigmoid(xf)
        g = jnp.log2(f)
        g_hi = g.astype(BF16)
        g_lo = (g - g_hi.astype(F32)).astype(BF16)
        gates.append((f, jnp.concatenate([g_hi, g_lo], axis=0)))
    stage1 = []
    for i in range(0, len(items), 2):
        pair = gates[i:i + 2]
        rhs = pair[0][1] if len(pair) == 1 else jnp.concatenate([pair[0][1], pair[1][1]], axis=1)
        dsum = jnp.dot(w2, rhs, preferred_element_type=F32)
        for k, (f, _) in enumerate(pair):
            stage1.append((f, 1.0 - f, dsum[:, k * B_DIM:(k + 1) * B_DIM]))

    stage2 = []
    for (q, xf, v, lb, states), (f, kin, dsum) in zip(items, stage1):
        fac = jnp.exp2(dsum)
        ea = fac[0:n]
        ops = []
        m = seg
        for li in range(n_lev):
            half = m // 2
            if half % 8 == 0:
                src = jnp.concatenate([(kin if (r0 // half) % 2 == 0 else q)[r0:r0 + half]
                                       for r0 in range(0, n, half)], axis=0)
                tm = src * fac[(2 + li) * n:(3 + li) * n]
            elif m > 2:
                tm = jnp.where((row & half) != 0, q, kin) * fac[(2 + li) * n:(3 + li) * n]
            else:
                tm = jnp.where((row & half) != 0, q * f, kin)
            ops.append(tm.astype(BF16))
            m //= 2
        stage2.append((ea, (q * ea).astype(BF16), kin * fac[n:2 * n], ops))

    stage3 = []
    in_level = [level == li for li in range(n_lev + 1)]
    for (q, xf, v, lb, states), (f, kin, dsum), (ea, qt, kh, ops) in zip(items, stage1, stage2):
        att = jnp.zeros((n, n), F32)
        for li, tm in enumerate(ops):
            att = jnp.where(in_level[li], _nt(tm, tm), att)
        att = jnp.where(in_level[n_lev], _nt(q.astype(BF16), kin.astype(BF16)), att)
        stage3.append(att.astype(BF16))

    results = []
    for (q, xf, v, lb, states), (ea, qt, kh, ops), att in zip(items, stage2, stage3):
        vb = v.astype(BF16)
        o = jnp.dot(att, vb, preferred_element_type=F32)
        new_states = []
        for si in range(n_seg):
            st = states[si]
            o_s = _nt(qt, st.astype(BF16))
            if n_seg == 1:
                o = o + o_s
                kh_s = kh
            else:
                in_seg = (row >= si * seg) & (row < (si + 1) * seg)
                o = o + jnp.where(in_seg, o_s, 0.0)
                kh_s = jnp.where(in_seg, kh, 0.0)
            decay = ea[(si + 1) * seg - 1:(si + 1) * seg, :]
            new_states.append(st * decay + _tn(vb, kh_s.astype(BF16)))
        results.append((o, new_states))
    return results


def _lower_bound(lbl_ref, layer):
    lg = lbl_ref[...]
    e = jnp.exp(lg - jnp.max(lg, axis=0, keepdims=True))
    p = e / jnp.sum(e, axis=0, keepdims=True)
    return jnp.sum(p[0:layer + 1], axis=0, keepdims=True) - p[0:1]


def _hgrn_prompt_kernel(q_ref, f_ref, v_ref, lbl_ref, gn_ref, w3_ref, lev_ref, o_ref, s_ref,
                        st_ref, *, layer, n_lev, tl, n_heads):
    li = pl.program_id(2)

    @pl.when(li == 0)
    def _():
        st_ref[...] = jnp.zeros_like(st_ref)

    lb = _lower_bound(lbl_ref, layer)
    w3 = w3_ref[...]
    level = lev_ref[...]
    gain = gn_ref[...]
    head = [slice(hh * B_DIM, (hh + 1) * B_DIM) for hh in range(n_heads)]

    def body(c, carry):
        r0 = pl.multiple_of(c * HGRN_ROWS, HGRN_ROWS)
        rows = pl.ds(r0, HGRN_ROWS)
        items = [(q_ref[0, rows, hs].astype(F32), f_ref[0, rows, hs].astype(F32),
                  v_ref[0, rows, hs].astype(F32), lb[:, hs], [st_ref[hh]])
                 for hh, hs in enumerate(head)]
        for hh, (o, (st,)) in enumerate(_hgrn_blocks(items, w3, level, n_lev, HGRN_ROWS)):
            st_ref[hh] = st
            o_ref[0, rows, head[hh]] = _rmsnorm_rows(o, gain).astype(o_ref.dtype)
        return carry

    lax.fori_loop(0, tl // HGRN_ROWS, body, 0, unroll=2)

    @pl.when(li == pl.num_programs(2) - 1)
    def _():
        for hh in range(n_heads):
            s_ref[0, hh] = st_ref[hh].T


def _hgrn_prompt(z3, col, lb_logits, hgrn_gain, layer, *, tl):
    b, s, _ = z3.shape
    w3, level, n_lev = _hgrn_consts(HGRN_ROWS)
    nh = HGRN_HEADS_PER_STEP
    wblk = nh * B_DIM

    def cspec(name):
        c0 = col[name] // wblk
        return pl.BlockSpec((1, tl, wblk), lambda bb, h, i: (bb, i, c0 + h))

    const = lambda shape: pl.BlockSpec(shape, lambda bb, h, i: (0,) * len(shape))
    return pl.pallas_call(
        functools.partial(_hgrn_prompt_kernel, layer=layer, n_lev=n_lev, tl=tl, n_heads=nh),
        grid=(b, B_HEADS // nh, s // tl),
        in_specs=[cspec("bq"), cspec("bf"), cspec("bi"),
                  pl.BlockSpec((lb_logits.shape[0], wblk), lambda bb, h, i: (0, h)),
                  const((1, B_DIM)), const(w3.shape), const(level.shape)],
        out_specs=[pl.BlockSpec((1, tl, wblk), lambda bb, h, i: (bb, i, h)),
                   pl.BlockSpec((1, nh, B_DIM, B_DIM), lambda bb, h, i: (bb, h, 0, 0))],
        out_shape=[jax.ShapeDtypeStruct((b, s, B_WIDTH), BF16),
                   jax.ShapeDtypeStruct((b, B_HEADS, B_DIM, B_DIM), F32)],
        scratch_shapes=[pltpu.VMEM((nh, B_DIM, B_DIM), F32)],
        compiler_params=_cparams(("parallel", "parallel", "arbitrary")),
        name="hgrn_prompt",
    )(z3, z3, z3, lb_logits, hgrn_gain.reshape(1, B_DIM), w3, level)


def _hgrn_sample_kernel(q_ref, f_ref, v_ref, s0_ref, lbl_ref, gn_ref, w3_ref, lev_ref, acc_ref,
                        o_ref, s_ref, *, layer, n_lev, t_new, n_heads):
    del acc_ref
    lb = _lower_bound(lbl_ref, layer)
    n_seg = HGRN_ROWS // t_new
    head = [slice(hh * B_DIM, (hh + 1) * B_DIM) for hh in range(n_heads)]
    items = [(q_ref[:, hs], f_ref[:, hs], v_ref[:, hs], lb[:, hs],
              [s0_ref[0, si, hh].T for si in range(n_seg)])
             for hh, hs in enumerate(head)]
    results = _hgrn_blocks(items, w3_ref[...], lev_ref[...], n_lev, t_new)
    for hh, (o, new_states) in enumerate(results):
        o_ref[:, head[hh]] = _rmsnorm_rows(o, gn_ref[...]).astype(o_ref.dtype)
        for si in range(n_seg):
            s_ref[0, si, hh] = new_states[si].T


def _hgrn_sample(z_s, state, new_state, lb_logits, hgrn_gain, layer, *, n_seq, t_new):
    w3, level, n_lev = _hgrn_consts(t_new)
    n_seg = HGRN_ROWS // t_new
    nh = HGRN_SAMPLE_HEADS_PER_STEP
    wblk = nh * B_DIM

    def cspec(name):
        c0 = _COL_S[name] // wblk
        return pl.BlockSpec((HGRN_ROWS, wblk), lambda i, h: (i, c0 + h))

    const = lambda shape: pl.BlockSpec(shape, lambda i, h: (0,) * len(shape))
    return pl.pallas_call(
        functools.partial(_hgrn_sample_kernel, layer=layer, n_lev=n_lev, t_new=t_new, n_heads=nh),
        grid=(n_seq // n_seg, B_HEADS // nh),
        in_specs=[cspec("bq"), cspec("bf"), cspec("bi"),
                  pl.BlockSpec((1, n_seg, nh, B_DIM, B_DIM), lambda i, h, layer=layer: (layer, i, h, 0, 0)),
                  pl.BlockSpec((lb_logits.shape[0], wblk), lambda i, h: (0, h)),
                  const((1, B_DIM)), const(w3.shape), const(level.shape),
                  pl.BlockSpec(memory_space=pl.ANY)],
        out_specs=[pl.BlockSpec((HGRN_ROWS, wblk), lambda i, h: (i, h)),
                   pl.BlockSpec((1, n_seg, nh, B_DIM, B_DIM), lambda i, h, layer=layer: (layer, i, h, 0, 0))],
        out_shape=[jax.ShapeDtypeStruct((n_seq * t_new, B_WIDTH), BF16),
                   jax.ShapeDtypeStruct(new_state.shape, F32)],
        input_output_aliases={8: 1},
        compiler_params=_cparams(("parallel", "parallel")),
        name="hgrn_sample",
    )(z_s, z_s, z_s, state, lb_logits, hgrn_gain.reshape(1, B_DIM), w3, level, new_state)


_POOL_PAD = 16


def _pool_mix(ext_at, u, pos, pw_ref, ps_ref):
    outs = []
    for g, win in enumerate(C_WINDOWS):
        gs = slice(g * C_GROUP_DIM, (g + 1) * C_GROUP_DIM)
        acc = u[:, gs]
        for k in range(1, win):
            acc = acc + ext_at(k, gs)
        cnt = jnp.minimum(pos + 1, win).astype(F32)
        pooled = acc * (1.0 / cnt) - u[:, gs]
        mixed = jnp.dot(pooled.astype(BF16), pw_ref[g].astype(BF16), preferred_element_type=F32)
        outs.append(mixed * ps_ref[:, gs])
    return outs


def _pool_prompt_kernel(u_ref, h_ref, pw_ref, ps_ref, o_ref, ext_ref, *, tl):
    i = pl.program_id(1)
    halo = h_ref[0].astype(F32)
    ext_ref[0:_POOL_PAD, :] = jnp.where(i > 0, halo, 0.0)
    u = u_ref[0].astype(F32)
    ext_ref[_POOL_PAD:, :] = u
    pos = i * tl + lax.broadcasted_iota(jnp.int32, (tl, 1), 0)
    outs = _pool_mix(lambda k, gs: ext_ref[_POOL_PAD - k:_POOL_PAD - k + tl, gs], u, pos, pw_ref, ps_ref)
    for g, og in enumerate(outs):
        o_ref[0, :, g * C_GROUP_DIM:(g + 1) * C_GROUP_DIM] = og.astype(o_ref.dtype)


def _pool_prompt(z3, col, pool_w, pool_scale, *, tl):
    b, s, _ = z3.shape
    c0 = col["cu"] // C_WIDTH
    per = tl // _POOL_PAD
    return pl.pallas_call(
        functools.partial(_pool_prompt_kernel, tl=tl),
        grid=(b, s // tl),
        in_specs=[pl.BlockSpec((1, tl, C_WIDTH), lambda bb, i: (bb, i, c0)),
                  pl.BlockSpec((1, _POOL_PAD, C_WIDTH), lambda bb, i: (bb, jnp.maximum(i * per - 1, 0), c0)),
                  pl.BlockSpec((C_GROUPS, C_GROUP_DIM, C_GROUP_DIM), lambda bb, i: (0, 0, 0)),
                  pl.BlockSpec((1, C_WIDTH), lambda bb, i: (0, 0))],
        out_specs=pl.BlockSpec((1, tl, C_WIDTH), lambda bb, i: (bb, i, 0)),
        out_shape=jax.ShapeDtypeStruct((b, s, C_WIDTH), BF16),
        scratch_shapes=[pltpu.VMEM((_POOL_PAD + tl, C_WIDTH), F32)],
        compiler_params=_cparams(("parallel", "arbitrary")),
        name="pool_prompt",
    )(z3, z3, pool_w, pool_scale.reshape(1, C_WIDTH))


def _pool_sample_kernel(u_ref, h_ref, pw_ref, ps_ref, o_ref, ext_ref, *, nb, t_new, first_pos):
    ext_ref[:, 0:1, :] = jnp.zeros((nb, 1, C_WIDTH), F32)
    ext_ref[:, 1:_POOL_PAD, :] = h_ref[0]
    u3 = u_ref[...].reshape(nb, t_new, C_WIDTH)
    ext_ref[:, _POOL_PAD:, :] = u3
    u = u_ref[...]
    pos = first_pos + lax.broadcasted_iota(jnp.int32, (nb * t_new, 1), 0) % t_new

    def ext_at(k, gs):
        return ext_ref[:, _POOL_PAD - k:_POOL_PAD - k + t_new, gs].reshape(nb * t_new, C_GROUP_DIM)

    outs = _pool_mix(ext_at, u, pos, pw_ref, ps_ref)
    for g, og in enumerate(outs):
        o_ref[:, g * C_GROUP_DIM:(g + 1) * C_GROUP_DIM] = og.astype(o_ref.dtype)


def _pool_sample(z_s, state_pool, pool_w, pool_scale, layer, *, n_seq, t_new, first_pos, nb):
    c0 = _COL_S["cu"] // C_WIDTH
    return pl.pallas_call(
        functools.partial(_pool_sample_kernel, nb=nb, t_new=t_new, first_pos=first_pos),
        grid=(n_seq // nb,),
        in_specs=[pl.BlockSpec((nb * t_new, C_WIDTH), lambda i: (i, c0)),
                  pl.BlockSpec((1, nb, C_HIST, C_WIDTH), lambda i, layer=layer: (layer, i, 0, 0)),
                  pl.BlockSpec((C_GROUPS, C_GROUP_DIM, C_GROUP_DIM), lambda i: (0, 0, 0)),
                  pl.BlockSpec((1, C_WIDTH), lambda i: (0, 0))],
        out_specs=pl.BlockSpec((nb * t_new, C_WIDTH), lambda i: (i, 0)),
        out_shape=jax.ShapeDtypeStruct((n_seq * t_new, C_WIDTH), BF16),
        scratch_shapes=[pltpu.VMEM((nb, _POOL_PAD + t_new, C_WIDTH), F32)],
        compiler_params=_cparams(("parallel",)),
        name="pool_sample",
    )(z_s, state_pool, pool_w, pool_scale.reshape(1, C_WIDTH))


def _token_order(ref, scr_ref, d, tm):
    n_chunk = ref.shape[-1] // LANE
    if d == 1:
        return [ref[0, 0, :, c * LANE:(c + 1) * LANE] for c in range(n_chunk)]
    for r in range(d):
        for c in range(n_chunk):
            scr_ref[c, pl.ds(r, tm // d, stride=d), :] = ref[0, r, :, c * LANE:(c + 1) * LANE]
    return [scr_ref[c] for c in range(n_chunk)]


def _out_proj_kernel(x_ref, ag_ref, bg_ref, cg_ref, m0_ref, m1_ref, m2_ref, o0_ref, o1_ref, o2_ref,
                     l0_ref, l1_ref, l2_ref, ob_ref, oc_ref, wa_ref, wb_ref, wc_ref, wo_ref,
                     gp_ref, y_ref, *scratch, dil, tm):
    scr = iter(scratch)
    o_g, l_g = [], []
    for o_ref, l_ref, d in zip((o0_ref, o1_ref, o2_ref), (l0_ref, l1_ref, l2_ref), dil):
        so = next(scr) if d > 1 else None
        sl = next(scr) if d > 1 else None
        o_g.append(_token_order(o_ref, so, d, tm))
        l_g.append(_token_order(l_ref, sl, d, tm)[0])
    heads = []
    for h in range(A_HEADS):
        l0, l1, l2 = (l[:, h:h + 1] for l in l_g)
        mx = jnp.maximum(jnp.maximum(l0, l1), l2)
        e0, e1, e2 = jnp.exp(l0 - mx), jnp.exp(l1 - mx), jnp.exp(l2 - mx)
        inv = 1.0 / (e0 + e1 + e2)
        heads.append((e0 * inv) * o_g[0][h] + (e1 * inv) * o_g[1][h] + (e2 * inv) * o_g[2][h])
    o_a = jnp.concatenate(heads, axis=1)

    def silu_gated(o, g_ref):
        h = g_ref[...].astype(F32) * 0.5
        return (o.astype(F32) * (h + h * jnp.tanh(h))).astype(BF16)

    def merge_gated(m_ref, y, w_ref):
        p = jnp.dot(y, w_ref[...], preferred_element_type=F32)
        return p + jnp.tanh(m_ref[...].astype(F32) * 0.5) * p

    merged = (merge_gated(m0_ref, silu_gated(o_a, ag_ref), wa_ref)
              + merge_gated(m1_ref, silu_gated(ob_ref[...], bg_ref), wb_ref)
              + merge_gated(m2_ref, silu_gated(oc_ref[...], cg_ref), wc_ref)) * 0.5
    out = jnp.dot(merged.astype(BF16), wo_ref[...], preferred_element_type=F32)
    y_ref[...] = x_ref[...] + _rmsnorm_rows(out, gp_ref[...])


def _out_proj(x, z, col, o_g, lse_g, o_b, o_c, wa, wb, wc, wo, gain_post, *, tm):
    n = x.shape[0]
    dil = tuple(int(a.shape[1]) for a in o_g)
    tpb = (n // o_g[0].shape[0]) // tm

    def zcol(name, width, k=0):
        assert col[name] % width == 0
        c0 = col[name] // width + k
        return pl.BlockSpec((tm, width), lambda i: (i, c0))

    def gspec(d, width):
        return pl.BlockSpec((1, d, tm // d, width), lambda i: (i // tpb, 0, i % tpb, 0))

    rows = lambda width: pl.BlockSpec((tm, width), lambda i: (i, 0))
    const = lambda a: pl.BlockSpec(a.shape, lambda i: (0,) * a.ndim)
    gp = gain_post.reshape(1, D_MODEL)
    scratch = []
    for d in dil:
        if d > 1:
            scratch += [pltpu.VMEM((A_WIDTH // LANE, tm, LANE), F32), pltpu.VMEM((1, tm, LANE), F32)]
    return pl.pallas_call(
        functools.partial(_out_proj_kernel, dil=dil, tm=tm),
        grid=(n // tm,),
        in_specs=[rows(D_MODEL), zcol("ag", A_WIDTH), zcol("bg", B_WIDTH), zcol("cg", C_WIDTH),
                  zcol("mg", D_MODEL, 0), zcol("mg", D_MODEL, 1), zcol("mg", D_MODEL, 2)]
                 + [gspec(d, A_WIDTH) for d in dil] + [gspec(d, LANE) for d in dil]
                 + [rows(B_WIDTH), rows(C_WIDTH)]
                 + [const(wa), const(wb), const(wc), const(wo), const(gp)],
        out_specs=rows(D_MODEL),
        out_shape=jax.ShapeDtypeStruct((n, D_MODEL), F32),
        scratch_shapes=scratch,
        compiler_params=_cparams(("parallel",)),
        name="out_proj",
    )(x, z, z, z, z, z, z, *o_g, *lse_g, o_b, o_c, wa, wb, wc, wo, gp)


def _ref_cols(w, name, g=None):
    off = _REF_OFF[name]
    if g is None:
        return w[:, off:off + _WIDTHS[name]]
    return w[:, off + g * A_WIDTH:off + (g + 1) * A_WIDTH]


def _prep_w_in(w):
    parts = []
    for g in range(1, A_GROUPS):
        parts += [_ref_cols(w, "ak", g), _ref_cols(w, "av", g), _ref_cols(w, "aq", g)]
    parts += [_ref_cols(w, "ak", 0), _ref_cols(w, "av", 0), _ref_cols(w, "cu"), _ref_cols(w, "cg"),
              _ref_cols(w, "aq", 0)] + [_ref_cols(w, k) for k in ("ag", "bq", "bf", "bi", "bg", "mg")]
    return jnp.concatenate(parts, axis=1).astype(BF16)


def _kv_split(kv):
    return kv.reshape(kv.shape[:-1] + (2, A_HEADS, A_HEAD_DIM))


def kernel(x_prompt, x_sample, cache_kv_w128, cache_kv_w512, cache_kv_w2048, state_hgrn, state_pool,
           norm_pre, norm_post, w_in, hgrn_lb_logits, hgrn_norm, pool_w, pool_scale,
           w_br_a, w_br_b, w_br_c, w_out):
    depth = w_in.shape[0]
    b_p, s_p, _ = x_prompt.shape
    n_seq, t_new, _ = x_sample.shape
    caches = (cache_kv_w128, cache_kv_w512, cache_kv_w2048)
    past_len = cache_kv_w2048.shape[2]
    assert HGRN_ROWS % t_new == 0 and n_seq % (HGRN_ROWS // t_new) == 0
    n_p, n_s = b_p * s_p, n_seq * t_new

    hp = x_prompt.reshape(n_p, D_MODEL)
    hs = x_sample.reshape(n_s, D_MODEL)
    kv_p = [[] for _ in range(A_GROUPS)]
    kv_s = [[] for _ in range(A_GROUPS)]
    hg_p, pl_p, pl_s = [], [], []
    hg_s = jnp.zeros(state_hgrn.shape, F32)
    tm_p = min(1024, s_p)
    for l in range(depth):
        w_l = _prep_w_in(w_in[l])
        wa, wb, wc, wo = (w[l].astype(BF16) for w in (w_br_a, w_br_b, w_br_c, w_out))

        (z_p,) = _in_proj(hp, norm_pre[l], w_l, col0=_P_COL0, tm=min(2048, s_p), tn=512,
                          z_dtype=BF16, f32_cols=0)
        z3 = z_p.reshape(b_p, s_p, _P_WIDTH)

        (u_tail,) = _in_proj(hp, norm_pre[l], w_l, col0=_COL_S["cu"], width=C_WIDTH,
                             n_rows=b_p * _POOL_PAD, row_block=_tail_row_block(s_p, _POOL_PAD, _POOL_PAD),
                             tm=_POOL_PAD, tn=C_WIDTH, z_dtype=F32, f32_cols=0)
        pl_p.append(u_tail.reshape(b_p, _POOL_PAD, C_WIDTH)[:, _POOL_PAD - C_HIST:])
        o_g, lse_g = [], []
        for g, (win, d) in enumerate(A_PATTERNS):
            rows = min(win, s_p)
            kv = _kv_rows(hp, norm_pre[l], w_l, _attn_cols(g)[0], n_seq=b_p, seq=s_p, rows=rows,
                          tm=min(1024, rows))
            kv_p[g].append(kv.reshape(b_p, rows, 2, A_HEADS, A_HEAD_DIM))
            if g == 0:
                arr = z_p.reshape(b_p, 1, s_p, _P_WIDTH)
                cols = (_COL_P["aq"] // A_WIDTH, _COL_P["ak"] // A_WIDTH, _COL_P["av"] // A_WIDTH)
            else:
                arr = _in_proj_perm(hp, norm_pre[l], w_l, g, b=b_p, s=s_p, d=d, tm=tm_p)
                cols = (2, 0, 1)
            tq = min(1024, s_p // d)
            o, lse = _attn_prompt(arr, cols, tq=tq, rpb=max(1, min(d, 1024 // tq)))
            o_g.append(o)
            lse_g.append(lse)
        o_b, s_new = _hgrn_prompt(z3, _COL_P, hgrn_lb_logits, hgrn_norm[l], l, tl=min(1024, s_p))
        hg_p.append(s_new)
        o_c = _pool_prompt(z3, _COL_P, pool_w[l], pool_scale[l], tl=min(1024, s_p))
        hp = _out_proj(hp, z_p, _COL_P, o_g, lse_g, o_b.reshape(n_p, B_WIDTH),
                       o_c.reshape(n_p, C_WIDTH), wa, wb, wc, wo, norm_post[l], tm=min(512, s_p))

        (z_s,) = _in_proj(hs, norm_pre[l], w_l, col0=0, tm=min(1024, n_s), tn=1024, z_dtype=F32,
                          f32_cols=0)
        o_g, lse_g = _attn_sample(z_s, caches, l, n_seq=n_seq, t_new=t_new)
        zs3 = z_s.reshape(n_seq, t_new, IN_WIDTH)
        for g in range(A_GROUPS):
            k_off, v_off, _ = _attn_cols(g)
            kv_s[g].append(_kv_split(zs3[:, :, k_off:v_off + A_WIDTH]))
        o_b, hg_s = _hgrn_sample(z_s, state_hgrn, hg_s, hgrn_lb_logits, hgrn_norm[l], l,
                                 n_seq=n_seq, t_new=t_new)
        o_c = _pool_sample(z_s, state_pool, pool_w[l], pool_scale[l], l, n_seq=n_seq,
                           t_new=t_new, first_pos=past_len, nb=HGRN_ROWS // t_new)
        u_new = zs3[:, :, _COL_S["cu"]:_COL_S["cu"] + C_WIDTH]
        pl_s.append(jnp.concatenate([state_pool[l], u_new], axis=1)[:, -C_HIST:])
        as4 = lambda a: a.reshape(1, 1, n_s, a.shape[-1])
        hs = _out_proj(hs, z_s, _COL_S, [as4(a) for a in o_g], [as4(a) for a in lse_g], o_b, o_c,
                       wa, wb, wc, wo, norm_post[l], tm=min(256, n_s))

    return (hp.reshape(b_p, s_p, D_MODEL), hs.reshape(n_seq, t_new, D_MODEL),
            jnp.stack(kv_p[0]), jnp.stack(kv_p[1]), jnp.stack(kv_p[2]),
            jnp.stack(kv_s[0]), jnp.stack(kv_s[1]), jnp.stack(kv_s[2]),
            jnp.stack(hg_p), hg_s, jnp.stack(pl_p), jnp.stack(pl_s))
```

```python
import functools

import numpy as np
import jax
import jax.numpy as jnp
from jax import lax
from jax.experimental import pallas as pl
from jax.experimental.pallas import tpu as pltpu

F32 = jnp.float32
BF16 = jnp.bfloat16

D_MODEL = 1024
A_PATTERNS = ((128, 1), (512, 4), (2048, 16))
A_GROUPS = 3
A_HEADS = 4
A_HEAD_DIM = 128
A_WIDTH = A_HEADS * A_HEAD_DIM
N_STEPS = 128
B_HEADS = 8
B_DIM = 128
B_WIDTH = B_HEADS * B_DIM
C_WINDOWS = (2, 4, 8, 16)
C_GROUPS = 4
C_GROUP_DIM = 128
C_WIDTH = C_GROUPS * C_GROUP_DIM
C_HIST = 15
N_BRANCH = 3
EPS = 1e-6
NEG = -1e30
IN_WIDTH = 13312
LANE = 128
HGRN_ROWS = 128
HGRN_HEADS_PER_STEP = 8
HGRN_SAMPLE_HEADS_PER_STEP = 4

_REF_OFF = dict(aq=0, ak=1536, av=3072, ag=4608, bq=5120, bf=6144, bi=7168, bg=8192,
                cu=9216, cg=9728, mg=10240)
_WIDTHS = dict(aq=1536, ak=1536, av=1536, ag=512, bq=1024, bf=1024, bi=1024, bg=1024,
               cu=512, cg=512, mg=3072)

_PERM_WIDTH = 3 * A_WIDTH
_P_COL0 = (A_GROUPS - 1) * _PERM_WIDTH
_COL_P = dict(ak=0, av=512, cu=1024, cg=1536, aq=2048, ag=2560, bq=3072, bf=4096,
              bi=5120, bg=6144, mg=7168)
_P_WIDTH = IN_WIDTH - _P_COL0
_COL_S = {name: _P_COL0 + off for name, off in _COL_P.items()}


def _attn_cols(g):
    if g == 0:
        return _COL_S["ak"], _COL_S["av"], _COL_S["aq"]
    base = (g - 1) * _PERM_WIDTH
    return base, base + A_WIDTH, base + 2 * A_WIDTH

_VMEM_LIMIT = 56 * 1024 * 1024


def _cparams(sem):
    return pltpu.CompilerParams(dimension_semantics=sem, vmem_limit_bytes=_VMEM_LIMIT)


def _nt(a, b):
    return lax.dot_general(a, b, (((1,), (1,)), ((), ())), preferred_element_type=F32)


def _tn(a, b):
    return lax.dot_general(a, b, (((0,), (0,)), ((), ())), preferred_element_type=F32)


def _sigmoid(x):
    return 1.0 / (1.0 + jnp.exp(-x))


def _rmsnorm_rows(x, gain):
    return x * lax.rsqrt(jnp.mean(x * x, axis=-1, keepdims=True) + EPS) * gain


def _in_proj_kernel(x_ref, g_ref, w_ref, *refs, n_f32_tiles, z_dtype):
    if n_f32_tiles:
        z_ref, zf_ref, xn_ref = refs
    else:
        z_ref, xn_ref = refs
    j = pl.program_id(1)

    @pl.when(j == 0)
    def _():
        xn_ref[...] = _rmsnorm_rows(x_ref[...], g_ref[...]).astype(BF16)

    acc = jnp.dot(xn_ref[...], w_ref[...], preferred_element_type=F32)
    z_ref[...] = acc.astype(z_dtype)
    if n_f32_tiles:
        @pl.when(j < n_f32_tiles)
        def _():
            zf_ref[...] = acc


def _tail_row_block(seq, rows, tm):
    per_seq, tiles, first = seq // tm, rows // tm, (seq - rows) // tm
    assert seq % tm == 0 and rows % tm == 0
    return lambda i: (i // tiles) * per_seq + first + i % tiles


def _in_proj(x, gain, w_bf16, *, col0, tm, tn, z_dtype, f32_cols, width=None, n_rows=None,
             row_block=lambda i: i):
    n = x.shape[0] if n_rows is None else n_rows
    width = w_bf16.shape[1] - col0 if width is None else width
    j0 = col0 // tn
    n_f32_tiles = f32_cols // tn
    out_shape = [jax.ShapeDtypeStruct((n, width), z_dtype)]
    out_specs = [pl.BlockSpec((tm, tn), lambda i, j: (i, j))]
    if n_f32_tiles:
        out_shape.append(jax.ShapeDtypeStruct((n, f32_cols), F32))
        out_specs.append(pl.BlockSpec((tm, tn), lambda i, j: (i, jnp.minimum(j, n_f32_tiles - 1))))
    return pl.pallas_call(
        functools.partial(_in_proj_kernel, n_f32_tiles=n_f32_tiles, z_dtype=z_dtype),
        grid=(n // tm, width // tn),
        in_specs=[pl.BlockSpec((tm, D_MODEL), lambda i, j: (row_block(i), 0)),
                  pl.BlockSpec((1, D_MODEL), lambda i, j: (0, 0)),
                  pl.BlockSpec((D_MODEL, tn), lambda i, j: (0, j + j0))],
        out_specs=out_specs,
        out_shape=out_shape,
        scratch_shapes=[pltpu.VMEM((tm, D_MODEL), BF16)],
        compiler_params=_cparams(("parallel", "arbitrary")),
        name="in_proj",
    )(x, gain.reshape(1, D_MODEL), w_bf16)


def _kv_rows_kernel(x_ref, g_ref, wk_ref, wv_ref, o_ref, *, tm):
    xn = _rmsnorm_rows(x_ref[...], g_ref[...]).astype(BF16)
    per_tok = 2 * A_HEADS
    for kv, w_ref in enumerate((wk_ref, wv_ref)):
        acc = jnp.dot(xn, w_ref[...], preferred_element_type=F32)
        for h in range(A_HEADS):
            o_ref[pl.ds(kv * A_HEADS + h, tm, stride=per_tok), :] = (
                acc[:, h * A_HEAD_DIM:(h + 1) * A_HEAD_DIM])


def _kv_rows(x, gain, w_bf16, k_col, *, n_seq, seq, rows, tm):
    n = n_seq * rows
    per_tok = 2 * A_HEADS
    kb = k_col // A_WIDTH
    row_block = _tail_row_block(seq, rows, tm)
    return pl.pallas_call(
        functools.partial(_kv_rows_kernel, tm=tm),
        grid=(n // tm,),
        in_specs=[pl.BlockSpec((tm, D_MODEL), lambda i: (row_block(i), 0)),
                  pl.BlockSpec((1, D_MODEL), lambda i: (0, 0)),
                  pl.BlockSpec((D_MODEL, A_WIDTH), lambda i: (0, kb)),
                  pl.BlockSpec((D_MODEL, A_WIDTH), lambda i: (0, kb + 1))],
        out_specs=pl.BlockSpec((tm * per_tok, A_HEAD_DIM), lambda i: (i, 0)),
        out_shape=jax.ShapeDtypeStruct((n * per_tok, A_HEAD_DIM), F32),
        compiler_params=_cparams(("parallel",)),
        name="kv_rows",
    )(x, gain.reshape(1, D_MODEL), w_bf16, w_bf16)


def _in_proj_perm_kernel(x_ref, g_ref, w_ref, r_ref, xs_ref, xn_ref, *, d, tm):
    xn = _rmsnorm_rows(x_ref[...], g_ref[...])
    n_chunk = D_MODEL // LANE
    rows = tm // d
    for c in range(n_chunk):
        xs_ref[c] = xn[:, c * LANE:(c + 1) * LANE]
    for r in range(d):
        for c in range(n_chunk):
            xn_ref[r * rows:(r + 1) * rows, c * LANE:(c + 1) * LANE] = (
                xs_ref[c, pl.ds(r, rows, stride=d), :].astype(BF16))
    acc = jnp.dot(xn_ref[...], w_ref[...], preferred_element_type=F32)
    r_ref[0] = acc.astype(BF16).reshape(d, rows, _PERM_WIDTH)


def _in_proj_perm(x, gain, w_bf16, g, *, b, s, d, tm):
    tpb = s // tm
    rows = tm // d
    idx = lambda i: (i // tpb, 0, i % tpb, 0)
    return pl.pallas_call(
        functools.partial(_in_proj_perm_kernel, d=d, tm=tm),
        grid=(b * tpb,),
        in_specs=[pl.BlockSpec((tm, D_MODEL), lambda i: (i, 0)),
                  pl.BlockSpec((1, D_MODEL), lambda i: (0, 0)),
                  pl.BlockSpec((D_MODEL, _PERM_WIDTH), lambda i: (0, g - 1))],
        out_specs=pl.BlockSpec((1, d, rows, _PERM_WIDTH), idx),
        out_shape=jax.ShapeDtypeStruct((b, d, s // d, _PERM_WIDTH), BF16),
        scratch_shapes=[pltpu.VMEM((D_MODEL // LANE, tm, LANE), F32),
                        pltpu.VMEM((tm, D_MODEL), BF16)],
        compiler_params=_cparams(("parallel",)),
        name="in_proj_d%d" % d,
    )(x, gain.reshape(1, D_MODEL), w_bf16)


def _attn_prompt_kernel(q_ref, kp_ref, kc_ref, vp_ref, vc_ref, o_ref, lse_ref, *, tq):
    i = pl.program_id(2)
    n = N_STEPS
    qi = lax.broadcasted_iota(jnp.int32, (n, 2 * n), 0)
    kj = lax.broadcasted_iota(jnp.int32, (n, 2 * n), 1)
    band = (kj >= qi) & (kj <= qi + n)
    band_first = band & ((i > 0) | (kj >= n))
    scale = A_HEAD_DIM ** -0.5
    lane = lax.broadcasted_iota(jnp.int32, (n, LANE), 1)
    for rr, j in [(rr, j) for rr in range(q_ref.shape[1]) for j in range(tq // n)]:
        rows = slice(j * n, (j + 1) * n)
        lse = jnp.zeros((n, LANE), F32)
        for h in range(A_HEADS):
            hs = slice(h * A_HEAD_DIM, (h + 1) * A_HEAD_DIM)
            q = q_ref[0, rr, rows, hs]
            if j == 0:
                k = jnp.concatenate([kp_ref[0, rr, :, hs], kc_ref[0, rr, rows, hs]], axis=0)
                v = jnp.concatenate([vp_ref[0, rr, :, hs], vc_ref[0, rr, rows, hs]], axis=0)
                valid = band_first
            else:
                k = kc_ref[0, rr, (j - 1) * n:(j + 1) * n, hs]
                v = vc_ref[0, rr, (j - 1) * n:(j + 1) * n, hs]
                valid = band
            s = jnp.where(valid, _nt(q, k) * scale, NEG)
            m = jnp.max(s, axis=-1, keepdims=True)
            p = jnp.exp(s - m)
            l = jnp.sum(p, axis=-1, keepdims=True)
            o = jnp.dot(p.astype(BF16), v, preferred_element_type=F32)
            o_ref[0, rr, rows, hs] = o / l
            lse = jnp.where(lane == h, m + jnp.log(l), lse)
        lse_ref[0, rr, rows, :] = lse


def _attn_prompt(arr, cols, *, tq, rpb):
    b, d, rows, _ = arr.shape
    qc, kc, vc = cols
    prev_per_blk = tq // N_STEPS

    def cur(c):
        return pl.BlockSpec((1, rpb, tq, A_WIDTH), lambda bb, r, i: (bb, r, i, c))

    def prev(c):
        return pl.BlockSpec((1, rpb, N_STEPS, A_WIDTH),
                            lambda bb, r, i: (bb, r, jnp.maximum(i * prev_per_blk - 1, 0), c))

    out_spec = pl.BlockSpec((1, rpb, tq, A_WIDTH), lambda bb, r, i: (bb, r, i, 0))
    return pl.pallas_call(
        functools.partial(_attn_prompt_kernel, tq=tq),
        grid=(b, d // rpb, rows // tq),
        in_specs=[cur(qc), prev(kc), cur(kc), prev(vc), cur(vc)],
        out_specs=[out_spec, pl.BlockSpec((1, rpb, tq, LANE), lambda bb, r, i: (bb, r, i, 0))],
        out_shape=[jax.ShapeDtypeStruct((b, d, rows, A_WIDTH), F32),
                   jax.ShapeDtypeStruct((b, d, rows, LANE), F32)],
        compiler_params=_cparams(("parallel", "parallel", "arbitrary")),
        name="attn_prompt_d%d" % d,
    )(arr, arr, arr, arr, arr)


def _gather_pitch(n_j):
    p = -(-n_j // 4)
    return 4 * (p if p % 2 else p + 1)


def _attn_sample_kernel(z_ref, c0_ref, c1_ref, c2_ref, *refs, t_new):
    out_refs, flat_refs = refs[:2 * A_GROUPS], refs[2 * A_GROUPS:]
    scale = A_HEAD_DIM ** -0.5
    nrow = A_HEADS * t_new
    per_res = 2 * A_HEADS
    for g, c_ref in enumerate((c0_ref, c1_ref, c2_ref)):
        d = A_PATTERNS[g][1]
        n_res = min(d, t_new)
        o_ref, lse_ref = out_refs[2 * g], out_refs[2 * g + 1]
        k_off, v_off, q_off = _attn_cols(g)
        q = z_ref[:, q_off:q_off + A_WIDTH]
        kn = z_ref[:, k_off:k_off + A_WIDTH]
        vn = z_ref[:, v_off:v_off + A_WIDTH]
        n_j = n_res * per_res
        pitch = _gather_pitch(n_j)
        flat_ref = flat_refs[g]
        for mm in range(N_STEPS):
            flat_ref[mm * pitch:mm * pitch + n_j, :] = c_ref[mm]
        gather = lambda j: flat_ref[pl.ds(j, N_STEPS, stride=pitch), :]
        kcat = jnp.concatenate([gather(r * per_res + h)
                                for r in range(n_res) for h in range(A_HEADS)], axis=1).astype(BF16)
        vcat = jnp.concatenate([gather(r * per_res + A_HEADS + h)
                                for r in range(n_res) for h in range(A_HEADS)], axis=1).astype(BF16)
        wide = n_res * A_WIDTH
        q4 = jnp.concatenate([q] * A_HEADS, axis=0)
        row = lax.broadcasted_iota(jnp.int32, (nrow, wide), 0)
        col = lax.broadcasted_iota(jnp.int32, (nrow, wide), 1)
        row_h, row_t = row // t_new, row % t_new
        sel = ((col // A_WIDTH) == (row_t % d)) & (((col % A_WIDTH) // A_HEAD_DIM) == row_h)
        qbd = jnp.where(sel, jnp.concatenate([q4] * n_res, axis=1), 0.0).astype(BF16)
        s_c = _nt(qbd, kcat) * scale
        r2 = lax.broadcasted_iota(jnp.int32, (nrow, N_STEPS), 0) % t_new
        c2 = lax.broadcasted_iota(jnp.int32, (nrow, N_STEPS), 1)
        s_c = jnp.where(c2 >= r2 // d, s_c, NEG)
        pad = jnp.zeros((N_STEPS - t_new, A_WIDTH), F32)
        knp = jnp.concatenate([kn, pad], axis=0).astype(BF16)
        vnp = jnp.concatenate([vn, pad], axis=0).astype(BF16)
        rowq = lax.broadcasted_iota(jnp.int32, (nrow, A_WIDTH), 0) // t_new
        colq = lax.broadcasted_iota(jnp.int32, (nrow, A_WIDTH), 1) // A_HEAD_DIM
        selq = rowq == colq
        qbd2 = jnp.where(selq, q4, 0.0).astype(BF16)
        s_n = _nt(qbd2, knp) * scale
        s_n = jnp.where((c2 <= r2) & ((r2 - c2) % d == 0), s_n, NEG)
        m = jnp.maximum(jnp.max(s_c, axis=-1, keepdims=True), jnp.max(s_n, axis=-1, keepdims=True))
        p_c = jnp.exp(s_c - m)
        p_n = jnp.exp(s_n - m)
        l = jnp.sum(p_c, axis=-1, keepdims=True) + jnp.sum(p_n, axis=-1, keepdims=True)
        o_wide = jnp.where(sel, jnp.dot(p_c.astype(BF16), vcat, preferred_element_type=F32), 0.0)
        o_new = jnp.where(selq, jnp.dot(p_n.astype(BF16), vnp, preferred_element_type=F32), 0.0)
        o = o_new[:, 0:A_HEAD_DIM]
        for c in range(1, A_HEADS):
            o = o + o_new[:, c * A_HEAD_DIM:(c + 1) * A_HEAD_DIM]
        for c in range(wide // A_HEAD_DIM):
            o = o + o_wide[:, c * A_HEAD_DIM:(c + 1) * A_HEAD_DIM]
        o = o / l
        lse = m + jnp.log(l)
        lane = lax.broadcasted_iota(jnp.int32, (t_new, LANE), 1)
        lse_out = jnp.zeros((t_new, LANE), F32)
        for h in range(A_HEADS):
            hs = slice(h * A_HEAD_DIM, (h + 1) * A_HEAD_DIM)
            o_ref[:, hs] = o[h * t_new:(h + 1) * t_new]
            lse_out = jnp.where(lane == h, lse[h * t_new:(h + 1) * t_new], lse_out)
        lse_ref[...] = lse_out


def _attn_sample(z_s, caches, layer, *, n_seq, t_new):
    in_specs = [pl.BlockSpec((t_new, IN_WIDTH), lambda b: (b, 0))]
    views, scratch = [], []
    per_res = 2 * A_HEADS
    for g, c in enumerate(caches):
        win, d = A_PATTERNS[g]
        assert c.shape[2] == win == N_STEPS * d, "cache must hold one full window"
        n_res = min(d, t_new)
        views.append(c.reshape(c.shape[0], n_seq, N_STEPS, d * per_res, A_HEAD_DIM))
        in_specs.append(pl.BlockSpec((None, None, N_STEPS, n_res * per_res, A_HEAD_DIM),
                                     lambda b, layer=layer: (layer, b, 0, 0, 0)))
        scratch.append(pltpu.VMEM((N_STEPS * _gather_pitch(n_res * per_res), A_HEAD_DIM), F32))
    out_specs = [pl.BlockSpec((t_new, A_WIDTH), lambda b: (b, 0)),
                 pl.BlockSpec((t_new, LANE), lambda b: (b, 0))] * A_GROUPS
    out_shape = [jax.ShapeDtypeStruct((n_seq * t_new, A_WIDTH), F32),
                 jax.ShapeDtypeStruct((n_seq * t_new, LANE), F32)] * A_GROUPS
    outs = pl.pallas_call(
        functools.partial(_attn_sample_kernel, t_new=t_new),
        grid=(n_seq,),
        in_specs=in_specs,
        out_specs=out_specs,
        out_shape=out_shape,
        scratch_shapes=scratch,
        compiler_params=_cparams(("parallel",)),
        name="attn_sample",
    )(z_s, *views)
    return outs[0::2], outs[1::2]


def _hgrn_consts(seg):
    n = HGRN_ROWS
    t = np.arange(n)[:, None]
    u = np.arange(n)[None, :]
    seg_lo = (t // seg) * seg
    seg_hi = seg_lo + seg - 1
    mats = [(u >= seg_lo) & (u <= t),
            (u > t) & (u <= seg_hi)]
    level = np.full((n, n), -1, np.int32)
    s = u
    li = 0
    m = seg
    while m >= 2:
        half = m // 2
        upper = (t % m) >= half
        ref = (t // m) * m + half - 1
        if m > 2:
            mats.append(np.where(upper, (u > ref) & (u <= t), (u > t) & (u <= ref)))
        pair = (t // m == s // m) & ((t % m) >= half) & ((s % m) < half)
        level[pair] = li
        li += 1
        m //= 2
    level[np.arange(n), np.arange(n)] = li
    w = np.concatenate(mats, axis=0).astype(np.float32)
    w2 = np.concatenate([w, w], axis=1)
    return jnp.asarray(w2, BF16), jnp.asarray(level), li


def _hgrn_blocks(items, w2, level, n_lev, seg):
    n = HGRN_ROWS
    n_seg = n // seg
    row = lax.broadcasted_iota(jnp.int32, (n, B_DIM), 0)

    gates = []
    for q, xf, v, lb, states in items:
        f = lb + (1.0 - lb) * _sigmoid(xf)
        g = jnp.log2(f)
        g_hi = g.astype(BF16)
        g_lo = (g - g_hi.astype(F32)).astype(BF16)
        gates.append((f, jnp.concatenate([g_hi, g_lo], axis=0)))
    stage1 = []
    for i in range(0, len(items), 2):
        pair = gates[i:i + 2]
        rhs = pair[0][1] if len(pair) == 1 else jnp.concatenate([pair[0][1], pair[1][1]], axis=1)
        dsum = jnp.dot(w2, rhs, preferred_element_type=F32)
        for k, (f, _) in enumerate(pair):
            stage1.append((f, 1.0 - f, dsum[:, k * B_DIM:(k + 1) * B_DIM]))

    stage2 = []
    for (q, xf, v, lb, states), (f, kin, dsum) in zip(items, stage1):
        fac = jnp.exp2(dsum)
        ea = fac[0:n]
        ops = []
        m = seg
        for li in range(n_lev):
            half = m // 2
            if half % 8 == 0:
                src = jnp.concatenate([(kin if (r0 // half) % 2 == 0 else q)[r0:r0 + half]
                                       for r0 in range(0, n, half)], axis=0)
                tm = src * fac[(2 + li) * n:(3 + li) * n]
            elif m > 2:
                tm = jnp.where((row & half) != 0, q, kin) * fac[(2 + li) * n:(3 + li) * n]
            else:
                tm = jnp.where((row & half) != 0, q * f, kin)
            ops.append(tm.astype(BF16))
            m //= 2
        stage2.append((ea, (q * ea).astype(BF16), kin * fac[n:2 * n], ops))

    stage3 = []
    in_level = [level == li for li in range(n_lev + 1)]
    for (q, xf, v, lb, states), (f, kin, dsum), (ea, qt, kh, ops) in zip(items, stage1, stage2):
        att = jnp.zeros((n, n), F32)
        for li, tm in enumerate(ops):
            att = jnp.where(in_level[li], _nt(tm, tm), att)
        att = jnp.where(in_level[n_lev], _nt(q.astype(BF16), kin.astype(BF16)), att)
        stage3.append(att.astype(BF16))

    results = []
    for (q, xf, v, lb, states), (ea, qt, kh, ops), att in zip(items, stage2, stage3):
        vb = v.astype(BF16)
        o = jnp.dot(att, vb, preferred_element_type=F32)
        new_states = []
        for si in range(n_seg):
            st = states[si]
            o_s = _nt(qt, st.astype(BF16))
            if n_seg == 1:
                o = o + o_s
                kh_s = kh
            else:
                in_seg = (row >= si * seg) & (row < (si + 1) * seg)
                o = o + jnp.where(in_seg, o_s, 0.0)
                kh_s = jnp.where(in_seg, kh, 0.0)
            decay = ea[(si + 1) * seg - 1:(si + 1) * seg, :]
            new_states.append(st * decay + _tn(vb, kh_s.astype(BF16)))
        results.append((o, new_states))
    return results


def _lower_bound(lbl_ref, layer):
    lg = lbl_ref[...]
    e = jnp.exp(lg - jnp.max(lg, axis=0, keepdims=True))
    p = e / jnp.sum(e, axis=0, keepdims=True)
    return jnp.sum(p[0:layer + 1], axis=0, keepdims=True) - p[0:1]


def _hgrn_prompt_kernel(q_ref, f_ref, v_ref, lbl_ref, gn_ref, w3_ref, lev_ref, o_ref, s_ref,
                        st_ref, *, layer, n_lev, tl, n_heads):
    li = pl.program_id(2)

    @pl.when(li == 0)
    def _():
        st_ref[...] = jnp.zeros_like(st_ref)

    lb = _lower_bound(lbl_ref, layer)
    w3 = w3_ref[...]
    level = lev_ref[...]
    gain = gn_ref[...]
    head = [slice(hh * B_DIM, (hh + 1) * B_DIM) for hh in range(n_heads)]

    def body(c, carry):
        r0 = pl.multiple_of(c * HGRN_ROWS, HGRN_ROWS)
        rows = pl.ds(r0, HGRN_ROWS)
        items = [(q_ref[0, rows, hs].astype(F32), f_ref[0, rows, hs].astype(F32),
                  v_ref[0, rows, hs].astype(F32), lb[:, hs], [st_ref[hh]])
                 for hh, hs in enumerate(head)]
        for hh, (o, (st,)) in enumerate(_hgrn_blocks(items, w3, level, n_lev, HGRN_ROWS)):
            st_ref[hh] = st
            o_ref[0, rows, head[hh]] = _rmsnorm_rows(o, gain).astype(o_ref.dtype)
        return carry

    lax.fori_loop(0, tl // HGRN_ROWS, body, 0, unroll=2)

    @pl.when(li == pl.num_programs(2) - 1)
    def _():
        for hh in range(n_heads):
            s_ref[0, hh] = st_ref[hh].T


def _hgrn_prompt(z3, col, lb_logits, hgrn_gain, layer, *, tl):
    b, s, _ = z3.shape
    w3, level, n_lev = _hgrn_consts(HGRN_ROWS)
    nh = HGRN_HEADS_PER_STEP
    wblk = nh * B_DIM

    def cspec(name):
        c0 = col[name] // wblk
        return pl.BlockSpec((1, tl, wblk), lambda bb, h, i: (bb, i, c0 + h))

    const = lambda shape: pl.BlockSpec(shape, lambda bb, h, i: (0,) * len(shape))
    return pl.pallas_call(
        functools.partial(_hgrn_prompt_kernel, layer=layer, n_lev=n_lev, tl=tl, n_heads=nh),
        grid=(b, B_HEADS // nh, s // tl),
        in_specs=[cspec("bq"), cspec("bf"), cspec("bi"),
                  pl.BlockSpec((lb_logits.shape[0], wblk), lambda bb, h, i: (0, h)),
                  const((1, B_DIM)), const(w3.shape), const(level.shape)],
        out_specs=[pl.BlockSpec((1, tl, wblk), lambda bb, h, i: (bb, i, h)),
                   pl.BlockSpec((1, nh, B_DIM, B_DIM), lambda bb, h, i: (bb, h, 0, 0))],
        out_shape=[jax.ShapeDtypeStruct((b, s, B_WIDTH), BF16),
                   jax.ShapeDtypeStruct((b, B_HEADS, B_DIM, B_DIM), F32)],
        scratch_shapes=[pltpu.VMEM((nh, B_DIM, B_DIM), F32)],
        compiler_params=_cparams(("parallel", "parallel", "arbitrary")),
        name="hgrn_prompt",
    )(z3, z3, z3, lb_logits, hgrn_gain.reshape(1, B_DIM), w3, level)


def _hgrn_sample_kernel(q_ref, f_ref, v_ref, s0_ref, lbl_ref, gn_ref, w3_ref, lev_ref, acc_ref,
                        o_ref, s_ref, *, layer, n_lev, t_new, n_heads):
    del acc_ref
    lb = _lower_bound(lbl_ref, layer)
    n_seg = HGRN_ROWS // t_new
    head = [slice(hh * B_DIM, (hh + 1) * B_DIM) for hh in range(n_heads)]
    items = [(q_ref[:, hs], f_ref[:, hs], v_ref[:, hs], lb[:, hs],
              [s0_ref[0, si, hh].T for si in range(n_seg)])
             for hh, hs in enumerate(head)]
    results = _hgrn_blocks(items, w3_ref[...], lev_ref[...], n_lev, t_new)
    for hh, (o, new_states) in enumerate(results):
        o_ref[:, head[hh]] = _rmsnorm_rows(o, gn_ref[...]).astype(o_ref.dtype)
        for si in range(n_seg):
            s_ref[0, si, hh] = new_states[si].T


def _hgrn_sample(z_s, state, new_state, lb_logits, hgrn_gain, layer, *, n_seq, t_new):
    w3, level, n_lev = _hgrn_consts(t_new)
    n_seg = HGRN_ROWS // t_new
    nh = HGRN_SAMPLE_HEADS_PER_STEP
    wblk = nh * B_DIM

    def cspec(name):
        c0 = _COL_S[name] // wblk
        return pl.BlockSpec((HGRN_ROWS, wblk), lambda i, h: (i, c0 + h))

    const = lambda shape: pl.BlockSpec(shape, lambda i, h: (0,) * len(shape))
    return pl.pallas_call(
        functools.partial(_hgrn_sample_kernel, layer=layer, n_lev=n_lev, t_new=t_new, n_heads=nh),
        grid=(n_seq // n_seg, B_HEADS // nh),
        in_specs=[cspec("bq"), cspec("bf"), cspec("bi"),
                  pl.BlockSpec((1, n_seg, nh, B_DIM, B_DIM), lambda i, h, layer=layer: (layer, i, h, 0, 0)),
                  pl.BlockSpec((lb_logits.shape[0], wblk), lambda i, h: (0, h)),
                  const((1, B_DIM)), const(w3.shape), const(level.shape),
                  pl.BlockSpec(memory_space=pl.ANY)],
        out_specs=[pl.BlockSpec((HGRN_ROWS, wblk), lambda i, h: (i, h)),
                   pl.BlockSpec((1, n_seg, nh, B_DIM, B_DIM), lambda i, h, layer=layer: (layer, i, h, 0, 0))],
        out_shape=[jax.ShapeDtypeStruct((n_seq * t_new, B_WIDTH), BF16),
                   jax.ShapeDtypeStruct(new_state.shape, F32)],
        input_output_aliases={8: 1},
        compiler_params=_cparams(("parallel", "parallel")),
        name="hgrn_sample",
    )(z_s, z_s, z_s, state, lb_logits, hgrn_gain.reshape(1, B_DIM), w3, level, new_state)


_POOL_PAD = 16


def _pool_mix(ext_at, u, pos, pw_ref, ps_ref):
    outs = []
    for g, win in enumerate(C_WINDOWS):
        gs = slice(g * C_GROUP_DIM, (g + 1) * C_GROUP_DIM)
        acc = u[:, gs]
        for k in range(1, win):
            acc = acc + ext_at(k, gs)
        cnt = jnp.minimum(pos + 1, win).astype(F32)
        pooled = acc * (1.0 / cnt) - u[:, gs]
        mixed = jnp.dot(pooled.astype(BF16), pw_ref[g].astype(BF16), preferred_element_type=F32)
        outs.append(mixed * ps_ref[:, gs])
    return outs


def _pool_prompt_kernel(u_ref, h_ref, pw_ref, ps_ref, o_ref, ext_ref, *, tl):
    i = pl.program_id(1)
    halo = h_ref[0].astype(F32)
    ext_ref[0:_POOL_PAD, :] = jnp.where(i > 0, halo, 0.0)
    u = u_ref[0].astype(F32)
    ext_ref[_POOL_PAD:, :] = u
    pos = i * tl + lax.broadcasted_iota(jnp.int32, (tl, 1), 0)
    outs = _pool_mix(lambda k, gs: ext_ref[_POOL_PAD - k:_POOL_PAD - k + tl, gs], u, pos, pw_ref, ps_ref)
    for g, og in enumerate(outs):
        o_ref[0, :, g * C_GROUP_DIM:(g + 1) * C_GROUP_DIM] = og.astype(o_ref.dtype)


def _pool_prompt(z3, col, pool_w, pool_scale, *, tl):
    b, s, _ = z3.shape
    c0 = col["cu"] // C_WIDTH
    per = tl // _POOL_PAD
    return pl.pallas_call(
        functools.partial(_pool_prompt_kernel, tl=tl),
        grid=(b, s // tl),
        in_specs=[pl.BlockSpec((1, tl, C_WIDTH), lambda bb, i: (bb, i, c0)),
                  pl.BlockSpec((1, _POOL_PAD, C_WIDTH), lambda bb, i: (bb, jnp.maximum(i * per - 1, 0), c0)),
                  pl.BlockSpec((C_GROUPS, C_GROUP_DIM, C_GROUP_DIM), lambda bb, i: (0, 0, 0)),
                  pl.BlockSpec((1, C_WIDTH), lambda bb, i: (0, 0))],
        out_specs=pl.BlockSpec((1, tl, C_WIDTH), lambda bb, i: (bb, i, 0)),
        out_shape=jax.ShapeDtypeStruct((b, s, C_WIDTH), BF16),
        scratch_shapes=[pltpu.VMEM((_POOL_PAD + tl, C_WIDTH), F32)],
        compiler_params=_cparams(("parallel", "arbitrary")),
        name="pool_prompt",
    )(z3, z3, pool_w, pool_scale.reshape(1, C_WIDTH))


def _pool_sample_kernel(u_ref, h_ref, pw_ref, ps_ref, o_ref, ext_ref, *, nb, t_new, first_pos):
    ext_ref[:, 0:1, :] = jnp.zeros((nb, 1, C_WIDTH), F32)
    ext_ref[:, 1:_POOL_PAD, :] = h_ref[0]
    u3 = u_ref[...].reshape(nb, t_new, C_WIDTH)
    ext_ref[:, _POOL_PAD:, :] = u3
    u = u_ref[...]
    pos = first_pos + lax.broadcasted_iota(jnp.int32, (nb * t_new, 1), 0) % t_new

    def ext_at(k, gs):
        return ext_ref[:, _POOL_PAD - k:_POOL_PAD - k + t_new, gs].reshape(nb * t_new, C_GROUP_DIM)

    outs = _pool_mix(ext_at, u, pos, pw_ref, ps_ref)
    for g, og in enumerate(outs):
        o_ref[:, g * C_GROUP_DIM:(g + 1) * C_GROUP_DIM] = og.astype(o_ref.dtype)


def _pool_sample(z_s, state_pool, pool_w, pool_scale, layer, *, n_seq, t_new, first_pos, nb):
    c0 = _COL_S["cu"] // C_WIDTH
    return pl.pallas_call(
        functools.partial(_pool_sample_kernel, nb=nb, t_new=t_new, first_pos=first_pos),
        grid=(n_seq // nb,),
        in_specs=[pl.BlockSpec((nb * t_new, C_WIDTH), lambda i: (i, c0)),
                  pl.BlockSpec((1, nb, C_HIST, C_WIDTH), lambda i, layer=layer: (layer, i, 0, 0)),
                  pl.BlockSpec((C_GROUPS, C_GROUP_DIM, C_GROUP_DIM), lambda i: (0, 0, 0)),
                  pl.BlockSpec((1, C_WIDTH), lambda i: (0, 0))],
        out_specs=pl.BlockSpec((nb * t_new, C_WIDTH), lambda i: (i, 0)),
        out_shape=jax.ShapeDtypeStruct((n_seq * t_new, C_WIDTH), BF16),
        scratch_shapes=[pltpu.VMEM((nb, _POOL_PAD + t_new, C_WIDTH), F32)],
        compiler_params=_cparams(("parallel",)),
        name="pool_sample",
    )(z_s, state_pool, pool_w, pool_scale.reshape(1, C_WIDTH))


def _token_order(ref, scr_ref, d, tm):
    n_chunk = ref.shape[-1] // LANE
    if d == 1:
        return [ref[0, 0, :, c * LANE:(c + 1) * LANE] for c in range(n_chunk)]
    for r in range(d):
        for c in range(n_chunk):
            scr_ref[c, pl.ds(r, tm // d, stride=d), :] = ref[0, r, :, c * LANE:(c + 1) * LANE]
    return [scr_ref[c] for c in range(n_chunk)]


def _out_proj_kernel(x_ref, ag_ref, bg_ref, cg_ref, m0_ref, m1_ref, m2_ref, o0_ref, o1_ref, o2_ref,
                     l0_ref, l1_ref, l2_ref, ob_ref, oc_ref, wa_ref, wb_ref, wc_ref, wo_ref,
                     gp_ref, y_ref, *scratch, dil, tm):
    scr = iter(scratch)
    o_g, l_g = [], []
    for o_ref, l_ref, d in zip((o0_ref, o1_ref, o2_ref), (l0_ref, l1_ref, l2_ref), dil):
        so = next(scr) if d > 1 else None
        sl = next(scr) if d > 1 else None
        o_g.append(_token_order(o_ref, so, d, tm))
        l_g.append(_token_order(l_ref, sl, d, tm)[0])
    heads = []
    for h in range(A_HEADS):
        l0, l1, l2 = (l[:, h:h + 1] for l in l_g)
        mx = jnp.maximum(jnp.maximum(l0, l1), l2)
        e0, e1, e2 = jnp.exp(l0 - mx), jnp.exp(l1 - mx), jnp.exp(l2 - mx)
        inv = 1.0 / (e0 + e1 + e2)
        heads.append((e0 * inv) * o_g[0][h] + (e1 * inv) * o_g[1][h] + (e2 * inv) * o_g[2][h])
    o_a = jnp.concatenate(heads, axis=1)

    def silu_gated(o, g_ref):
        h = g_ref[...].astype(F32) * 0.5
        return (o.astype(F32) * (h + h * jnp.tanh(h))).astype(BF16)

    def merge_gated(m_ref, y, w_ref):
        p = jnp.dot(y, w_ref[...], preferred_element_type=F32)
        return p + jnp.tanh(m_ref[...].astype(F32) * 0.5) * p

    merged = (merge_gated(m0_ref, silu_gated(o_a, ag_ref), wa_ref)
              + merge_gated(m1_ref, silu_gated(ob_ref[...], bg_ref), wb_ref)
              + merge_gated(m2_ref, silu_gated(oc_ref[...], cg_ref), wc_ref)) * 0.5
    out = jnp.dot(merged.astype(BF16), wo_ref[...], preferred_element_type=F32)
    y_ref[...] = x_ref[...] + _rmsnorm_rows(out, gp_ref[...])


def _out_proj(x, z, col, o_g, lse_g, o_b, o_c, wa, wb, wc, wo, gain_post, *, tm):
    n = x.shape[0]
    dil = tuple(int(a.shape[1]) for a in o_g)
    tpb = (n // o_g[0].shape[0]) // tm

    def zcol(name, width, k=0):
        assert col[name] % width == 0
        c0 = col[name] // width + k
        return pl.BlockSpec((tm, width), lambda i: (i, c0))

    def gspec(d, width):
        return pl.BlockSpec((1, d, tm // d, width), lambda i: (i // tpb, 0, i % tpb, 0))

    rows = lambda width: pl.BlockSpec((tm, width), lambda i: (i, 0))
    const = lambda a: pl.BlockSpec(a.shape, lambda i: (0,) * a.ndim)
    gp = gain_post.reshape(1, D_MODEL)
    scratch = []
    for d in dil:
        if d > 1:
            scratch += [pltpu.VMEM((A_WIDTH // LANE, tm, LANE), F32), pltpu.VMEM((1, tm, LANE), F32)]
    return pl.pallas_call(
        functools.partial(_out_proj_kernel, dil=dil, tm=tm),
        grid=(n // tm,),
        in_specs=[rows(D_MODEL), zcol("ag", A_WIDTH), zcol("bg", B_WIDTH), zcol("cg", C_WIDTH),
                  zcol("mg", D_MODEL, 0), zcol("mg", D_MODEL, 1), zcol("mg", D_MODEL, 2)]
                 + [gspec(d, A_WIDTH) for d in dil] + [gspec(d, LANE) for d in dil]
                 + [rows(B_WIDTH), rows(C_WIDTH)]
                 + [const(wa), const(wb), const(wc), const(wo), const(gp)],
        out_specs=rows(D_MODEL),
        out_shape=jax.ShapeDtypeStruct((n, D_MODEL), F32),
        scratch_shapes=scratch,
        compiler_params=_cparams(("parallel",)),
        name="out_proj",
    )(x, z, z, z, z, z, z, *o_g, *lse_g, o_b, o_c, wa, wb, wc, wo, gp)


def _ref_cols(w, name, g=None):
    off = _REF_OFF[name]
    if g is None:
        return w[:, off:off + _WIDTHS[name]]
    return w[:, off + g * A_WIDTH:off + (g + 1) * A_WIDTH]


def _prep_w_in(w):
    parts = []
    for g in range(1, A_GROUPS):
        parts += [_ref_cols(w, "ak", g), _ref_cols(w, "av", g), _ref_cols(w, "aq", g)]
    parts += [_ref_cols(w, "ak", 0), _ref_cols(w, "av", 0), _ref_cols(w, "cu"), _ref_cols(w, "cg"),
              _ref_cols(w, "aq", 0)] + [_ref_cols(w, k) for k in ("ag", "bq", "bf", "bi", "bg", "mg")]
    return jnp.concatenate(parts, axis=1).astype(BF16)


def _tile_plan(s_p, n_s):
    attn = []
    for _, d in A_PATTERNS:
        tq = min(1024, s_p // d)
        attn.append((tq, max(1, min(d, 1024 // tq))))
    return dict(
        in_proj_p=(min(2048, s_p), A_WIDTH),
        in_proj_s=(min(1024, n_s), 1024),
        perm=min(1024, s_p),
        kv_rows=1024,
        attn=attn,
        hgrn=min(2048, s_p),
        pool=min(2048, s_p),
        out_p=min(512, s_p),
        out_s=min(256, n_s),
    )


def _kv_split(kv):
    return kv.reshape(kv.shape[:-1] + (2, A_HEADS, A_HEAD_DIM))


def kernel(x_prompt, x_sample, cache_kv_w128, cache_kv_w512, cache_kv_w2048, state_hgrn, state_pool,
           norm_pre, norm_post, w_in, hgrn_lb_logits, hgrn_norm, pool_w, pool_scale,
           w_br_a, w_br_b, w_br_c, w_out):
    depth = w_in.shape[0]
    b_p, s_p, _ = x_prompt.shape
    n_seq, t_new, _ = x_sample.shape
    caches = (cache_kv_w128, cache_kv_w512, cache_kv_w2048)
    past_len = cache_kv_w2048.shape[2]
    assert HGRN_ROWS % t_new == 0 and n_seq % (HGRN_ROWS // t_new) == 0
    n_p, n_s = b_p * s_p, n_seq * t_new

    hp = x_prompt.reshape(n_p, D_MODEL)
    hs = x_sample.reshape(n_s, D_MODEL)
    kv_p = [[] for _ in range(A_GROUPS)]
    kv_s = [[] for _ in range(A_GROUPS)]
    hg_p, pl_p, pl_s = [], [], []
    hg_s = jnp.zeros(state_hgrn.shape, F32)
    tiles = _tile_plan(s_p, n_s)
    for l in range(depth):
        w_l = _prep_w_in(w_in[l])
        wa, wb, wc, wo = (w[l].astype(BF16) for w in (w_br_a, w_br_b, w_br_c, w_out))

        (z_p,) = _in_proj(hp, norm_pre[l], w_l, col0=_P_COL0, tm=tiles["in_proj_p"][0],
                          tn=tiles["in_proj_p"][1], z_dtype=BF16, f32_cols=0)
        z3 = z_p.reshape(b_p, s_p, _P_WIDTH)

        (u_tail,) = _in_proj(hp, norm_pre[l], w_l, col0=_COL_S["cu"], width=C_WIDTH,
                             n_rows=b_p * _POOL_PAD, row_block=_tail_row_block(s_p, _POOL_PAD, _POOL_PAD),
                             tm=_POOL_PAD, tn=C_WIDTH, z_dtype=F32, f32_cols=0)
        pl_p.append(u_tail.reshape(b_p, _POOL_PAD, C_WIDTH)[:, _POOL_PAD - C_HIST:])
        o_g, lse_g = [], []
        for g, (win, d) in enumerate(A_PATTERNS):
            rows = min(win, s_p)
            kv = _kv_rows(hp, norm_pre[l], w_l, _attn_cols(g)[0], n_seq=b_p, seq=s_p, rows=rows,
                          tm=min(tiles["kv_rows"], rows))
            kv_p[g].append(kv.reshape(b_p, rows, 2, A_HEADS, A_HEAD_DIM))
            if g == 0:
                arr = z_p.reshape(b_p, 1, s_p, _P_WIDTH)
                cols = (_COL_P["aq"] // A_WIDTH, _COL_P["ak"] // A_WIDTH, _COL_P["av"] // A_WIDTH)
            else:
                arr = _in_proj_perm(hp, norm_pre[l], w_l, g, b=b_p, s=s_p, d=d, tm=tiles["perm"])
                cols = (2, 0, 1)
            tq, rpb = tiles["attn"][g]
            o, lse = _attn_prompt(arr, cols, tq=tq, rpb=rpb)
            o_g.append(o)
            lse_g.append(lse)
        o_b, s_new = _hgrn_prompt(z3, _COL_P, hgrn_lb_logits, hgrn_norm[l], l, tl=tiles["hgrn"])
        hg_p.append(s_new)
        o_c = _pool_prompt(z3, _COL_P, pool_w[l], pool_scale[l], tl=tiles["pool"])
        hp = _out_proj(hp, z_p, _COL_P, o_g, lse_g, o_b.reshape(n_p, B_WIDTH),
                       o_c.reshape(n_p, C_WIDTH), wa, wb, wc, wo, norm_post[l], tm=tiles["out_p"])

        (z_s,) = _in_proj(hs, norm_pre[l], w_l, col0=0, tm=tiles["in_proj_s"][0],
                          tn=tiles["in_proj_s"][1], z_dtype=F32, f32_cols=0)
        o_g, lse_g = _attn_sample(z_s, caches, l, n_seq=n_seq, t_new=t_new)
        zs3 = z_s.reshape(n_seq, t_new, IN_WIDTH)
        for g in range(A_GROUPS):
            k_off, v_off, _ = _attn_cols(g)
            kv_s[g].append(_kv_split(zs3[:, :, k_off:v_off + A_WIDTH]))
        o_b, hg_s = _hgrn_sample(z_s, state_hgrn, hg_s, hgrn_lb_logits, hgrn_norm[l], l,
                                 n_seq=n_seq, t_new=t_new)
        o_c = _pool_sample(z_s, state_pool, pool_w[l], pool_scale[l], l, n_seq=n_seq,
                           t_new=t_new, first_pos=past_len, nb=HGRN_ROWS // t_new)
        u_new = zs3[:, :, _COL_S["cu"]:_COL_S["cu"] + C_WIDTH]
        pl_s.append(jnp.concatenate([state_pool[l], u_new], axis=1)[:, -C_HIST:])
        as4 = lambda a: a.reshape(1, 1, n_s, a.shape[-1])
        hs = _out_proj(hs, z_s, _COL_S, [as4(a) for a in o_g], [as4(a) for a in lse_g], o_b, o_c,
                       wa, wb, wc, wo, norm_post[l], tm=tiles["out_s"])

    return (hp.reshape(b_p, s_p, D_MODEL), hs.reshape(n_seq, t_new, D_MODEL),
            jnp.stack(kv_p[0]), jnp.stack(kv_p[1]), jnp.stack(kv_p[2]),
            jnp.stack(kv_s[0]), jnp.stack(kv_s[1]), jnp.stack(kv_s[2]),
            jnp.stack(hg_p), hg_s, jnp.stack(pl_p), jnp.stack(pl_s))
```

```python
import functools

import numpy as np
import jax
import jax.numpy as jnp
from jax import lax
from jax.experimental import pallas as pl
from jax.experimental.pallas import tpu as pltpu

F32 = jnp.float32
BF16 = jnp.bfloat16

D_MODEL = 1024
A_PATTERNS = ((128, 1), (512, 4), (2048, 16))
A_GROUPS = 3
A_HEADS = 4
A_HEAD_DIM = 128
A_WIDTH = A_HEADS * A_HEAD_DIM
N_STEPS = 128
B_HEADS = 8
B_DIM = 128
B_WIDTH = B_HEADS * B_DIM
C_WINDOWS = (2, 4, 8, 16)
C_GROUPS = 4
C_GROUP_DIM = 128
C_WIDTH = C_GROUPS * C_GROUP_DIM
C_HIST = 15
N_BRANCH = 3
EPS = 1e-6
NEG = -1e30
IN_WIDTH = 13312
LANE = 128
HGRN_ROWS = 128
HGRN_HEADS_PER_STEP = 8
HGRN_SAMPLE_HEADS_PER_STEP = 4

_REF_OFF = dict(aq=0, ak=1536, av=3072, ag=4608, bq=5120, bf=6144, bi=7168, bg=8192,
                cu=9216, cg=9728, mg=10240)
_WIDTHS = dict(aq=1536, ak=1536, av=1536, ag=512, bq=1024, bf=1024, bi=1024, bg=1024,
               cu=512, cg=512, mg=3072)

_PERM_WIDTH = 3 * A_WIDTH
_P_COL0 = (A_GROUPS - 1) * _PERM_WIDTH
_COL_P = dict(ak=0, av=512, cu=1024, cg=1536, aq=2048, ag=2560, bq=3072, bf=4096,
              bi=5120, bg=6144, mg=7168)
_P_WIDTH = IN_WIDTH - _P_COL0
_COL_S = {name: _P_COL0 + off for name, off in _COL_P.items()}


def _attn_cols(g):
    if g == 0:
        return _COL_S["ak"], _COL_S["av"], _COL_S["aq"]
    base = (g - 1) * _PERM_WIDTH
    return base, base + A_WIDTH, base + 2 * A_WIDTH

_VMEM_LIMIT = 56 * 1024 * 1024


def _cparams(sem):
    return pltpu.CompilerParams(dimension_semantics=sem, vmem_limit_bytes=_VMEM_LIMIT)


def _nt(a, b):
    return lax.dot_general(a, b, (((1,), (1,)), ((), ())), preferred_element_type=F32)


def _tn(a, b):
    return lax.dot_general(a, b, (((0,), (0,)), ((), ())), preferred_element_type=F32)


def _sigmoid(x):
    return 1.0 / (1.0 + jnp.exp(-x))


def _rmsnorm_rows(x, gain):
    return x * lax.rsqrt(jnp.mean(x * x, axis=-1, keepdims=True) + EPS) * gain


def _in_proj_kernel(x_ref, g_ref, w_ref, *refs, n_f32_tiles, z_dtype):
    if n_f32_tiles:
        z_ref, zf_ref, xn_ref = refs
    else:
        z_ref, xn_ref = refs
    j = pl.program_id(1)

    @pl.when(j == 0)
    def _():
        xn_ref[...] = _rmsnorm_rows(x_ref[...], g_ref[...]).astype(BF16)

    acc = jnp.dot(xn_ref[...], w_ref[...], preferred_element_type=F32)
    z_ref[...] = acc.astype(z_dtype)
    if n_f32_tiles:
        @pl.when(j < n_f32_tiles)
        def _():
            zf_ref[...] = acc


def _tail_row_block(seq, rows, tm):
    per_seq, tiles, first = seq // tm, rows // tm, (seq - rows) // tm
    assert seq % tm == 0 and rows % tm == 0
    return lambda i: (i // tiles) * per_seq + first + i % tiles


def _in_proj(x, gain, w_bf16, *, col0, tm, tn, z_dtype, f32_cols, width=None, n_rows=None,
             row_block=lambda i: i):
    n = x.shape[0] if n_rows is None else n_rows
    width = w_bf16.shape[1] - col0 if width is None else width
    j0 = col0 // tn
    n_f32_tiles = f32_cols // tn
    out_shape = [jax.ShapeDtypeStruct((n, width), z_dtype)]
    out_specs = [pl.BlockSpec((tm, tn), lambda i, j: (i, j))]
    if n_f32_tiles:
        out_shape.append(jax.ShapeDtypeStruct((n, f32_cols), F32))
        out_specs.append(pl.BlockSpec((tm, tn), lambda i, j: (i, jnp.minimum(j, n_f32_tiles - 1))))
    return pl.pallas_call(
        functools.partial(_in_proj_kernel, n_f32_tiles=n_f32_tiles, z_dtype=z_dtype),
        grid=(n // tm, width // tn),
        in_specs=[pl.BlockSpec((tm, D_MODEL), lambda i, j: (row_block(i), 0)),
                  pl.BlockSpec((1, D_MODEL), lambda i, j: (0, 0)),
                  pl.BlockSpec((D_MODEL, tn), lambda i, j: (0, j + j0))],
        out_specs=out_specs,
        out_shape=out_shape,
        scratch_shapes=[pltpu.VMEM((tm, D_MODEL), BF16)],
        compiler_params=_cparams(("parallel", "arbitrary")),
        name="in_proj",
    )(x, gain.reshape(1, D_MODEL), w_bf16)


def _kv_rows_kernel(x_ref, g_ref, wk_ref, wv_ref, o_ref, *, tm):
    xn = _rmsnorm_rows(x_ref[...], g_ref[...]).astype(BF16)
    per_tok = 2 * A_HEADS
    for kv, w_ref in enumerate((wk_ref, wv_ref)):
        acc = jnp.dot(xn, w_ref[...], preferred_element_type=F32)
        for h in range(A_HEADS):
            o_ref[pl.ds(kv * A_HEADS + h, tm, stride=per_tok), :] = (
                acc[:, h * A_HEAD_DIM:(h + 1) * A_HEAD_DIM])


def _kv_rows(x, gain, w_bf16, k_col, *, n_seq, seq, rows, tm):
    n = n_seq * rows
    per_tok = 2 * A_HEADS
    kb = k_col // A_WIDTH
    row_block = _tail_row_block(seq, rows, tm)
    return pl.pallas_call(
        functools.partial(_kv_rows_kernel, tm=tm),
        grid=(n // tm,),
        in_specs=[pl.BlockSpec((tm, D_MODEL), lambda i: (row_block(i), 0)),
                  pl.BlockSpec((1, D_MODEL), lambda i: (0, 0)),
                  pl.BlockSpec((D_MODEL, A_WIDTH), lambda i: (0, kb)),
                  pl.BlockSpec((D_MODEL, A_WIDTH), lambda i: (0, kb + 1))],
        out_specs=pl.BlockSpec((tm * per_tok, A_HEAD_DIM), lambda i: (i, 0)),
        out_shape=jax.ShapeDtypeStruct((n * per_tok, A_HEAD_DIM), F32),
        compiler_params=_cparams(("parallel",)),
        name="kv_rows",
    )(x, gain.reshape(1, D_MODEL), w_bf16, w_bf16)


def _in_proj_perm_kernel(x_ref, g_ref, w_ref, r_ref, xs_ref, xn_ref, *, d, tm):
    xn = _rmsnorm_rows(x_ref[...], g_ref[...])
    n_chunk = D_MODEL // LANE
    rows = tm // d
    for c in range(n_chunk):
        xs_ref[c] = xn[:, c * LANE:(c + 1) * LANE]
    for r in range(d):
        for c in range(n_chunk):
            xn_ref[r * rows:(r + 1) * rows, c * LANE:(c + 1) * LANE] = (
                xs_ref[c, pl.ds(r, rows, stride=d), :].astype(BF16))
    acc = jnp.dot(xn_ref[...], w_ref[...], preferred_element_type=F32)
    r_ref[0] = acc.astype(BF16).reshape(d, rows, _PERM_WIDTH)


def _in_proj_perm(x, gain, w_bf16, g, *, b, s, d, tm):
    tpb = s // tm
    rows = tm // d
    idx = lambda i: (i // tpb, 0, i % tpb, 0)
    return pl.pallas_call(
        functools.partial(_in_proj_perm_kernel, d=d, tm=tm),
        grid=(b * tpb,),
        in_specs=[pl.BlockSpec((tm, D_MODEL), lambda i: (i, 0)),
                  pl.BlockSpec((1, D_MODEL), lambda i: (0, 0)),
                  pl.BlockSpec((D_MODEL, _PERM_WIDTH), lambda i: (0, g - 1))],
        out_specs=pl.BlockSpec((1, d, rows, _PERM_WIDTH), idx),
        out_shape=jax.ShapeDtypeStruct((b, d, s // d, _PERM_WIDTH), BF16),
        scratch_shapes=[pltpu.VMEM((D_MODEL // LANE, tm, LANE), F32),
                        pltpu.VMEM((tm, D_MODEL), BF16)],
        compiler_params=_cparams(("parallel",)),
        name="in_proj_d%d" % d,
    )(x, gain.reshape(1, D_MODEL), w_bf16)


def _attn_prompt_kernel(q_ref, kp_ref, kc_ref, vp_ref, vc_ref, o_ref, lse_ref, *, tq):
    i = pl.program_id(2)
    n = N_STEPS
    qi = lax.broadcasted_iota(jnp.int32, (n, 2 * n), 0)
    kj = lax.broadcasted_iota(jnp.int32, (n, 2 * n), 1)
    band = (kj >= qi) & (kj <= qi + n)
    band_first = band & ((i > 0) | (kj >= n))
    scale = A_HEAD_DIM ** -0.5
    lane = lax.broadcasted_iota(jnp.int32, (n, LANE), 1)
    for rr, j in [(rr, j) for rr in range(q_ref.shape[1]) for j in range(tq // n)]:
        rows = slice(j * n, (j + 1) * n)
        lse = jnp.zeros((n, LANE), F32)
        for h in range(A_HEADS):
            hs = slice(h * A_HEAD_DIM, (h + 1) * A_HEAD_DIM)
            q = q_ref[0, rr, rows, hs]
            if j == 0:
                k = jnp.concatenate([kp_ref[0, rr, :, hs], kc_ref[0, rr, rows, hs]], axis=0)
                v = jnp.concatenate([vp_ref[0, rr, :, hs], vc_ref[0, rr, rows, hs]], axis=0)
                valid = band_first
            else:
                k = kc_ref[0, rr, (j - 1) * n:(j + 1) * n, hs]
                v = vc_ref[0, rr, (j - 1) * n:(j + 1) * n, hs]
                valid = band
            s = jnp.where(valid, _nt(q, k) * scale, NEG)
            m = jnp.max(s, axis=-1, keepdims=True)
            p = jnp.exp(s - m)
            l = jnp.sum(p, axis=-1, keepdims=True)
            o = jnp.dot(p.astype(BF16), v, preferred_element_type=F32)
            o_ref[0, rr, rows, hs] = o / l
            lse = jnp.where(lane == h, m + jnp.log(l), lse)
        lse_ref[0, rr, rows, :] = lse


def _attn_prompt(arr, cols, *, tq, rpb):
    b, d, rows, _ = arr.shape
    qc, kc, vc = cols
    prev_per_blk = tq // N_STEPS

    def cur(c):
        return pl.BlockSpec((1, rpb, tq, A_WIDTH), lambda bb, r, i: (bb, r, i, c))

    def prev(c):
        return pl.BlockSpec((1, rpb, N_STEPS, A_WIDTH),
                            lambda bb, r, i: (bb, r, jnp.maximum(i * prev_per_blk - 1, 0), c))

    out_spec = pl.BlockSpec((1, rpb, tq, A_WIDTH), lambda bb, r, i: (bb, r, i, 0))
    return pl.pallas_call(
        functools.partial(_attn_prompt_kernel, tq=tq),
        grid=(b, d // rpb, rows // tq),
        in_specs=[cur(qc), prev(kc), cur(kc), prev(vc), cur(vc)],
        out_specs=[out_spec, pl.BlockSpec((1, rpb, tq, LANE), lambda bb, r, i: (bb, r, i, 0))],
        out_shape=[jax.ShapeDtypeStruct((b, d, rows, A_WIDTH), F32),
                   jax.ShapeDtypeStruct((b, d, rows, LANE), F32)],
        compiler_params=_cparams(("parallel", "parallel", "arbitrary")),
        name="attn_prompt_d%d" % d,
    )(arr, arr, arr, arr, arr)


def _gather_pitch(n_j):
    p = -(-n_j // 4)
    return 4 * (p if p % 2 else p + 1)


def _attn_sample_kernel(z_ref, c0_ref, c1_ref, c2_ref, *refs, t_new):
    out_refs, flat_refs = refs[:2 * A_GROUPS], refs[2 * A_GROUPS:]
    scale = A_HEAD_DIM ** -0.5
    nrow = A_HEADS * t_new
    per_res = 2 * A_HEADS
    for g, c_ref in enumerate((c0_ref, c1_ref, c2_ref)):
        d = A_PATTERNS[g][1]
        n_res = min(d, t_new)
        o_ref, lse_ref = out_refs[2 * g], out_refs[2 * g + 1]
        k_off, v_off, q_off = _attn_cols(g)
        q = z_ref[:, q_off:q_off + A_WIDTH]
        kn = z_ref[:, k_off:k_off + A_WIDTH]
        vn = z_ref[:, v_off:v_off + A_WIDTH]
        n_j = n_res * per_res
        pitch = _gather_pitch(n_j)
        flat_ref = flat_refs[g]
        for mm in range(N_STEPS):
            flat_ref[mm * pitch:mm * pitch + n_j, :] = c_ref[mm]
        gather = lambda j: flat_ref[pl.ds(j, N_STEPS, stride=pitch), :]
        kcat = jnp.concatenate([gather(r * per_res + h)
                                for r in range(n_res) for h in range(A_HEADS)], axis=1).astype(BF16)
        vcat = jnp.concatenate([gather(r * per_res + A_HEADS + h)
                                for r in range(n_res) for h in range(A_HEADS)], axis=1).astype(BF16)
        wide = n_res * A_WIDTH
        q4 = jnp.concatenate([q] * A_HEADS, axis=0)
        row = lax.broadcasted_iota(jnp.int32, (nrow, wide), 0)
        col = lax.broadcasted_iota(jnp.int32, (nrow, wide), 1)
        row_h, row_t = row // t_new, row % t_new
        sel = ((col // A_WIDTH) == (row_t % d)) & (((col % A_WIDTH) // A_HEAD_DIM) == row_h)
        qbd = jnp.where(sel, jnp.concatenate([q4] * n_res, axis=1), 0.0).astype(BF16)
        s_c = _nt(qbd, kcat) * scale
        r2 = lax.broadcasted_iota(jnp.int32, (nrow, N_STEPS), 0) % t_new
        c2 = lax.broadcasted_iota(jnp.int32, (nrow, N_STEPS), 1)
        s_c = jnp.where(c2 >= r2 // d, s_c, NEG)
        pad = jnp.zeros((N_STEPS - t_new, A_WIDTH), F32)
        knp = jnp.concatenate([kn, pad], axis=0).astype(BF16)
        vnp = jnp.concatenate([vn, pad], axis=0).astype(BF16)
        rowq = lax.broadcasted_iota(jnp.int32, (nrow, A_WIDTH), 0) // t_new
        colq = lax.broadcasted_iota(jnp.int32, (nrow, A_WIDTH), 1) // A_HEAD_DIM
        selq = rowq == colq
        qbd2 = jnp.where(selq, q4, 0.0).astype(BF16)
        s_n = _nt(qbd2, knp) * scale
        s_n = jnp.where((c2 <= r2) & ((r2 - c2) % d == 0), s_n, NEG)
        m = jnp.maximum(jnp.max(s_c, axis=-1, keepdims=True), jnp.max(s_n, axis=-1, keepdims=True))
        p_c = jnp.exp(s_c - m)
        p_n = jnp.exp(s_n - m)
        l = jnp.sum(p_c, axis=-1, keepdims=True) + jnp.sum(p_n, axis=-1, keepdims=True)
        o_wide = jnp.where(sel, jnp.dot(p_c.astype(BF16), vcat, preferred_element_type=F32), 0.0)
        o_new = jnp.where(selq, jnp.dot(p_n.astype(BF16), vnp, preferred_element_type=F32), 0.0)
        o = o_new[:, 0:A_HEAD_DIM]
        for c in range(1, A_HEADS):
            o = o + o_new[:, c * A_HEAD_DIM:(c + 1) * A_HEAD_DIM]
        for c in range(wide // A_HEAD_DIM):
            o = o + o_wide[:, c * A_HEAD_DIM:(c + 1) * A_HEAD_DIM]
        o = o / l
        lse = m + jnp.log(l)
        lane = lax.broadcasted_iota(jnp.int32, (t_new, LANE), 1)
        lse_out = jnp.zeros((t_new, LANE), F32)
        for h in range(A_HEADS):
            hs = slice(h * A_HEAD_DIM, (h + 1) * A_HEAD_DIM)
            o_ref[:, hs] = o[h * t_new:(h + 1) * t_new]
            lse_out = jnp.where(lane == h, lse[h * t_new:(h + 1) * t_new], lse_out)
        lse_ref[...] = lse_out


def _attn_sample(z_s, caches, layer, *, n_seq, t_new):
    in_specs = [pl.BlockSpec((t_new, IN_WIDTH), lambda b: (b, 0))]
    views, scratch = [], []
    per_res = 2 * A_HEADS
    for g, c in enumerate(caches):
        win, d = A_PATTERNS[g]
        assert c.shape[2] == win == N_STEPS * d, "cache must hold one full window"
        n_res = min(d, t_new)
        views.append(c.reshape(c.shape[0], n_seq, N_STEPS, d * per_res, A_HEAD_DIM))
        in_specs.append(pl.BlockSpec((None, None, N_STEPS, n_res * per_res, A_HEAD_DIM),
                                     lambda b, layer=layer: (layer, b, 0, 0, 0)))
        scratch.append(pltpu.VMEM((N_STEPS * _gather_pitch(n_res * per_res), A_HEAD_DIM), F32))
    out_specs = [pl.BlockSpec((t_new, A_WIDTH), lambda b: (b, 0)),
                 pl.BlockSpec((t_new, LANE), lambda b: (b, 0))] * A_GROUPS
    out_shape = [jax.ShapeDtypeStruct((n_seq * t_new, A_WIDTH), F32),
                 jax.ShapeDtypeStruct((n_seq * t_new, LANE), F32)] * A_GROUPS
    outs = pl.pallas_call(
        functools.partial(_attn_sample_kernel, t_new=t_new),
        grid=(n_seq,),
        in_specs=in_specs,
        out_specs=out_specs,
        out_shape=out_shape,
        scratch_shapes=scratch,
        compiler_params=_cparams(("parallel",)),
        name="attn_sample",
    )(z_s, *views)
    return outs[0::2], outs[1::2]


def _hgrn_consts(seg):
    n = HGRN_ROWS
    t = np.arange(n)[:, None]
    u = np.arange(n)[None, :]
    seg_lo = (t // seg) * seg
    seg_hi = seg_lo + seg - 1
    mats = [(u >= seg_lo) & (u <= t),
            (u > t) & (u <= seg_hi)]
    level = np.full((n, n), -1, np.int32)
    s = u
    li = 0
    m = seg
    while m >= 2:
        half = m // 2
        upper = (t % m) >= half
        ref = (t // m) * m + half - 1
        if m > 2:
            mats.append(np.where(upper, (u > ref) & (u <= t), (u > t) & (u <= ref)))
        pair = (t // m == s // m) & ((t % m) >= half) & ((s % m) < half)
        level[pair] = li
        li += 1
        m //= 2
    level[np.arange(n), np.arange(n)] = li
    w = np.concatenate(mats, axis=0).astype(np.float32)
    w2 = np.concatenate([w, w], axis=1)
    return jnp.asarray(w2, BF16), jnp.asarray(level), li


def _hgrn_blocks(items, w2, level, n_lev, seg):
    n = HGRN_ROWS
    n_seg = n // seg
    row = lax.broadcasted_iota(jnp.int32, (n, B_DIM), 0)

    gates = []
    for q, xf, v, lb, states in items:
        f = lb + (1.0 - lb) * _sigmoid(xf)
        g = jnp.log2(f)
        g_hi = g.astype(BF16)
        g_lo = (g - g_hi.astype(F32)).astype(BF16)
        gates.append((f, jnp.concatenate([g_hi, g_lo], axis=0)))
    stage1 = []
    for i in range(0, len(items), 2):
        pair = gates[i:i + 2]
        rhs = pair[0][1] if len(pair) == 1 else jnp.concatenate([pair[0][1], pair[1][1]], axis=1)
        dsum = jnp.dot(w2, rhs, preferred_element_type=F32)
        for k, (f, _) in enumerate(pair):
            stage1.append((f, 1.0 - f, dsum[:, k * B_DIM:(k + 1) * B_DIM]))

    stage2 = []
    for (q, xf, v, lb, states), (f, kin, dsum) in zip(items, stage1):
        fac = jnp.exp2(dsum)
        ea = fac[0:n]
        ops = []
        m = seg
        for li in range(n_lev):
            half = m // 2
            if half % 8 == 0:
                src = jnp.concatenate([(kin if (r0 // half) % 2 == 0 else q)[r0:r0 + half]
                                       for r0 in range(0, n, half)], axis=0)
                tm = src * fac[(2 + li) * n:(3 + li) * n]
            elif m > 2:
                tm = jnp.where((row & half) != 0, q, kin) * fac[(2 + li) * n:(3 + li) * n]
            else:
                tm = jnp.where((row & half) != 0, q * f, kin)
            ops.append(tm.astype(BF16))
            m //= 2
        stage2.append((ea, (q * ea).astype(BF16), kin * fac[n:2 * n], ops))

    stage3 = []
    in_level = [level == li for li in range(n_lev + 1)]
    for (q, xf, v, lb, states), (f, kin, dsum), (ea, qt, kh, ops) in zip(items, stage1, stage2):
        att = jnp.zeros((n, n), F32)
        for li, tm in enumerate(ops):
            att = jnp.where(in_level[li], _nt(tm, tm), att)
        att = jnp.where(in_level[n_lev], jnp.sum(q * kin, axis=-1, keepdims=True), att)
        stage3.append(att.astype(BF16))

    results = []
    for (q, xf, v, lb, states), (ea, qt, kh, ops), att in zip(items, stage2, stage3):
        vb = v.astype(BF16)
        o = jnp.dot(att, vb, preferred_element_type=F32)
        new_states = []
        for si in range(n_seg):
            st = states[si]
            o_s = _nt(qt, st.astype(BF16))
            if n_seg == 1:
                o = o + o_s
                kh_s = kh
            else:
                in_seg = (row >= si * seg) & (row < (si + 1) * seg)
                o = o + jnp.where(in_seg, o_s, 0.0)
                kh_s = jnp.where(in_seg, kh, 0.0)
            decay = ea[(si + 1) * seg - 1:(si + 1) * seg, :]
            new_states.append(st * decay + _tn(vb, kh_s.astype(BF16)))
        results.append((o, new_states))
    return results


def _lower_bound(lbl_ref, layer):
    lg = lbl_ref[...]
    e = jnp.exp(lg - jnp.max(lg, axis=0, keepdims=True))
    p = e / jnp.sum(e, axis=0, keepdims=True)
    return jnp.sum(p[0:layer + 1], axis=0, keepdims=True) - p[0:1]


def _hgrn_prompt_kernel(q_ref, f_ref, v_ref, lbl_ref, gn_ref, w3_ref, lev_ref, o_ref, s_ref,
                        st_ref, *, layer, n_lev, tl, n_heads):
    li = pl.program_id(2)

    @pl.when(li == 0)
    def _():
        st_ref[...] = jnp.zeros_like(st_ref)

    lb = _lower_bound(lbl_ref, layer)
    w3 = w3_ref[...]
    level = lev_ref[...]
    gain = gn_ref[...]
    head = [slice(hh * B_DIM, (hh + 1) * B_DIM) for hh in range(n_heads)]

    def body(c, carry):
        r0 = pl.multiple_of(c * HGRN_ROWS, HGRN_ROWS)
        rows = pl.ds(r0, HGRN_ROWS)
        items = [(q_ref[0, rows, hs].astype(F32), f_ref[0, rows, hs].astype(F32),
                  v_ref[0, rows, hs].astype(F32), lb[:, hs], [st_ref[hh]])
                 for hh, hs in enumerate(head)]
        for hh, (o, (st,)) in enumerate(_hgrn_blocks(items, w3, level, n_lev, HGRN_ROWS)):
            st_ref[hh] = st
            o_ref[0, rows, head[hh]] = _rmsnorm_rows(o, gain).astype(o_ref.dtype)
        return carry

    lax.fori_loop(0, tl // HGRN_ROWS, body, 0, unroll=2)

    @pl.when(li == pl.num_programs(2) - 1)
    def _():
        for hh in range(n_heads):
            s_ref[0, hh] = st_ref[hh].T


def _hgrn_prompt(z3, col, lb_logits, hgrn_gain, layer, *, tl):
    b, s, _ = z3.shape
    w3, level, n_lev = _hgrn_consts(HGRN_ROWS)
    nh = HGRN_HEADS_PER_STEP
    wblk = nh * B_DIM

    def cspec(name):
        c0 = col[name] // wblk
        return pl.BlockSpec((1, tl, wblk), lambda bb, h, i: (bb, i, c0 + h))

    const = lambda shape: pl.BlockSpec(shape, lambda bb, h, i: (0,) * len(shape))
    return pl.pallas_call(
        functools.partial(_hgrn_prompt_kernel, layer=layer, n_lev=n_lev, tl=tl, n_heads=nh),
        grid=(b, B_HEADS // nh, s // tl),
        in_specs=[cspec("bq"), cspec("bf"), cspec("bi"),
                  pl.BlockSpec((lb_logits.shape[0], wblk), lambda bb, h, i: (0, h)),
                  const((1, B_DIM)), const(w3.shape), const(level.shape)],
        out_specs=[pl.BlockSpec((1, tl, wblk), lambda bb, h, i: (bb, i, h)),
                   pl.BlockSpec((1, nh, B_DIM, B_DIM), lambda bb, h, i: (bb, h, 0, 0))],
        out_shape=[jax.ShapeDtypeStruct((b, s, B_WIDTH), BF16),
                   jax.ShapeDtypeStruct((b, B_HEADS, B_DIM, B_DIM), F32)],
        scratch_shapes=[pltpu.VMEM((nh, B_DIM, B_DIM), F32)],
        compiler_params=_cparams(("parallel", "parallel", "arbitrary")),
        name="hgrn_prompt",
    )(z3, z3, z3, lb_logits, hgrn_gain.reshape(1, B_DIM), w3, level)


def _hgrn_sample_kernel(q_ref, f_ref, v_ref, s0_ref, lbl_ref, gn_ref, w3_ref, lev_ref, acc_ref,
                        o_ref, s_ref, *, layer, n_lev, t_new, n_heads):
    del acc_ref
    lb = _lower_bound(lbl_ref, layer)
    n_seg = HGRN_ROWS // t_new
    head = [slice(hh * B_DIM, (hh + 1) * B_DIM) for hh in range(n_heads)]
    items = [(q_ref[:, hs], f_ref[:, hs], v_ref[:, hs], lb[:, hs],
              [s0_ref[0, si, hh].T for si in range(n_seg)])
             for hh, hs in enumerate(head)]
    results = _hgrn_blocks(items, w3_ref[...], lev_ref[...], n_lev, t_new)
    for hh, (o, new_states) in enumerate(results):
        o_ref[:, head[hh]] = _rmsnorm_rows(o, gn_ref[...]).astype(o_ref.dtype)
        for si in range(n_seg):
            s_ref[0, si, hh] = new_states[si].T


def _hgrn_sample(z_s, state, new_state, lb_logits, hgrn_gain, layer, *, n_seq, t_new):
    w3, level, n_lev = _hgrn_consts(t_new)
    n_seg = HGRN_ROWS // t_new
    nh = HGRN_SAMPLE_HEADS_PER_STEP
    wblk = nh * B_DIM

    def cspec(name):
        c0 = _COL_S[name] // wblk
        return pl.BlockSpec((HGRN_ROWS, wblk), lambda i, h: (i, c0 + h))

    const = lambda shape: pl.BlockSpec(shape, lambda i, h: (0,) * len(shape))
    return pl.pallas_call(
        functools.partial(_hgrn_sample_kernel, layer=layer, n_lev=n_lev, t_new=t_new, n_heads=nh),
        grid=(n_seq // n_seg, B_HEADS // nh),
        in_specs=[cspec("bq"), cspec("bf"), cspec("bi"),
                  pl.BlockSpec((1, n_seg, nh, B_DIM, B_DIM), lambda i, h, layer=layer: (layer, i, h, 0, 0)),
                  pl.BlockSpec((lb_logits.shape[0], wblk), lambda i, h: (0, h)),
                  const((1, B_DIM)), const(w3.shape), const(level.shape),
                  pl.BlockSpec(memory_space=pl.ANY)],
        out_specs=[pl.BlockSpec((HGRN_ROWS, wblk), lambda i, h: (i, h)),
                   pl.BlockSpec((1, n_seg, nh, B_DIM, B_DIM), lambda i, h, layer=layer: (layer, i, h, 0, 0))],
        out_shape=[jax.ShapeDtypeStruct((n_seq * t_new, B_WIDTH), BF16),
                   jax.ShapeDtypeStruct(new_state.shape, F32)],
        input_output_aliases={8: 1},
        compiler_params=_cparams(("parallel", "parallel")),
        name="hgrn_sample",
    )(z_s, z_s, z_s, state, lb_logits, hgrn_gain.reshape(1, B_DIM), w3, level, new_state)


_POOL_PAD = 16


def _pool_mix(ext_at, u, pos, pw_ref, ps_ref):
    outs = []
    for g, win in enumerate(C_WINDOWS):
        gs = slice(g * C_GROUP_DIM, (g + 1) * C_GROUP_DIM)
        acc = u[:, gs]
        for k in range(1, win):
            acc = acc + ext_at(k, gs)
        cnt = jnp.minimum(pos + 1, win).astype(F32)
        pooled = acc * (1.0 / cnt) - u[:, gs]
        mixed = jnp.dot(pooled.astype(BF16), pw_ref[g].astype(BF16), preferred_element_type=F32)
        outs.append(mixed * ps_ref[:, gs])
    return outs


def _pool_prompt_kernel(u_ref, h_ref, pw_ref, ps_ref, o_ref, ext_ref, *, tl):
    i = pl.program_id(1)
    halo = h_ref[0].astype(F32)
    ext_ref[0:_POOL_PAD, :] = jnp.where(i > 0, halo, 0.0)
    u = u_ref[0].astype(F32)
    ext_ref[_POOL_PAD:, :] = u
    pos = i * tl + lax.broadcasted_iota(jnp.int32, (tl, 1), 0)
    outs = _pool_mix(lambda k, gs: ext_ref[_POOL_PAD - k:_POOL_PAD - k + tl, gs], u, pos, pw_ref, ps_ref)
    for g, og in enumerate(outs):
        o_ref[0, :, g * C_GROUP_DIM:(g + 1) * C_GROUP_DIM] = og.astype(o_ref.dtype)


def _pool_prompt(z3, col, pool_w, pool_scale, *, tl):
    b, s, _ = z3.shape
    c0 = col["cu"] // C_WIDTH
    per = tl // _POOL_PAD
    return pl.pallas_call(
        functools.partial(_pool_prompt_kernel, tl=tl),
        grid=(b, s // tl),
        in_specs=[pl.BlockSpec((1, tl, C_WIDTH), lambda bb, i: (bb, i, c0)),
                  pl.BlockSpec((1, _POOL_PAD, C_WIDTH), lambda bb, i: (bb, jnp.maximum(i * per - 1, 0), c0)),
                  pl.BlockSpec((C_GROUPS, C_GROUP_DIM, C_GROUP_DIM), lambda bb, i: (0, 0, 0)),
                  pl.BlockSpec((1, C_WIDTH), lambda bb, i: (0, 0))],
        out_specs=pl.BlockSpec((1, tl, C_WIDTH), lambda bb, i: (bb, i, 0)),
        out_shape=jax.ShapeDtypeStruct((b, s, C_WIDTH), BF16),
        scratch_shapes=[pltpu.VMEM((_POOL_PAD + tl, C_WIDTH), F32)],
        compiler_params=_cparams(("parallel", "arbitrary")),
        name="pool_prompt",
    )(z3, z3, pool_w, pool_scale.reshape(1, C_WIDTH))


def _pool_sample_kernel(u_ref, h_ref, pw_ref, ps_ref, o_ref, ext_ref, *, nb, t_new, first_pos):
    ext_ref[:, 0:1, :] = jnp.zeros((nb, 1, C_WIDTH), F32)
    ext_ref[:, 1:_POOL_PAD, :] = h_ref[0]
    u3 = u_ref[...].reshape(nb, t_new, C_WIDTH)
    ext_ref[:, _POOL_PAD:, :] = u3
    u = u_ref[...]
    pos = first_pos + lax.broadcasted_iota(jnp.int32, (nb * t_new, 1), 0) % t_new

    def ext_at(k, gs):
        return ext_ref[:, _POOL_PAD - k:_POOL_PAD - k + t_new, gs].reshape(nb * t_new, C_GROUP_DIM)

    outs = _pool_mix(ext_at, u, pos, pw_ref, ps_ref)
    for g, og in enumerate(outs):
        o_ref[:, g * C_GROUP_DIM:(g + 1) * C_GROUP_DIM] = og.astype(o_ref.dtype)


def _pool_sample(z_s, state_pool, pool_w, pool_scale, layer, *, n_seq, t_new, first_pos, nb):
    c0 = _COL_S["cu"] // C_WIDTH
    return pl.pallas_call(
        functools.partial(_pool_sample_kernel, nb=nb, t_new=t_new, first_pos=first_pos),
        grid=(n_seq // nb,),
        in_specs=[pl.BlockSpec((nb * t_new, C_WIDTH), lambda i: (i, c0)),
                  pl.BlockSpec((1, nb, C_HIST, C_WIDTH), lambda i, layer=layer: (layer, i, 0, 0)),
                  pl.BlockSpec((C_GROUPS, C_GROUP_DIM, C_GROUP_DIM), lambda i: (0, 0, 0)),
                  pl.BlockSpec((1, C_WIDTH), lambda i: (0, 0))],
        out_specs=pl.BlockSpec((nb * t_new, C_WIDTH), lambda i: (i, 0)),
        out_shape=jax.ShapeDtypeStruct((n_seq * t_new, C_WIDTH), BF16),
        scratch_shapes=[pltpu.VMEM((nb, _POOL_PAD + t_new, C_WIDTH), F32)],
        compiler_params=_cparams(("parallel",)),
        name="pool_sample",
    )(z_s, state_pool, pool_w, pool_scale.reshape(1, C_WIDTH))


def _token_order(ref, scr_ref, d, tm):
    n_chunk = ref.shape[-1] // LANE
    if d == 1:
        return [ref[0, 0, :, c * LANE:(c + 1) * LANE] for c in range(n_chunk)]
    for r in range(d):
        for c in range(n_chunk):
            scr_ref[c, pl.ds(r, tm // d, stride=d), :] = ref[0, r, :, c * LANE:(c + 1) * LANE]
    return [scr_ref[c] for c in range(n_chunk)]


def _out_proj_kernel(x_ref, ag_ref, bg_ref, cg_ref, m0_ref, m1_ref, m2_ref, o0_ref, o1_ref, o2_ref,
                     l0_ref, l1_ref, l2_ref, ob_ref, oc_ref, wa_ref, wb_ref, wc_ref, wo_ref,
                     gp_ref, y_ref, *scratch, dil, tm):
    scr = iter(scratch)
    o_g, l_g = [], []
    for o_ref, l_ref, d in zip((o0_ref, o1_ref, o2_ref), (l0_ref, l1_ref, l2_ref), dil):
        so = next(scr) if d > 1 else None
        sl = next(scr) if d > 1 else None
        o_g.append(_token_order(o_ref, so, d, tm))
        l_g.append(_token_order(l_ref, sl, d, tm)[0])
    heads = []
    for h in range(A_HEADS):
        l0, l1, l2 = (l[:, h:h + 1] for l in l_g)
        mx = jnp.maximum(jnp.maximum(l0, l1), l2)
        e0, e1, e2 = jnp.exp(l0 - mx), jnp.exp(l1 - mx), jnp.exp(l2 - mx)
        inv = 1.0 / (e0 + e1 + e2)
        heads.append((e0 * inv) * o_g[0][h] + (e1 * inv) * o_g[1][h] + (e2 * inv) * o_g[2][h])
    o_a = jnp.concatenate(heads, axis=1)

    def silu_gated(o, g_ref):
        h = g_ref[...].astype(F32) * 0.5
        return (o.astype(F32) * (h + h * jnp.tanh(h))).astype(BF16)

    def merge_gated(m_ref, y, w_ref):
        p = jnp.dot(y, w_ref[...], preferred_element_type=F32)
        return p + jnp.tanh(m_ref[...].astype(F32) * 0.5) * p

    merged = (merge_gated(m0_ref, silu_gated(o_a, ag_ref), wa_ref)
              + merge_gated(m1_ref, silu_gated(ob_ref[...], bg_ref), wb_ref)
              + merge_gated(m2_ref, silu_gated(oc_ref[...], cg_ref), wc_ref)) * 0.5
    out = jnp.dot(merged.astype(BF16), wo_ref[...], preferred_element_type=F32)
    y_ref[...] = x_ref[...] + _rmsnorm_rows(out, gp_ref[...])


def _out_proj(x, z, col, o_g, lse_g, o_b, o_c, wa, wb, wc, wo, gain_post, *, tm):
    n = x.shape[0]
    dil = tuple(int(a.shape[1]) for a in o_g)
    tpb = (n // o_g[0].shape[0]) // tm

    def zcol(name, width, k=0):
        assert col[name] % width == 0
        c0 = col[name] // width + k
        return pl.BlockSpec((tm, width), lambda i: (i, c0))

    def gspec(d, width):
        return pl.BlockSpec((1, d, tm // d, width), lambda i: (i // tpb, 0, i % tpb, 0))

    rows = lambda width: pl.BlockSpec((tm, width), lambda i: (i, 0))
    const = lambda a: pl.BlockSpec(a.shape, lambda i: (0,) * a.ndim)
    gp = gain_post.reshape(1, D_MODEL)
    scratch = []
    for d in dil:
        if d > 1:
            scratch += [pltpu.VMEM((A_WIDTH // LANE, tm, LANE), F32), pltpu.VMEM((1, tm, LANE), F32)]
    return pl.pallas_call(
        functools.partial(_out_proj_kernel, dil=dil, tm=tm),
        grid=(n // tm,),
        in_specs=[rows(D_MODEL), zcol("ag", A_WIDTH), zcol("bg", B_WIDTH), zcol("cg", C_WIDTH),
                  zcol("mg", D_MODEL, 0), zcol("mg", D_MODEL, 1), zcol("mg", D_MODEL, 2)]
                 + [gspec(d, A_WIDTH) for d in dil] + [gspec(d, LANE) for d in dil]
                 + [rows(B_WIDTH), rows(C_WIDTH)]
                 + [const(wa), const(wb), const(wc), const(wo), const(gp)],
        out_specs=rows(D_MODEL),
        out_shape=jax.ShapeDtypeStruct((n, D_MODEL), F32),
        scratch_shapes=scratch,
        compiler_params=_cparams(("parallel",)),
        name="out_proj",
    )(x, z, z, z, z, z, z, *o_g, *lse_g, o_b, o_c, wa, wb, wc, wo, gp)


def _ref_cols(w, name, g=None):
    off = _REF_OFF[name]
    if g is None:
        return w[:, off:off + _WIDTHS[name]]
    return w[:, off + g * A_WIDTH:off + (g + 1) * A_WIDTH]


def _prep_w_in(w):
    parts = []
    for g in range(1, A_GROUPS):
        parts += [_ref_cols(w, "ak", g), _ref_cols(w, "av", g), _ref_cols(w, "aq", g)]
    parts += [_ref_cols(w, "ak", 0), _ref_cols(w, "av", 0), _ref_cols(w, "cu"), _ref_cols(w, "cg"),
              _ref_cols(w, "aq", 0)] + [_ref_cols(w, k) for k in ("ag", "bq", "bf", "bi", "bg", "mg")]
    return jnp.concatenate(parts, axis=1).astype(BF16)


def _tile_plan(s_p, n_s):
    attn = []
    for _, d in A_PATTERNS:
        tq = min(1024, s_p // d)
        attn.append((tq, max(1, min(d, 1024 // tq))))
    return dict(
        in_proj_p=(min(2048, s_p), A_WIDTH),
        in_proj_s=(min(1024, n_s), 1024),
        perm=min(1024, s_p),
        kv_rows=1024,
        attn=attn,
        hgrn=min(2048, s_p),
        pool=min(2048, s_p),
        out_p=min(512, s_p),
        out_s=min(256, n_s),
    )


def _kv_split(kv):
    return kv.reshape(kv.shape[:-1] + (2, A_HEADS, A_HEAD_DIM))


def kernel(x_prompt, x_sample, cache_kv_w128, cache_kv_w512, cache_kv_w2048, state_hgrn, state_pool,
           norm_pre, norm_post, w_in, hgrn_lb_logits, hgrn_norm, pool_w, pool_scale,
           w_br_a, w_br_b, w_br_c, w_out):
    depth = w_in.shape[0]
    b_p, s_p, _ = x_prompt.shape
    n_seq, t_new, _ = x_sample.shape
    caches = (cache_kv_w128, cache_kv_w512, cache_kv_w2048)
    past_len = cache_kv_w2048.shape[2]
    assert HGRN_ROWS % t_new == 0 and n_seq % (HGRN_ROWS // t_new) == 0
    n_p, n_s = b_p * s_p, n_seq * t_new

    hp = x_prompt.reshape(n_p, D_MODEL)
    hs = x_sample.reshape(n_s, D_MODEL)
    kv_p = [[] for _ in range(A_GROUPS)]
    kv_s = [[] for _ in range(A_GROUPS)]
    hg_p, pl_p, pl_s = [], [], []
    hg_s = jnp.zeros(state_hgrn.shape, F32)
    tiles = _tile_plan(s_p, n_s)
    for l in range(depth):
        w_l = _prep_w_in(w_in[l])
        wa, wb, wc, wo = (w[l].astype(BF16) for w in (w_br_a, w_br_b, w_br_c, w_out))

        (z_p,) = _in_proj(hp, norm_pre[l], w_l, col0=_P_COL0, tm=tiles["in_proj_p"][0],
                          tn=tiles["in_proj_p"][1], z_dtype=BF16, f32_cols=0)
        z3 = z_p.reshape(b_p, s_p, _P_WIDTH)

        (u_tail,) = _in_proj(hp, norm_pre[l], w_l, col0=_COL_S["cu"], width=C_WIDTH,
                             n_rows=b_p * _POOL_PAD, row_block=_tail_row_block(s_p, _POOL_PAD, _POOL_PAD),
                             tm=_POOL_PAD, tn=C_WIDTH, z_dtype=F32, f32_cols=0)
        pl_p.append(u_tail.reshape(b_p, _POOL_PAD, C_WIDTH)[:, _POOL_PAD - C_HIST:])
        o_g, lse_g = [], []
        for g, (win, d) in enumerate(A_PATTERNS):
            rows = min(win, s_p)
            kv = _kv_rows(hp, norm_pre[l], w_l, _attn_cols(g)[0], n_seq=b_p, seq=s_p, rows=rows,
                          tm=min(tiles["kv_rows"], rows))
            kv_p[g].append(kv.reshape(b_p, rows, 2, A_HEADS, A_HEAD_DIM))
            if g == 0:
                arr = z_p.reshape(b_p, 1, s_p, _P_WIDTH)
                cols = (_COL_P["aq"] // A_WIDTH, _COL_P["ak"] // A_WIDTH, _COL_P["av"] // A_WIDTH)
            else:
                arr = _in_proj_perm(hp, norm_pre[l], w_l, g, b=b_p, s=s_p, d=d, tm=tiles["perm"])
                cols = (2, 0, 1)
            tq, rpb = tiles["attn"][g]
            o, lse = _attn_prompt(arr, cols, tq=tq, rpb=rpb)
            o_g.append(o)
            lse_g.append(lse)
        o_b, s_new = _hgrn_prompt(z3, _COL_P, hgrn_lb_logits, hgrn_norm[l], l, tl=tiles["hgrn"])
        hg_p.append(s_new)
        o_c = _pool_prompt(z3, _COL_P, pool_w[l], pool_scale[l], tl=tiles["pool"])
        hp = _out_proj(hp, z_p, _COL_P, o_g, lse_g, o_b.reshape(n_p, B_WIDTH),
                       o_c.reshape(n_p, C_WIDTH), wa, wb, wc, wo, norm_post[l], tm=tiles["out_p"])

        (z_s,) = _in_proj(hs, norm_pre[l], w_l, col0=0, tm=tiles["in_proj_s"][0],
                          tn=tiles["in_proj_s"][1], z_dtype=F32, f32_cols=0)
        o_g, lse_g = _attn_sample(z_s, caches, l, n_seq=n_seq, t_new=t_new)
        zs3 = z_s.reshape(n_seq, t_new, IN_WIDTH)
        for g in range(A_GROUPS):
            k_off, v_off, _ = _attn_cols(g)
            kv_s[g].append(_kv_split(zs3[:, :, k_off:v_off + A_WIDTH]))
        o_b, hg_s = _hgrn_sample(z_s, state_hgrn, hg_s, hgrn_lb_logits, hgrn_norm[l], l,
                                 n_seq=n_seq, t_new=t_new)
        o_c = _pool_sample(z_s, state_pool, pool_w[l], pool_scale[l], l, n_seq=n_seq,
                           t_new=t_new, first_pos=past_len, nb=HGRN_ROWS // t_new)
        u_new = zs3[:, :, _COL_S["cu"]:_COL_S["cu"] + C_WIDTH]
        pl_s.append(jnp.concatenate([state_pool[l], u_new], axis=1)[:, -C_HIST:])
        as4 = lambda a: a.reshape(1, 1, n_s, a.shape[-1])
        hs = _out_proj(hs, z_s, _COL_S, [as4(a) for a in o_g], [as4(a) for a in lse_g], o_b, o_c,
                       wa, wb, wc, wo, norm_post[l], tm=tiles["out_s"])

    return (hp.reshape(b_p, s_p, D_MODEL), hs.reshape(n_seq, t_new, D_MODEL),
            jnp.stack(kv_p[0]), jnp.stack(kv_p[1]), jnp.stack(kv_p[2]),
            jnp.stack(kv_s[0]), jnp.stack(kv_s[1]), jnp.stack(kv_s[2]),
            jnp.stack(hg_p), hg_s, jnp.stack(pl_p), jnp.stack(pl_s))
```

```python
import functools

import numpy as np
import jax
import jax.numpy as jnp
from jax import lax
from jax.experimental import pallas as pl
from jax.experimental.pallas import tpu as pltpu

F32 = jnp.float32
BF16 = jnp.bfloat16

D_MODEL = 1024
A_PATTERNS = ((128, 1), (512, 4), (2048, 16))
A_GROUPS = 3
A_HEADS = 4
A_HEAD_DIM = 128
A_WIDTH = A_HEADS * A_HEAD_DIM
N_STEPS = 128
B_HEADS = 8
B_DIM = 128
B_WIDTH = B_HEADS * B_DIM
C_WINDOWS = (2, 4, 8, 16)
C_GROUPS = 4
C_GROUP_DIM = 128
C_WIDTH = C_GROUPS * C_GROUP_DIM
C_HIST = 15
N_BRANCH = 3
EPS = 1e-6
NEG = -1e30
IN_WIDTH = 13312
LANE = 128
HGRN_ROWS = 128
HGRN_HEADS_PER_STEP = 8
HGRN_SAMPLE_HEADS_PER_STEP = 4

_REF_OFF = dict(aq=0, ak=1536, av=3072, ag=4608, bq=5120, bf=6144, bi=7168, bg=8192,
                cu=9216, cg=9728, mg=10240)
_WIDTHS = dict(aq=1536, ak=1536, av=1536, ag=512, bq=1024, bf=1024, bi=1024, bg=1024,
               cu=512, cg=512, mg=3072)

_PERM_WIDTH = 3 * A_WIDTH
_P_COL0 = (A_GROUPS - 1) * _PERM_WIDTH
_COL_P = dict(ak=0, av=512, cu=1024, cg=1536, aq=2048, ag=2560, bq=3072, bf=4096,
              bi=5120, bg=6144, mg=7168)
_P_WIDTH = IN_WIDTH - _P_COL0
_COL_S = {name: _P_COL0 + off for name, off in _COL_P.items()}


def _attn_cols(g):
    if g == 0:
        return _COL_S["ak"], _COL_S["av"], _COL_S["aq"]
    base = (g - 1) * _PERM_WIDTH
    return base, base + A_WIDTH, base + 2 * A_WIDTH

_VMEM_LIMIT = 56 * 1024 * 1024


def _cparams(sem):
    return pltpu.CompilerParams(dimension_semantics=sem, vmem_limit_bytes=_VMEM_LIMIT)


def _nt(a, b):
    return lax.dot_general(a, b, (((1,), (1,)), ((), ())), preferred_element_type=F32)


def _tn(a, b):
    return lax.dot_general(a, b, (((0,), (0,)), ((), ())), preferred_element_type=F32)


def _sigmoid(x):
    return 1.0 / (1.0 + jnp.exp(-x))


def _rmsnorm_rows(x, gain):
    return x * lax.rsqrt(jnp.mean(x * x, axis=-1, keepdims=True) + EPS) * gain


def _in_proj_kernel(x_ref, g_ref, w_ref, *refs, n_f32_tiles, z_dtype):
    if n_f32_tiles:
        z_ref, zf_ref, xn_ref = refs
    else:
        z_ref, xn_ref = refs
    j = pl.program_id(1)

    @pl.when(j == 0)
    def _():
        xn_ref[...] = _rmsnorm_rows(x_ref[...], g_ref[...]).astype(BF16)

    acc = jnp.dot(xn_ref[...], w_ref[...], preferred_element_type=F32)
    z_ref[...] = acc.astype(z_dtype)
    if n_f32_tiles:
        @pl.when(j < n_f32_tiles)
        def _():
            zf_ref[...] = acc


def _tail_row_block(seq, rows, tm):
    per_seq, tiles, first = seq // tm, rows // tm, (seq - rows) // tm
    assert seq % tm == 0 and rows % tm == 0
    return lambda i: (i // tiles) * per_seq + first + i % tiles


def _in_proj(x, gain, w_bf16, *, col0, tm, tn, z_dtype, f32_cols, width=None, n_rows=None,
             row_block=lambda i: i):
    n = x.shape[0] if n_rows is None else n_rows
    width = w_bf16.shape[1] - col0 if width is None else width
    j0 = col0 // tn
    n_f32_tiles = f32_cols // tn
    out_shape = [jax.ShapeDtypeStruct((n, width), z_dtype)]
    out_specs = [pl.BlockSpec((tm, tn), lambda i, j: (i, j))]
    if n_f32_tiles:
        out_shape.append(jax.ShapeDtypeStruct((n, f32_cols), F32))
        out_specs.append(pl.BlockSpec((tm, tn), lambda i, j: (i, jnp.minimum(j, n_f32_tiles - 1))))
    return pl.pallas_call(
        functools.partial(_in_proj_kernel, n_f32_tiles=n_f32_tiles, z_dtype=z_dtype),
        grid=(n // tm, width // tn),
        in_specs=[pl.BlockSpec((tm, D_MODEL), lambda i, j: (row_block(i), 0)),
                  pl.BlockSpec((1, D_MODEL), lambda i, j: (0, 0)),
                  pl.BlockSpec((D_MODEL, tn), lambda i, j: (0, j + j0))],
        out_specs=out_specs,
        out_shape=out_shape,
        scratch_shapes=[pltpu.VMEM((tm, D_MODEL), BF16)],
        compiler_params=_cparams(("parallel", "arbitrary")),
        name="in_proj",
    )(x, gain.reshape(1, D_MODEL), w_bf16)


def _kv_rows_kernel(x_ref, g_ref, wk_ref, wv_ref, o_ref, *, tm):
    xn = _rmsnorm_rows(x_ref[...], g_ref[...]).astype(BF16)
    per_tok = 2 * A_HEADS
    for kv, w_ref in enumerate((wk_ref, wv_ref)):
        acc = jnp.dot(xn, w_ref[...], preferred_element_type=F32)
        for h in range(A_HEADS):
            o_ref[pl.ds(kv * A_HEADS + h, tm, stride=per_tok), :] = (
                acc[:, h * A_HEAD_DIM:(h + 1) * A_HEAD_DIM])


def _kv_rows(x, gain, w_bf16, k_col, *, n_seq, seq, rows, tm):
    n = n_seq * rows
    per_tok = 2 * A_HEADS
    kb = k_col // A_WIDTH
    row_block = _tail_row_block(seq, rows, tm)
    return pl.pallas_call(
        functools.partial(_kv_rows_kernel, tm=tm),
        grid=(n // tm,),
        in_specs=[pl.BlockSpec((tm, D_MODEL), lambda i: (row_block(i), 0)),
                  pl.BlockSpec((1, D_MODEL), lambda i: (0, 0)),
                  pl.BlockSpec((D_MODEL, A_WIDTH), lambda i: (0, kb)),
                  pl.BlockSpec((D_MODEL, A_WIDTH), lambda i: (0, kb + 1))],
        out_specs=pl.BlockSpec((tm * per_tok, A_HEAD_DIM), lambda i: (i, 0)),
        out_shape=jax.ShapeDtypeStruct((n * per_tok, A_HEAD_DIM), F32),
        compiler_params=_cparams(("parallel",)),
        name="kv_rows",
    )(x, gain.reshape(1, D_MODEL), w_bf16, w_bf16)


def _in_proj_perm_kernel(x_ref, g_ref, w_ref, r_ref, xs_ref, xn_ref, *, d, tm):
    xn = _rmsnorm_rows(x_ref[...], g_ref[...])
    n_chunk = D_MODEL // LANE
    rows = tm // d
    for c in range(n_chunk):
        xs_ref[c] = xn[:, c * LANE:(c + 1) * LANE]
    for r in range(d):
        for c in range(n_chunk):
            xn_ref[r * rows:(r + 1) * rows, c * LANE:(c + 1) * LANE] = (
                xs_ref[c, pl.ds(r, rows, stride=d), :].astype(BF16))
    acc = jnp.dot(xn_ref[...], w_ref[...], preferred_element_type=F32)
    r_ref[0] = acc.astype(BF16).reshape(d, rows, _PERM_WIDTH)


def _in_proj_perm(x, gain, w_bf16, g, *, b, s, d, tm):
    tpb = s // tm
    rows = tm // d
    idx = lambda i: (i // tpb, 0, i % tpb, 0)
    return pl.pallas_call(
        functools.partial(_in_proj_perm_kernel, d=d, tm=tm),
        grid=(b * tpb,),
        in_specs=[pl.BlockSpec((tm, D_MODEL), lambda i: (i, 0)),
                  pl.BlockSpec((1, D_MODEL), lambda i: (0, 0)),
                  pl.BlockSpec((D_MODEL, _PERM_WIDTH), lambda i: (0, g - 1),
                               pipeline_mode=pl.Buffered(1))],
        out_specs=pl.BlockSpec((1, d, rows, _PERM_WIDTH), idx),
        out_shape=jax.ShapeDtypeStruct((b, d, s // d, _PERM_WIDTH), BF16),
        scratch_shapes=[pltpu.VMEM((D_MODEL // LANE, tm, LANE), F32),
                        pltpu.VMEM((tm, D_MODEL), BF16)],
        compiler_params=_cparams(("parallel",)),
        name="in_proj_d%d" % d,
    )(x, gain.reshape(1, D_MODEL), w_bf16)


def _attn_prompt_kernel(q_ref, kp_ref, kc_ref, vp_ref, vc_ref, o_ref, lse_ref, *, tq):
    i = pl.program_id(2)
    n = N_STEPS
    qi = lax.broadcasted_iota(jnp.int32, (n, 2 * n), 0)
    kj = lax.broadcasted_iota(jnp.int32, (n, 2 * n), 1)
    band = (kj >= qi) & (kj <= qi + n)
    band_first = band & ((i > 0) | (kj >= n))
    scale = A_HEAD_DIM ** -0.5
    lane = lax.broadcasted_iota(jnp.int32, (n, LANE), 1)
    for rr, j in [(rr, j) for rr in range(q_ref.shape[1]) for j in range(tq // n)]:
        rows = slice(j * n, (j + 1) * n)
        lse = jnp.zeros((n, LANE), F32)
        for h in range(A_HEADS):
            hs = slice(h * A_HEAD_DIM, (h + 1) * A_HEAD_DIM)
            q = q_ref[0, rr, rows, hs]
            if j == 0:
                k = jnp.concatenate([kp_ref[0, rr, :, hs], kc_ref[0, rr, rows, hs]], axis=0)
                v = jnp.concatenate([vp_ref[0, rr, :, hs], vc_ref[0, rr, rows, hs]], axis=0)
                valid = band_first
            else:
                k = kc_ref[0, rr, (j - 1) * n:(j + 1) * n, hs]
                v = vc_ref[0, rr, (j - 1) * n:(j + 1) * n, hs]
                valid = band
            s = jnp.where(valid, _nt(q, k) * scale, NEG)
            m = jnp.max(s, axis=-1, keepdims=True)
            p = jnp.exp(s - m)
            l = jnp.sum(p, axis=-1, keepdims=True)
            o = jnp.dot(p.astype(BF16), v, preferred_element_type=F32)
            o_ref[0, rr, rows, hs] = o / l
            lse = jnp.where(lane == h, m + jnp.log(l), lse)
        lse_ref[0, rr, rows, :] = lse


def _attn_prompt(arr, cols, *, tq, rpb):
    b, d, rows, _ = arr.shape
    qc, kc, vc = cols
    prev_per_blk = tq // N_STEPS

    def cur(c):
        return pl.BlockSpec((1, rpb, tq, A_WIDTH), lambda bb, r, i: (bb, r, i, c))

    def prev(c):
        return pl.BlockSpec((1, rpb, N_STEPS, A_WIDTH),
                            lambda bb, r, i: (bb, r, jnp.maximum(i * prev_per_blk - 1, 0), c))

    out_spec = pl.BlockSpec((1, rpb, tq, A_WIDTH), lambda bb, r, i: (bb, r, i, 0))
    return pl.pallas_call(
        functools.partial(_attn_prompt_kernel, tq=tq),
        grid=(b, d // rpb, rows // tq),
        in_specs=[cur(qc), prev(kc), cur(kc), prev(vc), cur(vc)],
        out_specs=[out_spec, pl.BlockSpec((1, rpb, tq, LANE), lambda bb, r, i: (bb, r, i, 0))],
        out_shape=[jax.ShapeDtypeStruct((b, d, rows, A_WIDTH), F32),
                   jax.ShapeDtypeStruct((b, d, rows, LANE), F32)],
        compiler_params=_cparams(("parallel", "parallel", "arbitrary")),
        name="attn_prompt_d%d" % d,
    )(arr, arr, arr, arr, arr)


def _gather_pitch(n_j):
    p = -(-n_j // 4)
    return 4 * (p if p % 2 else p + 1)


def _attn_sample_kernel(z_ref, c0_ref, c1_ref, c2_ref, *refs, t_new):
    out_refs, flat_refs = refs[:2 * A_GROUPS], refs[2 * A_GROUPS:]
    scale = A_HEAD_DIM ** -0.5
    nrow = A_HEADS * t_new
    per_res = 2 * A_HEADS
    for g, c_ref in enumerate((c0_ref, c1_ref, c2_ref)):
        d = A_PATTERNS[g][1]
        n_res = min(d, t_new)
        o_ref, lse_ref = out_refs[2 * g], out_refs[2 * g + 1]
        k_off, v_off, q_off = _attn_cols(g)
        q = z_ref[:, q_off:q_off + A_WIDTH]
        kn = z_ref[:, k_off:k_off + A_WIDTH]
        vn = z_ref[:, v_off:v_off + A_WIDTH]
        n_j = n_res * per_res
        pitch = _gather_pitch(n_j)
        flat_ref = flat_refs[g]
        for mm in range(N_STEPS):
            flat_ref[mm * pitch:mm * pitch + n_j, :] = c_ref[mm]
        gather = lambda j: flat_ref[pl.ds(j, N_STEPS, stride=pitch), :]
        kcat = jnp.concatenate([gather(r * per_res + h)
                                for r in range(n_res) for h in range(A_HEADS)], axis=1).astype(BF16)
        vcat = jnp.concatenate([gather(r * per_res + A_HEADS + h)
                                for r in range(n_res) for h in range(A_HEADS)], axis=1).astype(BF16)
        wide = n_res * A_WIDTH
        q4 = jnp.concatenate([q] * A_HEADS, axis=0)
        row = lax.broadcasted_iota(jnp.int32, (nrow, wide), 0)
        col = lax.broadcasted_iota(jnp.int32, (nrow, wide), 1)
        row_h, row_t = row // t_new, row % t_new
        sel = ((col // A_WIDTH) == (row_t % d)) & (((col % A_WIDTH) // A_HEAD_DIM) == row_h)
        qbd = jnp.where(sel, jnp.concatenate([q4] * n_res, axis=1), 0.0).astype(BF16)
        s_c = _nt(qbd, kcat) * scale
        r2 = lax.broadcasted_iota(jnp.int32, (nrow, N_STEPS), 0) % t_new
        c2 = lax.broadcasted_iota(jnp.int32, (nrow, N_STEPS), 1)
        s_c = jnp.where(c2 >= r2 // d, s_c, NEG)
        pad = jnp.zeros((N_STEPS - t_new, A_WIDTH), F32)
        knp = jnp.concatenate([kn, pad], axis=0).astype(BF16)
        vnp = jnp.concatenate([vn, pad], axis=0).astype(BF16)
        rowq = lax.broadcasted_iota(jnp.int32, (nrow, A_WIDTH), 0) // t_new
        colq = lax.broadcasted_iota(jnp.int32, (nrow, A_WIDTH), 1) // A_HEAD_DIM
        selq = rowq == colq
        qbd2 = jnp.where(selq, q4, 0.0).astype(BF16)
        s_n = _nt(qbd2, knp) * scale
        s_n = jnp.where((c2 <= r2) & ((r2 - c2) % d == 0), s_n, NEG)
        m = jnp.maximum(jnp.max(s_c, axis=-1, keepdims=True), jnp.max(s_n, axis=-1, keepdims=True))
        p_c = jnp.exp(s_c - m)
        p_n = jnp.exp(s_n - m)
        l = jnp.sum(p_c, axis=-1, keepdims=True) + jnp.sum(p_n, axis=-1, keepdims=True)
        o_wide = jnp.where(sel, jnp.dot(p_c.astype(BF16), vcat, preferred_element_type=F32), 0.0)
        o_new = jnp.where(selq, jnp.dot(p_n.astype(BF16), vnp, preferred_element_type=F32), 0.0)
        o = o_new[:, 0:A_HEAD_DIM]
        for c in range(1, A_HEADS):
            o = o + o_new[:, c * A_HEAD_DIM:(c + 1) * A_HEAD_DIM]
        for c in range(wide // A_HEAD_DIM):
            o = o + o_wide[:, c * A_HEAD_DIM:(c + 1) * A_HEAD_DIM]
        o = o / l
        lse = m + jnp.log(l)
        lane = lax.broadcasted_iota(jnp.int32, (t_new, LANE), 1)
        lse_out = jnp.zeros((t_new, LANE), F32)
        for h in range(A_HEADS):
            hs = slice(h * A_HEAD_DIM, (h + 1) * A_HEAD_DIM)
            o_ref[:, hs] = o[h * t_new:(h + 1) * t_new]
            lse_out = jnp.where(lane == h, lse[h * t_new:(h + 1) * t_new], lse_out)
        lse_ref[...] = lse_out


def _attn_sample(z_s, caches, layer, *, n_seq, t_new):
    in_specs = [pl.BlockSpec((t_new, IN_WIDTH), lambda b: (b, 0))]
    views, scratch = [], []
    per_res = 2 * A_HEADS
    for g, c in enumerate(caches):
        win, d = A_PATTERNS[g]
        assert c.shape[2] == win == N_STEPS * d, "cache must hold one full window"
        n_res = min(d, t_new)
        views.append(c.reshape(c.shape[0], n_seq, N_STEPS, d * per_res, A_HEAD_DIM))
        in_specs.append(pl.BlockSpec((None, None, N_STEPS, n_res * per_res, A_HEAD_DIM),
                                     lambda b, layer=layer: (layer, b, 0, 0, 0)))
        scratch.append(pltpu.VMEM((N_STEPS * _gather_pitch(n_res * per_res), A_HEAD_DIM), F32))
    out_specs = [pl.BlockSpec((t_new, A_WIDTH), lambda b: (b, 0)),
                 pl.BlockSpec((t_new, LANE), lambda b: (b, 0))] * A_GROUPS
    out_shape = [jax.ShapeDtypeStruct((n_seq * t_new, A_WIDTH), F32),
                 jax.ShapeDtypeStruct((n_seq * t_new, LANE), F32)] * A_GROUPS
    outs = pl.pallas_call(
        functools.partial(_attn_sample_kernel, t_new=t_new),
        grid=(n_seq,),
        in_specs=in_specs,
        out_specs=out_specs,
        out_shape=out_shape,
        scratch_shapes=scratch,
        compiler_params=_cparams(("parallel",)),
        name="attn_sample",
    )(z_s, *views)
    return outs[0::2], outs[1::2]


def _hgrn_consts(seg):
    n = HGRN_ROWS
    t = np.arange(n)[:, None]
    u = np.arange(n)[None, :]
    seg_lo = (t // seg) * seg
    seg_hi = seg_lo + seg - 1
    mats = [(u >= seg_lo) & (u <= t),
            (u > t) & (u <= seg_hi)]
    level = np.full((n, n), -1, np.int32)
    s = u
    li = 0
    m = seg
    while m >= 2:
        half = m // 2
        upper = (t % m) >= half
        ref = (t // m) * m + half - 1
        if m > 2:
            mats.append(np.where(upper, (u > ref) & (u <= t), (u > t) & (u <= ref)))
        pair = (t // m == s // m) & ((t % m) >= half) & ((s % m) < half)
        level[pair] = li
        li += 1
        m //= 2
    level[np.arange(n), np.arange(n)] = li
    w = np.concatenate(mats, axis=0).astype(np.float32)
    w2 = np.concatenate([w, w], axis=1)
    return jnp.asarray(w2, BF16), jnp.asarray(level), li


def _hgrn_blocks(items, w2, level, n_lev, seg):
    n = HGRN_ROWS
    n_seg = n // seg
    row = lax.broadcasted_iota(jnp.int32, (n, B_DIM), 0)

    gates = []
    for q, xf, v, lb, states in items:
        f = lb + (1.0 - lb) * _sigmoid(xf)
        g = jnp.log2(f)
        g_hi = g.astype(BF16)
        g_lo = (g - g_hi.astype(F32)).astype(BF16)
        gates.append((f, jnp.concatenate([g_hi, g_lo], axis=0)))
    stage1 = []
    for i in range(0, len(items), 2):
        pair = gates[i:i + 2]
        rhs = pair[0][1] if len(pair) == 1 else jnp.concatenate([pair[0][1], pair[1][1]], axis=1)
        dsum = jnp.dot(w2, rhs, preferred_element_type=F32)
        for k, (f, _) in enumerate(pair):
            stage1.append((f, 1.0 - f, dsum[:, k * B_DIM:(k + 1) * B_DIM]))

    stage2 = []
    for (q, xf, v, lb, states), (f, kin, dsum) in zip(items, stage1):
        fac = jnp.exp2(dsum)
        ea = fac[0:n]
        ops = []
        m = seg
        for li in range(n_lev):
            half = m // 2
            if half % 8 == 0:
                src = jnp.concatenate([(kin if (r0 // half) % 2 == 0 else q)[r0:r0 + half]
                                       for r0 in range(0, n, half)], axis=0)
                tm = src * fac[(2 + li) * n:(3 + li) * n]
            elif m > 2:
                tm = jnp.where((row & half) != 0, q, kin) * fac[(2 + li) * n:(3 + li) * n]
            else:
                tm = jnp.where((row & half) != 0, q * f, kin)
            ops.append(tm.astype(BF16))
            m //= 2
        stage2.append((ea, (q * ea).astype(BF16), kin * fac[n:2 * n], ops))

    stage3 = []
    in_level = [level == li for li in range(n_lev + 1)]
    for (q, xf, v, lb, states), (f, kin, dsum), (ea, qt, kh, ops) in zip(items, stage1, stage2):
        att = jnp.zeros((n, n), F32)
        for li, tm in enumerate(ops):
            att = jnp.where(in_level[li], _nt(tm, tm), att)
        att = jnp.where(in_level[n_lev], _nt(q.astype(BF16), kin.astype(BF16)), att)
        stage3.append(att.astype(BF16))

    results = []
    for (q, xf, v, lb, states), (ea, qt, kh, ops), att in zip(items, stage2, stage3):
        vb = v.astype(BF16)
        o = jnp.dot(att, vb, preferred_element_type=F32)
        new_states = []
        for si in range(n_seg):
            st = states[si]
            o_s = _nt(qt, st.astype(BF16))
            if n_seg == 1:
                o = o + o_s
                kh_s = kh
            else:
                in_seg = (row >= si * seg) & (row < (si + 1) * seg)
                o = o + jnp.where(in_seg, o_s, 0.0)
                kh_s = jnp.where(in_seg, kh, 0.0)
            decay = ea[(si + 1) * seg - 1:(si + 1) * seg, :]
            new_states.append(st * decay + _tn(vb, kh_s.astype(BF16)))
        results.append((o, new_states))
    return results


def _lower_bound(lbl_ref, layer):
    lg = lbl_ref[...]
    e = jnp.exp(lg - jnp.max(lg, axis=0, keepdims=True))
    p = e / jnp.sum(e, axis=0, keepdims=True)
    return jnp.sum(p[0:layer + 1], axis=0, keepdims=True) - p[0:1]


def _hgrn_prompt_kernel(q_ref, f_ref, v_ref, lbl_ref, gn_ref, w3_ref, lev_ref, o_ref, s_ref,
                        st_ref, *, layer, n_lev, tl, n_heads):
    li = pl.program_id(2)

    @pl.when(li == 0)
    def _():
        st_ref[...] = jnp.zeros_like(st_ref)

    lb = _lower_bound(lbl_ref, layer)
    w3 = w3_ref[...]
    level = lev_ref[...]
    gain = gn_ref[...]
    head = [slice(hh * B_DIM, (hh + 1) * B_DIM) for hh in range(n_heads)]

    def body(c, carry):
        r0 = pl.multiple_of(c * HGRN_ROWS, HGRN_ROWS)
        rows = pl.ds(r0, HGRN_ROWS)
        items = [(q_ref[0, rows, hs].astype(F32), f_ref[0, rows, hs].astype(F32),
                  v_ref[0, rows, hs].astype(F32), lb[:, hs], [st_ref[hh]])
                 for hh, hs in enumerate(head)]
        for hh, (o, (st,)) in enumerate(_hgrn_blocks(items, w3, level, n_lev, HGRN_ROWS)):
            st_ref[hh] = st
            o_ref[0, rows, head[hh]] = _rmsnorm_rows(o, gain).astype(o_ref.dtype)
        return carry

    lax.fori_loop(0, tl // HGRN_ROWS, body, 0, unroll=2)

    @pl.when(li == pl.num_programs(2) - 1)
    def _():
        for hh in range(n_heads):
            s_ref[0, hh] = st_ref[hh].T


def _hgrn_prompt(z3, col, lb_logits, hgrn_gain, layer, *, tl):
    b, s, _ = z3.shape
    w3, level, n_lev = _hgrn_consts(HGRN_ROWS)
    nh = HGRN_HEADS_PER_STEP
    wblk = nh * B_DIM

    def cspec(name):
        c0 = col[name] // wblk
        return pl.BlockSpec((1, tl, wblk), lambda bb, h, i: (bb, i, c0 + h))

    const = lambda shape: pl.BlockSpec(shape, lambda bb, h, i: (0,) * len(shape),
                                       pipeline_mode=pl.Buffered(1))
    return pl.pallas_call(
        functools.partial(_hgrn_prompt_kernel, layer=layer, n_lev=n_lev, tl=tl, n_heads=nh),
        grid=(b, B_HEADS // nh, s // tl),
        in_specs=[cspec("bq"), cspec("bf"), cspec("bi"),
                  pl.BlockSpec((lb_logits.shape[0], wblk), lambda bb, h, i: (0, h)),
                  const((1, B_DIM)), const(w3.shape), const(level.shape)],
        out_specs=[pl.BlockSpec((1, tl, wblk), lambda bb, h, i: (bb, i, h)),
                   pl.BlockSpec((1, nh, B_DIM, B_DIM), lambda bb, h, i: (bb, h, 0, 0))],
        out_shape=[jax.ShapeDtypeStruct((b, s, B_WIDTH), BF16),
                   jax.ShapeDtypeStruct((b, B_HEADS, B_DIM, B_DIM), F32)],
        scratch_shapes=[pltpu.VMEM((nh, B_DIM, B_DIM), F32)],
        compiler_params=_cparams(("parallel", "parallel", "arbitrary")),
        name="hgrn_prompt",
    )(z3, z3, z3, lb_logits, hgrn_gain.reshape(1, B_DIM), w3, level)


def _hgrn_sample_kernel(q_ref, f_ref, v_ref, s0_ref, lbl_ref, gn_ref, w3_ref, lev_ref, acc_ref,
                        o_ref, s_ref, *, layer, n_lev, t_new, n_heads):
    del acc_ref
    lb = _lower_bound(lbl_ref, layer)
    n_seg = HGRN_ROWS // t_new
    head = [slice(hh * B_DIM, (hh + 1) * B_DIM) for hh in range(n_heads)]
    items = [(q_ref[:, hs], f_ref[:, hs], v_ref[:, hs], lb[:, hs],
              [s0_ref[0, si, hh].T for si in range(n_seg)])
             for hh, hs in enumerate(head)]
    results = _hgrn_blocks(items, w3_ref[...], lev_ref[...], n_lev, t_new)
    for hh, (o, new_states) in enumerate(results):
        o_ref[:, head[hh]] = _rmsnorm_rows(o, gn_ref[...]).astype(o_ref.dtype)
        for si in range(n_seg):
            s_ref[0, si, hh] = new_states[si].T


def _hgrn_sample(z_s, state, new_state, lb_logits, hgrn_gain, layer, *, n_seq, t_new):
    w3, level, n_lev = _hgrn_consts(t_new)
    n_seg = HGRN_ROWS // t_new
    nh = HGRN_SAMPLE_HEADS_PER_STEP
    wblk = nh * B_DIM

    def cspec(name):
        c0 = _COL_S[name] // wblk
        return pl.BlockSpec((HGRN_ROWS, wblk), lambda i, h: (i, c0 + h))

    const = lambda shape: pl.BlockSpec(shape, lambda i, h: (0,) * len(shape))
    return pl.pallas_call(
        functools.partial(_hgrn_sample_kernel, layer=layer, n_lev=n_lev, t_new=t_new, n_heads=nh),
        grid=(n_seq // n_seg, B_HEADS // nh),
        in_specs=[cspec("bq"), cspec("bf"), cspec("bi"),
                  pl.BlockSpec((1, n_seg, nh, B_DIM, B_DIM), lambda i, h, layer=layer: (layer, i, h, 0, 0)),
                  pl.BlockSpec((lb_logits.shape[0], wblk), lambda i, h: (0, h)),
                  const((1, B_DIM)), const(w3.shape), const(level.shape),
                  pl.BlockSpec(memory_space=pl.ANY)],
        out_specs=[pl.BlockSpec((HGRN_ROWS, wblk), lambda i, h: (i, h)),
                   pl.BlockSpec((1, n_seg, nh, B_DIM, B_DIM), lambda i, h, layer=layer: (layer, i, h, 0, 0))],
        out_shape=[jax.ShapeDtypeStruct((n_seq * t_new, B_WIDTH), BF16),
                   jax.ShapeDtypeStruct(new_state.shape, F32)],
        input_output_aliases={8: 1},
        compiler_params=_cparams(("parallel", "parallel")),
        name="hgrn_sample",
    )(z_s, z_s, z_s, state, lb_logits, hgrn_gain.reshape(1, B_DIM), w3, level, new_state)


_POOL_PAD = 16


def _pool_mix(ext_at, u, pos, pw_ref, ps_ref):
    outs = []
    for g, win in enumerate(C_WINDOWS):
        gs = slice(g * C_GROUP_DIM, (g + 1) * C_GROUP_DIM)
        acc = u[:, gs]
        for k in range(1, win):
            acc = acc + ext_at(k, gs)
        cnt = jnp.minimum(pos + 1, win).astype(F32)
        pooled = acc * (1.0 / cnt) - u[:, gs]
        mixed = jnp.dot(pooled.astype(BF16), pw_ref[g].astype(BF16), preferred_element_type=F32)
        outs.append(mixed * ps_ref[:, gs])
    return outs


def _pool_prompt_kernel(u_ref, h_ref, pw_ref, ps_ref, o_ref, ext_ref, *, tl):
    i = pl.program_id(1)
    halo = h_ref[0].astype(F32)
    ext_ref[0:_POOL_PAD, :] = jnp.where(i > 0, halo, 0.0)
    u = u_ref[0].astype(F32)
    ext_ref[_POOL_PAD:, :] = u
    pos = i * tl + lax.broadcasted_iota(jnp.int32, (tl, 1), 0)
    outs = _pool_mix(lambda k, gs: ext_ref[_POOL_PAD - k:_POOL_PAD - k + tl, gs], u, pos, pw_ref, ps_ref)
    for g, og in enumerate(outs):
        o_ref[0, :, g * C_GROUP_DIM:(g + 1) * C_GROUP_DIM] = og.astype(o_ref.dtype)


def _pool_prompt(z3, col, pool_w, pool_scale, *, tl):
    b, s, _ = z3.shape
    c0 = col["cu"] // C_WIDTH
    per = tl // _POOL_PAD
    return pl.pallas_call(
        functools.partial(_pool_prompt_kernel, tl=tl),
        grid=(b, s // tl),
        in_specs=[pl.BlockSpec((1, tl, C_WIDTH), lambda bb, i: (bb, i, c0)),
                  pl.BlockSpec((1, _POOL_PAD, C_WIDTH), lambda bb, i: (bb, jnp.maximum(i * per - 1, 0), c0)),
                  pl.BlockSpec((C_GROUPS, C_GROUP_DIM, C_GROUP_DIM), lambda bb, i: (0, 0, 0)),
                  pl.BlockSpec((1, C_WIDTH), lambda bb, i: (0, 0))],
        out_specs=pl.BlockSpec((1, tl, C_WIDTH), lambda bb, i: (bb, i, 0)),
        out_shape=jax.ShapeDtypeStruct((b, s, C_WIDTH), BF16),
        scratch_shapes=[pltpu.VMEM((_POOL_PAD + tl, C_WIDTH), F32)],
        compiler_params=_cparams(("parallel", "arbitrary")),
        name="pool_prompt",
    )(z3, z3, pool_w, pool_scale.reshape(1, C_WIDTH))


def _pool_sample_kernel(u_ref, h_ref, pw_ref, ps_ref, o_ref, ext_ref, *, nb, t_new, first_pos):
    ext_ref[:, 0:1, :] = jnp.zeros((nb, 1, C_WIDTH), F32)
    ext_ref[:, 1:_POOL_PAD, :] = h_ref[0]
    u3 = u_ref[...].reshape(nb, t_new, C_WIDTH)
    ext_ref[:, _POOL_PAD:, :] = u3
    u = u_ref[...]
    pos = first_pos + lax.broadcasted_iota(jnp.int32, (nb * t_new, 1), 0) % t_new

    def ext_at(k, gs):
        return ext_ref[:, _POOL_PAD - k:_POOL_PAD - k + t_new, gs].reshape(nb * t_new, C_GROUP_DIM)

    outs = _pool_mix(ext_at, u, pos, pw_ref, ps_ref)
    for g, og in enumerate(outs):
        o_ref[:, g * C_GROUP_DIM:(g + 1) * C_GROUP_DIM] = og.astype(o_ref.dtype)


def _pool_sample(z_s, state_pool, pool_w, pool_scale, layer, *, n_seq, t_new, first_pos, nb):
    c0 = _COL_S["cu"] // C_WIDTH
    return pl.pallas_call(
        functools.partial(_pool_sample_kernel, nb=nb, t_new=t_new, first_pos=first_pos),
        grid=(n_seq // nb,),
        in_specs=[pl.BlockSpec((nb * t_new, C_WIDTH), lambda i: (i, c0)),
                  pl.BlockSpec((1, nb, C_HIST, C_WIDTH), lambda i, layer=layer: (layer, i, 0, 0)),
                  pl.BlockSpec((C_GROUPS, C_GROUP_DIM, C_GROUP_DIM), lambda i: (0, 0, 0)),
                  pl.BlockSpec((1, C_WIDTH), lambda i: (0, 0))],
        out_specs=pl.BlockSpec((nb * t_new, C_WIDTH), lambda i: (i, 0)),
        out_shape=jax.ShapeDtypeStruct((n_seq * t_new, C_WIDTH), BF16),
        scratch_shapes=[pltpu.VMEM((nb, _POOL_PAD + t_new, C_WIDTH), F32)],
        compiler_params=_cparams(("parallel",)),
        name="pool_sample",
    )(z_s, state_pool, pool_w, pool_scale.reshape(1, C_WIDTH))


def _token_order(ref, scr_ref, d, tm):
    n_chunk = ref.shape[-1] // LANE
    if d == 1:
        return [ref[0, 0, :, c * LANE:(c + 1) * LANE] for c in range(n_chunk)]
    for r in range(d):
        for c in range(n_chunk):
            scr_ref[c, pl.ds(r, tm // d, stride=d), :] = ref[0, r, :, c * LANE:(c + 1) * LANE]
    return [scr_ref[c] for c in range(n_chunk)]


def _out_proj_kernel(x_ref, ag_ref, bg_ref, cg_ref, m0_ref, m1_ref, m2_ref, o0_ref, o1_ref, o2_ref,
                     l0_ref, l1_ref, l2_ref, ob_ref, oc_ref, wa_ref, wb_ref, wc_ref, wo_ref,
                     gp_ref, y_ref, *scratch, dil, tm):
    scr = iter(scratch)
    o_g, l_g = [], []
    for o_ref, l_ref, d in zip((o0_ref, o1_ref, o2_ref), (l0_ref, l1_ref, l2_ref), dil):
        so = next(scr) if d > 1 else None
        sl = next(scr) if d > 1 else None
        o_g.append(_token_order(o_ref, so, d, tm))
        l_g.append(_token_order(l_ref, sl, d, tm)[0])
    heads = []
    for h in range(A_HEADS):
        l0, l1, l2 = (l[:, h:h + 1] for l in l_g)
        mx = jnp.maximum(jnp.maximum(l0, l1), l2)
        e0, e1, e2 = jnp.exp(l0 - mx), jnp.exp(l1 - mx), jnp.exp(l2 - mx)
        inv = 1.0 / (e0 + e1 + e2)
        heads.append((e0 * inv) * o_g[0][h] + (e1 * inv) * o_g[1][h] + (e2 * inv) * o_g[2][h])
    o_a = jnp.concatenate(heads, axis=1)

    def silu_gated(o, g_ref):
        h = g_ref[...].astype(F32) * 0.5
        return (o.astype(F32) * (h + h * jnp.tanh(h))).astype(BF16)

    def merge_gated(m_ref, y, w_ref):
        p = jnp.dot(y, w_ref[...], preferred_element_type=F32)
        return p + jnp.tanh(m_ref[...].astype(F32) * 0.5) * p

    merged = (merge_gated(m0_ref, silu_gated(o_a, ag_ref), wa_ref)
              + merge_gated(m1_ref, silu_gated(ob_ref[...], bg_ref), wb_ref)
              + merge_gated(m2_ref, silu_gated(oc_ref[...], cg_ref), wc_ref)) * 0.5
    out = jnp.dot(merged.astype(BF16), wo_ref[...], preferred_element_type=F32)
    y_ref[...] = x_ref[...] + _rmsnorm_rows(out, gp_ref[...])


def _out_proj(x, z, col, o_g, lse_g, o_b, o_c, wa, wb, wc, wo, gain_post, *, tm):
    n = x.shape[0]
    dil = tuple(int(a.shape[1]) for a in o_g)
    tpb = (n // o_g[0].shape[0]) // tm

    def zcol(name, width, k=0):
        assert col[name] % width == 0
        c0 = col[name] // width + k
        return pl.BlockSpec((tm, width), lambda i: (i, c0))

    def gspec(d, width):
        return pl.BlockSpec((1, d, tm // d, width), lambda i: (i // tpb, 0, i % tpb, 0))

    rows = lambda width: pl.BlockSpec((tm, width), lambda i: (i, 0))
    const = lambda a: pl.BlockSpec(a.shape, lambda i: (0,) * a.ndim, pipeline_mode=pl.Buffered(1))
    gp = gain_post.reshape(1, D_MODEL)
    scratch = []
    for d in dil:
        if d > 1:
            scratch += [pltpu.VMEM((A_WIDTH // LANE, tm, LANE), F32), pltpu.VMEM((1, tm, LANE), F32)]
    return pl.pallas_call(
        functools.partial(_out_proj_kernel, dil=dil, tm=tm),
        grid=(n // tm,),
        in_specs=[rows(D_MODEL), zcol("ag", A_WIDTH), zcol("bg", B_WIDTH), zcol("cg", C_WIDTH),
                  zcol("mg", D_MODEL, 0), zcol("mg", D_MODEL, 1), zcol("mg", D_MODEL, 2)]
                 + [gspec(d, A_WIDTH) for d in dil] + [gspec(d, LANE) for d in dil]
                 + [rows(B_WIDTH), rows(C_WIDTH)]
                 + [const(wa), const(wb), const(wc), const(wo), const(gp)],
        out_specs=rows(D_MODEL),
        out_shape=jax.ShapeDtypeStruct((n, D_MODEL), F32),
        scratch_shapes=scratch,
        compiler_params=_cparams(("parallel",)),
        name="out_proj",
    )(x, z, z, z, z, z, z, *o_g, *lse_g, o_b, o_c, wa, wb, wc, wo, gp)


def _ref_cols(w, name, g=None):
    off = _REF_OFF[name]
    if g is None:
        return w[:, off:off + _WIDTHS[name]]
    return w[:, off + g * A_WIDTH:off + (g + 1) * A_WIDTH]


def _prep_w_in(w):
    parts = []
    for g in range(1, A_GROUPS):
        parts += [_ref_cols(w, "ak", g), _ref_cols(w, "av", g), _ref_cols(w, "aq", g)]
    parts += [_ref_cols(w, "ak", 0), _ref_cols(w, "av", 0), _ref_cols(w, "cu"), _ref_cols(w, "cg"),
              _ref_cols(w, "aq", 0)] + [_ref_cols(w, k) for k in ("ag", "bq", "bf", "bi", "bg", "mg")]
    return jnp.concatenate(parts, axis=1).astype(BF16)


def _tile_plan(s_p, n_s):
    attn = []
    for _, d in A_PATTERNS:
        tq = min(1024, s_p // d)
        attn.append((tq, max(1, min(d, 1024 // tq))))
    return dict(
        in_proj_p=(min(2048, s_p), A_WIDTH),
        in_proj_s=(min(1024, n_s), 1024),
        perm=min(1024, s_p),
        kv_rows=1024,
        attn=attn,
        hgrn=min(2048, s_p),
        pool=min(2048, s_p),
        out_p=min(512, s_p),
        out_s=min(256, n_s),
    )


def _kv_split(kv):
    return kv.reshape(kv.shape[:-1] + (2, A_HEADS, A_HEAD_DIM))


def kernel(x_prompt, x_sample, cache_kv_w128, cache_kv_w512, cache_kv_w2048, state_hgrn, state_pool,
           norm_pre, norm_post, w_in, hgrn_lb_logits, hgrn_norm, pool_w, pool_scale,
           w_br_a, w_br_b, w_br_c, w_out):
    depth = w_in.shape[0]
    b_p, s_p, _ = x_prompt.shape
    n_seq, t_new, _ = x_sample.shape
    caches = (cache_kv_w128, cache_kv_w512, cache_kv_w2048)
    past_len = cache_kv_w2048.shape[2]
    assert HGRN_ROWS % t_new == 0 and n_seq % (HGRN_ROWS // t_new) == 0
    n_p, n_s = b_p * s_p, n_seq * t_new

    hp = x_prompt.reshape(n_p, D_MODEL)
    hs = x_sample.reshape(n_s, D_MODEL)
    kv_p = [[] for _ in range(A_GROUPS)]
    kv_s = [[] for _ in range(A_GROUPS)]
    hg_p, pl_p, pl_s = [], [], []
    hg_s = jnp.zeros(state_hgrn.shape, F32)
    tiles = _tile_plan(s_p, n_s)
    for l in range(depth):
        w_l = _prep_w_in(w_in[l])
        wa, wb, wc, wo = (w[l].astype(BF16) for w in (w_br_a, w_br_b, w_br_c, w_out))

        (z_p,) = _in_proj(hp, norm_pre[l], w_l, col0=_P_COL0, tm=tiles["in_proj_p"][0],
                          tn=tiles["in_proj_p"][1], z_dtype=BF16, f32_cols=0)
        z3 = z_p.reshape(b_p, s_p, _P_WIDTH)

        (u_tail,) = _in_proj(hp, norm_pre[l], w_l, col0=_COL_S["cu"], width=C_WIDTH,
                             n_rows=b_p * _POOL_PAD, row_block=_tail_row_block(s_p, _POOL_PAD, _POOL_PAD),
                             tm=_POOL_PAD, tn=C_WIDTH, z_dtype=F32, f32_cols=0)
        pl_p.append(u_tail.reshape(b_p, _POOL_PAD, C_WIDTH)[:, _POOL_PAD - C_HIST:])
        o_g, lse_g = [], []
        for g, (win, d) in enumerate(A_PATTERNS):
            rows = min(win, s_p)
            kv = _kv_rows(hp, norm_pre[l], w_l, _attn_cols(g)[0], n_seq=b_p, seq=s_p, rows=rows,
                          tm=min(tiles["kv_rows"], rows))
            kv_p[g].append(kv.reshape(b_p, rows, 2, A_HEADS, A_HEAD_DIM))
            if g == 0:
                arr = z_p.reshape(b_p, 1, s_p, _P_WIDTH)
                cols = (_COL_P["aq"] // A_WIDTH, _COL_P["ak"] // A_WIDTH, _COL_P["av"] // A_WIDTH)
            else:
                arr = _in_proj_perm(hp, norm_pre[l], w_l, g, b=b_p, s=s_p, d=d, tm=tiles["perm"])
                cols = (2, 0, 1)
            tq, rpb = tiles["attn"][g]
            o, lse = _attn_prompt(arr, cols, tq=tq, rpb=rpb)
            o_g.append(o)
            lse_g.append(lse)
        o_b, s_new = _hgrn_prompt(z3, _COL_P, hgrn_lb_logits, hgrn_norm[l], l, tl=tiles["hgrn"])
        hg_p.append(s_new)
        o_c = _pool_prompt(z3, _COL_P, pool_w[l], pool_scale[l], tl=tiles["pool"])
        hp = _out_proj(hp, z_p, _COL_P, o_g, lse_g, o_b.reshape(n_p, B_WIDTH),
                       o_c.reshape(n_p, C_WIDTH), wa, wb, wc, wo, norm_post[l], tm=tiles["out_p"])

        (z_s,) = _in_proj(hs, norm_pre[l], w_l, col0=0, tm=tiles["in_proj_s"][0],
                          tn=tiles["in_proj_s"][1], z_dtype=F32, f32_cols=0)
        o_g, lse_g = _attn_sample(z_s, caches, l, n_seq=n_seq, t_new=t_new)
        zs3 = z_s.reshape(n_seq, t_new, IN_WIDTH)
        for g in range(A_GROUPS):
            k_off, v_off, _ = _attn_cols(g)
            kv_s[g].append(_kv_split(zs3[:, :, k_off:v_off + A_WIDTH]))
        o_b, hg_s = _hgrn_sample(z_s, state_hgrn, hg_s, hgrn_lb_logits, hgrn_norm[l], l,
                                 n_seq=n_seq, t_new=t_new)
        o_c = _pool_sample(z_s, state_pool, pool_w[l], pool_scale[l], l, n_seq=n_seq,
                           t_new=t_new, first_pos=past_len, nb=HGRN_ROWS // t_new)
        u_new = zs3[:, :, _COL_S["cu"]:_COL_S["cu"] + C_WIDTH]
        pl_s.append(jnp.concatenate([state_pool[l], u_new], axis=1)[:, -C_HIST:])
        as4 = lambda a: a.reshape(1, 1, n_s, a.shape[-1])
        hs = _out_proj(hs, z_s, _COL_S, [as4(a) for a in o_g], [as4(a) for a in lse_g], o_b, o_c,
                       wa, wb, wc, wo, norm_post[l], tm=tiles["out_s"])

    return (hp.reshape(b_p, s_p, D_MODEL), hs.reshape(n_seq, t_new, D_MODEL),
            jnp.stack(kv_p[0]), jnp.stack(kv_p[1]), jnp.stack(kv_p[2]),
            jnp.stack(kv_s[0]), jnp.stack(kv_s[1]), jnp.stack(kv_s[2]),
            jnp.stack(hg_p), hg_s, jnp.stack(pl_p), jnp.stack(pl_s))
```
